```python
import jax
import jax.numpy as jnp
from jax import lax
import numpy as np

D_MODEL = 1024
BATCH = 2
SEQ = 8192
DEPTH = 1
DEC_BATCH = 128
DEC_SEQ = 4
PAST_LEN = 8192
PAGE_SIZE = 128

HEAD_DIM = 128
N_ATT_HEADS = D_MODEL // (2 * HEAD_DIM)
ATT_WIDTH = N_ATT_HEADS * HEAD_DIM
GDN_DK = 128
GDN_DV = 128
N_GDN_HEADS = (D_MODEL - ATT_WIDTH) // GDN_DV
GDN_QK_WIDTH = N_GDN_HEADS * GDN_DK
GDN_V_WIDTH = N_GDN_HEADS * GDN_DV
MIX_WIDTH = ATT_WIDTH + GDN_V_WIDTH
N_IDX_HEADS = 8
IDX_DIM = 64
TOPK_MAX = 256
CONV_W = 4
CONV_CH = 2 * GDN_QK_WIDTH + GDN_V_WIDTH
D_FF = 4 * D_MODEL
ROPE_THETA = 10000.0
NORM_EPS = 1e-6
Q_BLOCK = 128
GDN_CHUNK = 64
PROJ_SIZES = (ATT_WIDTH, ATT_WIDTH, ATT_WIDTH, N_IDX_HEADS * IDX_DIM, IDX_DIM, N_IDX_HEADS,
              CONV_CH, GDN_V_WIDTH, N_GDN_HEADS, N_GDN_HEADS)
PROJ_WIDTH = sum(PROJ_SIZES)

kernel_name = 'hybrid_dsa_gated_delta_step'


def split_last(x, sizes):
    out, o = [], 0
    for s in sizes:
        out.append(x[..., o:o + s])
        o += s
    return out


def rms_norm(x, g):
    xf = x.astype(jnp.float32)
    y = xf * lax.rsqrt(jnp.mean(xf * xf, axis=-1, keepdims=True) + NORM_EPS)
    return (y * g.astype(jnp.float32)).astype(x.dtype)


def layer_norm(x, g, b):
    xf = x.astype(jnp.float32)
    mu = jnp.mean(xf, axis=-1, keepdims=True)
    xc = xf - mu
    y = xc * lax.rsqrt(jnp.mean(xc * xc, axis=-1, keepdims=True) + NORM_EPS)
    return (y * g.astype(jnp.float32) + b.astype(jnp.float32)).astype(x.dtype)


def l2_normalize(x):
    return x * lax.rsqrt(jnp.sum(x * x, axis=-1, keepdims=True) + NORM_EPS)


def rope(x, pos):
    half = x.shape[-1] // 2
    inv = jnp.power(ROPE_THETA, -jnp.arange(half, dtype=jnp.float32) / half)
    ang = pos.astype(jnp.float32)[:, None] * inv[None, :]
    cos = jnp.cos(ang)[:, None, :]
    sin = jnp.sin(ang)[:, None, :]
    xf = x.astype(jnp.float32)
    x1, x2 = xf[..., :half], xf[..., half:]
    return jnp.concatenate([x1 * cos - x2 * sin, x2 * cos + x1 * sin], axis=-1).astype(x.dtype)


def causal_conv(xc, w):
    ch = xc.shape[-1]
    return lax.conv_general_dilated(xc, w[:, None, :].astype(xc.dtype), window_strides=(1,),
                                    padding='VALID', dimension_numbers=('NWC', 'WIO', 'NWC'),
                                    feature_group_count=ch)


def indexer_scores(q_idx, w_idx, k_idx):
    s = jnp.einsum('bthd,bsd->bths', q_idx, k_idx, preferred_element_type=jnp.float32)
    return jnp.einsum('bths,bth->bts', jax.nn.relu(s), w_idx.astype(jnp.float32))


def sparse_attend(q, k_sel, v_sel, valid):
    s = jnp.einsum('bthd,btkhd->bthk', q, k_sel, preferred_element_type=jnp.float32) * (q.shape[-1] ** -0.5)
    s = jnp.where(valid[:, :, None, :], s, -jnp.inf)
    p = jax.nn.softmax(s, axis=-1)
    return jnp.einsum('bthk,btkhd->bthd', p.astype(v_sel.dtype), v_sel)


def prompt_sparse_attention(q, k, v, q_idx, w_idx, k_idx):
    B, S, H, D = q.shape
    topk = min(TOPK_MAX, S // 4)
    qb = min(Q_BLOCK, S)
    nb = S // qb
    spos = jnp.arange(S)
    take_rows = jax.vmap(lambda rows, idx: rows[idx])

    def block(i):
        t0 = i * qb
        q_b = lax.dynamic_slice_in_dim(q, t0, qb, axis=1)
        qi_b = lax.dynamic_slice_in_dim(q_idx, t0, qb, axis=1)
        wi_b = lax.dynamic_slice_in_dim(w_idx, t0, qb, axis=1)
        tpos = t0 + jnp.arange(qb)
        sc = indexer_scores(qi_b, wi_b, k_idx)
        sc = jnp.where((spos[None, :] <= tpos[:, None])[None], sc, -jnp.inf)
        _, sel = lax.top_k(sc, topk)
        valid = sel <= tpos[None, :, None]
        return sparse_attend(q_b, take_rows(k, sel), take_rows(v, sel), valid)

    out = lax.map(block, jnp.arange(nb))
    return out.transpose(1, 0, 2, 3, 4).reshape(B, S, H, D)


def sample_sparse_attention(q, k_new, v_new, q_idx, w_idx, kidx_new, cache_k, cache_v, cache_idx_k, page_table):
    Bd, T, H, D = q.shape
    page = cache_k.shape[1]
    past = page_table.shape[1] * page
    L = past + T
    topk = min(TOPK_MAX, L // 4)
    kidx_past = cache_idx_k[page_table].reshape(Bd, past, kidx_new.shape[-1]).astype(kidx_new.dtype)
    k_idx_all = jnp.concatenate([kidx_past, kidx_new], axis=1)
    tpos = past + jnp.arange(T)
    spos = jnp.arange(L)
    sc = indexer_scores(q_idx, w_idx, k_idx_all)
    sc = jnp.where((spos[None, :] <= tpos[:, None])[None], sc, -jnp.inf)
    _, sel = lax.top_k(sc, topk)
    valid = sel <= tpos[None, :, None]
    in_past = sel < past
    sp = jnp.minimum(sel, past - 1)
    bidx = jnp.arange(Bd)[:, None, None]
    phys = page_table[bidx, sp // page]
    slot = sp % page
    sn = jnp.clip(sel - past, 0, T - 1)

    def gather(cache, new):
        return jnp.where(in_past[..., None, None], cache[phys, slot].astype(new.dtype), new[bidx, sn])

    return sparse_attend(q, gather(cache_k, k_new), gather(cache_v, v_new), valid)


def _to_chunks(x, n, c):
    b, _, h = x.shape[:3]
    x = x.reshape((b, n, c, h) + x.shape[3:])
    return x.transpose((1, 0, 3, 2) + tuple(range(4, x.ndim)))


def gated_delta_chunked(q, k, v, g, beta, s0):
    B, T, H, Dk = q.shape
    Dv = v.shape[-1]
    c = min(GDN_CHUNK, T)
    pad = (-T) % c
    if pad:
        padt = lambda a: jnp.pad(a, [(0, 0), (0, pad)] + [(0, 0)] * (a.ndim - 2))
        q, k, v, g, beta = [padt(a) for a in (q, k, v, g, beta)]
    n = (T + pad) // c
    q, k, v, g, beta = [_to_chunks(a, n, c) for a in (q, k, v, g, beta)]
    gc = jnp.cumsum(g, axis=-1)
    ii = jnp.arange(c)
    tril = ii[:, None] >= ii[None, :]
    strict = ii[:, None] > ii[None, :]
    decay = jnp.exp(jnp.where(tril, gc[..., :, None] - gc[..., None, :], -jnp.inf))
    kb = k * beta[..., None]
    lmat = jnp.where(strict, jnp.einsum('nbhid,nbhjd->nbhij', kb, k) * decay, 0.0)
    a_mat = lmat + jnp.eye(c, dtype=lmat.dtype)
    rhs = jnp.concatenate([v * beta[..., None], kb * jnp.exp(gc)[..., None]], axis=-1)
    sol = lax.linalg.triangular_solve(a_mat, rhs, left_side=True, lower=True, unit_diagonal=True)
    u, w = sol[..., :Dv], sol[..., Dv:]
    qk = jnp.einsum('nbhid,nbhjd->nbhij', q, k) * decay

    def step(s, xs):
        q_c, k_c, u_c, w_c, qk_c, g_c = xs
        v_new = u_c - jnp.einsum('bhck,bhkv->bhcv', w_c, s)
        o_c = (jnp.einsum('bhck,bhkv->bhcv', q_c * jnp.exp(g_c)[..., None], s)
               + jnp.einsum('bhij,bhjv->bhiv', qk_c, v_new))
        g_last = g_c[..., -1]
        k_dec = k_c * jnp.exp(g_last[..., None] - g_c)[..., None]
        s = s * jnp.exp(g_last)[..., None, None] + jnp.einsum('bhck,bhcv->bhkv', k_dec, v_new)
        return s, o_c

    s_fin, o = lax.scan(step, s0, (q, k, u, w, qk, gc))
    o = o.transpose(1, 0, 3, 2, 4).reshape(B, n * c, H, Dv)[:, :T]
    return o, s_fin


def decoder_layer(x, pos, conv_buf, s0, attend, lp):
    (w_in, conv_w, idx_k_norm_g, idx_k_norm_b, gdn_a_log, gdn_dt_bias, gdn_norm_g, w_out,
     pre_mix_g, post_mix_g, pre_mlp_g, post_mlp_g, w_mlp_up, w_mlp_down) = lp
    B, T, _ = x.shape
    f32 = jnp.float32
    h = rms_norm(x, pre_mix_g)
    aq, ak, av, iq, ik, iw, qkv, z, gb, ga = split_last(h @ w_in, PROJ_SIZES)
    aq = rope(aq.reshape(B, T, N_ATT_HEADS, HEAD_DIM), pos)
    ak = rope(ak.reshape(B, T, N_ATT_HEADS, HEAD_DIM), pos)
    av = av.reshape(B, T, N_ATT_HEADS, HEAD_DIM)
    iq = rope(iq.reshape(B, T, N_IDX_HEADS, IDX_DIM), pos)
    ik = rope(layer_norm(ik, idx_k_norm_g, idx_k_norm_b)[:, :, None, :], pos)[:, :, 0, :]
    iw = iw * (N_IDX_HEADS ** -0.5 * IDX_DIM ** -0.5)
    att = attend(aq, ak, av, iq, iw, ik)
    xc = jnp.concatenate([conv_buf.astype(qkv.dtype), qkv], axis=1)
    new_conv = xc[:, xc.shape[1] - (CONV_W - 1):]
    qkv_c = jax.nn.silu(causal_conv(xc, conv_w)).astype(f32)
    gq, gk, gv = split_last(qkv_c, (GDN_QK_WIDTH, GDN_QK_WIDTH, GDN_V_WIDTH))
    gq = l2_normalize(gq.reshape(B, T, N_GDN_HEADS, GDN_DK)) * (GDN_DK ** -0.5)
    gk = l2_normalize(gk.reshape(B, T, N_GDN_HEADS, GDN_DK))
    gv = gv.reshape(B, T, N_GDN_HEADS, GDN_DV)
    beta = jax.nn.sigmoid(gb.astype(f32))
    g = -jnp.exp(gdn_a_log.astype(f32)) * jax.nn.softplus(ga.astype(f32) + gdn_dt_bias.astype(f32))
    o, s_new = gated_delta_chunked(gq, gk, gv, g, beta, s0.astype(f32))
    o = rms_norm(o, gdn_norm_g) * jax.nn.silu(z.reshape(B, T, N_GDN_HEADS, GDN_DV).astype(f32))
    mix = jnp.concatenate([att.reshape(B, T, ATT_WIDTH).astype(x.dtype),
                           o.reshape(B, T, GDN_V_WIDTH).astype(x.dtype)], axis=-1) @ w_out
    x = x + rms_norm(mix, post_mix_g)
    ff = jnp.square(jax.nn.relu(rms_norm(x, pre_mlp_g) @ w_mlp_up)) @ w_mlp_down
    x = x + rms_norm(ff, post_mlp_g)
    return x, (ak, av, ik, s_new, new_conv)


def setup_inputs(seed: int = 0) -> dict:
    key = jax.random.key(seed)
    ks = jax.random.split(key, 24)
    f32 = jnp.float32
    n_pages = PAST_LEN // PAGE_SIZE
    n_used = DEC_BATCH * n_pages
    n_pool = (5 * n_used + 3) // 4
    nrm = lambda k, shape, scale: scale * jax.random.normal(k, shape, f32)
    page_table = jax.random.permutation(ks[5], n_pool)[:n_used].reshape(DEC_BATCH, n_pages).astype(jnp.int32)
    return {
        'x_prompt': nrm(ks[0], (BATCH, SEQ, D_MODEL), 1.0),
        'x_sample': nrm(ks[1], (DEC_BATCH, DEC_SEQ, D_MODEL), 1.0),
        'cache_k': nrm(ks[2], (DEPTH, n_pool, PAGE_SIZE, N_ATT_HEADS, HEAD_DIM), 1.0),
        'cache_v': nrm(ks[3], (DEPTH, n_pool, PAGE_SIZE, N_ATT_HEADS, HEAD_DIM), 1.0),
        'cache_idx_k': nrm(ks[4], (DEPTH, n_pool, PAGE_SIZE, IDX_DIM), 1.0),
        'page_table': page_table,
        'state_gdn': nrm(ks[6], (DEPTH, DEC_BATCH, N_GDN_HEADS, GDN_DK, GDN_DV), GDN_DK ** -0.5),
        'state_conv': nrm(ks[7], (DEPTH, DEC_BATCH, CONV_W - 1, CONV_CH), 1.0),
        'w_in': nrm(ks[8], (DEPTH, D_MODEL, PROJ_WIDTH), D_MODEL ** -0.5),
        'conv_w': nrm(ks[9], (DEPTH, CONV_W, CONV_CH), CONV_W ** -0.5),
        'idx_k_norm_g': 1.0 + nrm(ks[10], (DEPTH, IDX_DIM), 0.02),
        'idx_k_norm_b': nrm(ks[11], (DEPTH, IDX_DIM), 0.02),
        'gdn_a_log': jnp.log(jax.random.uniform(ks[12], (DEPTH, N_GDN_HEADS), f32, 1.0, 16.0)),
        'gdn_dt_bias': nrm(ks[13], (DEPTH, N_GDN_HEADS), 0.1),
        'gdn_norm_g': 1.0 + nrm(ks[14], (DEPTH, GDN_DV), 0.02),
        'w_out': nrm(ks[15], (DEPTH, MIX_WIDTH, D_MODEL), MIX_WIDTH ** -0.5),
        'pre_mix_g': 1.0 + nrm(ks[16], (DEPTH, D_MODEL), 0.02),
        'post_mix_g': 1.0 + nrm(ks[17], (DEPTH, D_MODEL), 0.02),
        'pre_mlp_g': 1.0 + nrm(ks[18], (DEPTH, D_MODEL), 0.02),
        'post_mlp_g': 1.0 + nrm(ks[19], (DEPTH, D_MODEL), 0.02),
        'w_mlp_up': nrm(ks[20], (DEPTH, D_MODEL, D_FF), D_MODEL ** -0.5),
        'w_mlp_down': nrm(ks[21], (DEPTH, D_FF, D_MODEL), D_FF ** -0.5),
    }


def reference(x_prompt, x_sample, cache_k, cache_v, cache_idx_k, page_table, state_gdn, state_conv,
              w_in, conv_w, idx_k_norm_g, idx_k_norm_b, gdn_a_log, gdn_dt_bias, gdn_norm_g, w_out,
              pre_mix_g, post_mix_g, pre_mlp_g, post_mlp_g, w_mlp_up, w_mlp_down):
    B, S, _ = x_prompt.shape
    Bd, T, _ = x_sample.shape
    past = page_table.shape[1] * cache_k.shape[2]
    pos_p = jnp.arange(S)
    pos_s = past + jnp.arange(T)
    conv0 = jnp.zeros((B, CONV_W - 1, CONV_CH), x_prompt.dtype)
    s0 = jnp.zeros((B, N_GDN_HEADS, GDN_DK, GDN_DV), jnp.float32)
    yp, ys = x_prompt, x_sample
    outs_p, outs_s = [], []
    for l in range(DEPTH):
        lp = (w_in[l], conv_w[l], idx_k_norm_g[l], idx_k_norm_b[l], gdn_a_log[l], gdn_dt_bias[l],
              gdn_norm_g[l], w_out[l], pre_mix_g[l], post_mix_g[l], pre_mlp_g[l], post_mlp_g[l],
              w_mlp_up[l], w_mlp_down[l])
        yp, st_p = decoder_layer(yp, pos_p, conv0, s0, prompt_sparse_attention, lp)
        ck, cv, ci = cache_k[l], cache_v[l], cache_idx_k[l]
        attend_s = lambda q, k, v, qi, wi, ki, ck=ck, cv=cv, ci=ci: sample_sparse_attention(
            q, k, v, qi, wi, ki, ck, cv, ci, page_table)
        ys, st_s = decoder_layer(ys, pos_s, state_conv[l], state_gdn[l], attend_s, lp)
        outs_p.append(st_p)
        outs_s.append(st_s)
    kp, vp, ikp, gp, cp = [jnp.stack([o[i] for o in outs_p]) for i in range(5)]
    ks_, vs_, iks, gs, cs = [jnp.stack([o[i] for o in outs_s]) for i in range(5)]
    return (yp, ys, kp, vp, ikp, gp, cp, ks_, vs_, iks, gs, cs)
```

```python
import functools
import math

import jax
import jax.numpy as jnp
from jax import lax
from jax.experimental import pallas as pl
from jax.experimental.pallas import tpu as pltpu

F32 = jnp.float32
BF16 = jnp.bfloat16

HEAD_DIM = 128
IDX_DIM = 64
N_IDX_HEADS = 8
GDN_DK = 128
GDN_DV = 128
CONV_W = 4
TOPK_MAX = 256
GDN_CHUNK = 64
ROPE_THETA = 10000.0
NORM_EPS = 1e-6

LANES = 128
SUBLANES = 8
VMEM_LIMIT = 56 * 1024 * 1024
NEG_BIG = -1e30
SM_IW = IDX_DIM
SM_GB = IDX_DIM + N_IDX_HEADS
HIGHEST = lax.Precision.HIGHEST


def _dot(a, b):
    return jnp.dot(a, b, preferred_element_type=F32)


def _dot_nt(a, b):
    return lax.dot_general(a, b, (((1,), (1,)), ((), ())), preferred_element_type=F32)


def _dot_tn(a, b):
    return lax.dot_general(a, b, (((0,), (0,)), ((), ())), preferred_element_type=F32)


def _rms(x, g):
    return x * lax.rsqrt(jnp.mean(x * x, axis=-1, keepdims=True) + NORM_EPS) * g


def _const_spec(shape):
    n = len(shape)
    return pl.BlockSpec(shape, lambda *_: (0,) * n, pipeline_mode=pl.Buffered(1))


def _inproj_kernel(x_ref, g_ref, wa_ref, wqkv_ref, wz_ref, ws_ref,
                   cos_ref, sin_ref, cosi_ref, sina_ref, sinb_ref, lng_ref, lnb_ref, smul_ref,
                   q_ref, kf_ref, vf_ref, kb_ref, vb_ref, iq_ref, ikf_ref, ikb_ref,
                   qkv_ref, z_ref, sm_ref, *, att_w, q_scale):
    x = x_ref[...]
    h = _rms(x, g_ref[...]).astype(BF16)
    a = _dot(h, wa_ref[...])
    cos = cos_ref[...]
    sin = sin_ref[...]
    n_heads = att_w // HEAD_DIM

    def rope_full(xh):
        return xh * cos + pltpu.roll(xh, HEAD_DIM // 2, axis=1) * sin

    for j in range(n_heads):
        sl = slice(j * HEAD_DIM, (j + 1) * HEAD_DIM)
        q_ref[:, sl] = (rope_full(a[:, sl]) * q_scale).astype(BF16)
        kr = rope_full(a[:, att_w + j * HEAD_DIM: att_w + (j + 1) * HEAD_DIM])
        kf_ref[:, sl] = kr
        kb_ref[:, sl] = kr.astype(BF16)
    v = a[:, 2 * att_w:3 * att_w]
    vf_ref[...] = v
    vb_ref[...] = v.astype(BF16)

    cosi = cosi_ref[...]
    sina = sina_ref[...]
    sinb = sinb_ref[...]

    def rope_half(xh):
        return (xh * cosi + pltpu.roll(xh, LANES - IDX_DIM // 2, axis=1) * sina
                + pltpu.roll(xh, IDX_DIM // 2, axis=1) * sinb)

    idx_w = N_IDX_HEADS * IDX_DIM
    for j in range(idx_w // LANES):
        sl = slice(j * LANES, (j + 1) * LANES)
        iq_ref[:, sl] = rope_half(a[:, 3 * att_w + j * LANES: 3 * att_w + (j + 1) * LANES]).astype(BF16)

    qkv_ref[...] = _dot(h, wqkv_ref[...])
    z_ref[...] = _dot(h, wz_ref[...])

    sm = _dot(h, ws_ref[...])
    lane = lax.broadcasted_iota(jnp.int32, sm.shape, 1)
    is_ik = lane < IDX_DIM
    ikraw = jnp.where(is_ik, sm, 0.0)
    mu = jnp.sum(ikraw, axis=-1, keepdims=True) * (1.0 / IDX_DIM)
    xc = jnp.where(is_ik, sm - mu, 0.0)
    var = jnp.sum(xc * xc, axis=-1, keepdims=True) * (1.0 / IDX_DIM)
    ikn = xc * lax.rsqrt(var + NORM_EPS) * lng_ref[...] + lnb_ref[...]
    ikr = rope_half(ikn)
    ikf_ref[...] = ikr[:, :IDX_DIM]
    ikb_ref[...] = ikr[:, :IDX_DIM].astype(BF16)
    sm_ref[...] = sm * smul_ref[...]


def _rope_tables(pos):
    pos = pos.astype(F32)[:, None]
    half = HEAD_DIM // 2
    inv = jnp.power(ROPE_THETA, -jnp.arange(half, dtype=F32) / half)
    ang = pos * inv[None, :]
    c, s = jnp.cos(ang), jnp.sin(ang)
    cos = jnp.concatenate([c, c], axis=-1)
    sin = jnp.concatenate([-s, s], axis=-1)
    halfi = IDX_DIM // 2
    invi = jnp.power(ROPE_THETA, -jnp.arange(halfi, dtype=F32) / halfi)
    angi = pos * invi[None, :]
    ci, si = jnp.cos(angi), jnp.sin(angi)
    zi = jnp.zeros_like(si)
    cosi = jnp.concatenate([ci, ci, ci, ci], axis=-1)
    sina = jnp.concatenate([-si, zi, -si, zi], axis=-1)
    sinb = jnp.concatenate([zi, si, zi, si], axis=-1)
    return cos, sin, cosi, sina, sinb


def _inproj(x2d, tables, lw, *, att_w, conv_ch, gdn_vw, tm):
    n, d = x2d.shape
    idx_w = N_IDX_HEADS * IDX_DIM
    rt = tables[0].shape[0]
    assert n % tm == 0 and rt % tm == 0 and n % rt == 0
    nt = rt // tm
    wa_w = 3 * att_w + idx_w
    row = lambda w: pl.BlockSpec((tm, w), lambda i: (i, 0))
    tab = pl.BlockSpec((tm, LANES), lambda i: (i % nt, 0))
    out_shapes = (
        jax.ShapeDtypeStruct((n, att_w), BF16),
        jax.ShapeDtypeStruct((n, att_w), F32),
        jax.ShapeDtypeStruct((n, att_w), F32),
        jax.ShapeDtypeStruct((n, att_w), BF16),
        jax.ShapeDtypeStruct((n, att_w), BF16),
        jax.ShapeDtypeStruct((n, idx_w), BF16),
        jax.ShapeDtypeStruct((n, IDX_DIM), F32),
        jax.ShapeDtypeStruct((n, IDX_DIM), BF16),
        jax.ShapeDtypeStruct((n, conv_ch), F32),
        jax.ShapeDtypeStruct((n, gdn_vw), F32),
        jax.ShapeDtypeStruct((n, LANES), F32),
    )
    out_specs = (row(att_w), row(att_w), row(att_w), row(att_w), row(att_w), row(idx_w),
                 row(IDX_DIM), row(IDX_DIM), row(conv_ch), row(gdn_vw), row(LANES))
    in_specs = [row(d), _const_spec((1, d)), _const_spec((d, wa_w)), _const_spec((d, conv_ch)),
                _const_spec((d, gdn_vw)), _const_spec((d, LANES)),
                tab, tab, tab, tab, tab,
                _const_spec((1, LANES)), _const_spec((1, LANES)), _const_spec((1, LANES))]
    return pl.pallas_call(
        functools.partial(_inproj_kernel, att_w=att_w, q_scale=HEAD_DIM ** -0.5),
        grid=(n // tm,),
        in_specs=in_specs, out_specs=out_specs, out_shape=out_shapes,
        compiler_params=pltpu.CompilerParams(dimension_semantics=("arbitrary",),
                                             vmem_limit_bytes=VMEM_LIMIT),
        name="inproj",
    )(x2d, lw["pre_mix_g"], lw["wa"], lw["wqkv"], lw["wz"], lw["ws"], *tables,
      lw["lng"], lw["lnb"], lw["smul"])


def _select_threshold(sc_ref, n_cols_blocks, tk, n_valid, row_max, row_min, topk, col_limit):
    rows = sc_ref.shape[0]
    kf = float(topk)

    def count(pred_fn):
        def body(kb, acc):
            k0 = pl.multiple_of(kb * tk, tk)
            m = pred_fn(sc_ref[:, pl.ds(k0, tk)], k0)
            part = m[:, 0:LANES]
            for j in range(1, tk // LANES):
                part = part + m[:, j * LANES:(j + 1) * LANES]
            return acc + part
        acc = lax.fori_loop(0, n_cols_blocks, body, jnp.zeros((rows, LANES), F32))
        return jnp.sum(acc, axis=-1, keepdims=True)

    need = n_valid > kf
    spread = jnp.maximum(row_max - row_min, jnp.abs(row_max) * (2.0 ** -20) + 2.0 ** -100)
    lo0 = jnp.where(need, row_min, NEG_BIG)
    hi0 = row_max + spread
    done0 = jnp.where(need, 0.0, 1.0)

    def cond(c):
        return jnp.min(c[4]) < 0.5

    def step(c):
        lo, hi, c_lo, c_hi, done = c
        mid = 0.5 * lo + 0.5 * hi
        stuck = jnp.logical_or(mid <= lo, mid >= hi)
        cnt = count(lambda x, k0: jnp.where(x >= mid, 1.0, 0.0))
        ge = cnt >= kf
        upd = jnp.logical_and(done < 0.5, jnp.logical_not(stuck))
        up_lo = jnp.logical_and(upd, ge)
        up_hi = jnp.logical_and(upd, jnp.logical_not(ge))
        lo = jnp.where(up_lo, mid, lo)
        c_lo = jnp.where(up_lo, cnt, c_lo)
        hi = jnp.where(up_hi, mid, hi)
        c_hi = jnp.where(up_hi, cnt, c_hi)
        fin = jnp.logical_or(stuck, jnp.logical_and(upd, cnt == kf))
        done = jnp.where(fin, 1.0, done)
        return lo, hi, c_lo, c_hi, done

    lo, hi, c_lo, c_hi, _ = lax.while_loop(
        cond, step, (lo0, hi0, n_valid, jnp.zeros_like(n_valid), done0))

    tied = jnp.logical_and(need, c_lo > kf)

    @pl.when(jnp.max(jnp.where(tied, 1.0, 0.0)) > 0.5)
    def _():
        want = kf - c_hi

        def in_tie(x):
            return jnp.logical_and(x >= lo, x < hi)

        def jstep(_, c):
            jlo, jhi = c
            jm = jnp.floor((jlo + jhi) * 0.5)
            cnt = count(lambda x, k0: jnp.where(
                jnp.logical_and(in_tie(x), (k0 + lax.broadcasted_iota(jnp.int32, x.shape, 1)).astype(F32) <= jm),
                1.0, 0.0))
            ok = cnt >= want
            return jnp.where(ok, jlo, jm), jnp.where(ok, jm, jhi)

        n_it = max(1, math.ceil(math.log2(col_limit + 1)))
        j0 = (jnp.full_like(lo, -1.0), jnp.full_like(lo, float(col_limit - 1)))
        _, jcut = lax.fori_loop(0, n_it, jstep, j0)

        def fix(kb, carry):
            k0 = pl.multiple_of(kb * tk, tk)
            x = sc_ref[:, pl.ds(k0, tk)]
            col = (k0 + lax.broadcasted_iota(jnp.int32, x.shape, 1)).astype(F32)
            drop = jnp.logical_and(tied, jnp.logical_and(in_tie(x), col > jcut))
            sc_ref[:, pl.ds(k0, tk)] = jnp.where(drop, -jnp.inf, x)
            return carry
        lax.fori_loop(0, n_cols_blocks, fix, 0)

    return lo


def _pattn_kernel(q_ref, iq_ref, sm_ref, k_ref, v_ref, ik_ref, o_ref, sc_ref, acc_ref,
                  *, tq, tk, topk, seq, n_heads):
    qi = pl.program_id(1)
    t0 = qi * tq
    nkb = (t0 + tq + tk - 1) // tk
    row_t = t0 + lax.broadcasted_iota(jnp.int32, (tq, 1), 0)
    smv = sm_ref[0]
    iq = iq_ref[0]
    iq_heads = [iq[:, h * IDX_DIM:(h + 1) * IDX_DIM] for h in range(N_IDX_HEADS)]
    w_heads = [smv[:, SM_IW + h:SM_IW + h + 1] for h in range(N_IDX_HEADS)]

    def fold(x, op):
        r = x[:, 0:LANES]
        for j in range(1, tk // LANES):
            r = op(r, x[:, j * LANES:(j + 1) * LANES])
        return r

    def score_block(kb, carry):
        mx, mn = carry
        k0 = pl.multiple_of(kb * tk, tk)
        ikb = ik_ref[0, pl.ds(k0, tk), :]
        acc = jnp.zeros((tq, tk), F32)
        for h in range(N_IDX_HEADS):
            acc = acc + w_heads[h] * jnp.maximum(_dot_nt(iq_heads[h], ikb), 0.0)
        col = k0 + lax.broadcasted_iota(jnp.int32, (tq, tk), 1)
        valid = col <= row_t
        sc_ref[:, pl.ds(k0, tk)] = jnp.where(valid, acc, -jnp.inf)
        mx = jnp.maximum(mx, fold(jnp.where(valid, acc, -jnp.inf), jnp.maximum))
        mn = jnp.minimum(mn, fold(jnp.where(valid, acc, jnp.inf), jnp.minimum))
        return mx, mn

    mx, mn = lax.fori_loop(0, nkb, score_block,
                           (jnp.full((tq, LANES), -jnp.inf, F32), jnp.full((tq, LANES), jnp.inf, F32)))
    row_max = jnp.max(mx, axis=-1, keepdims=True)
    row_min = jnp.min(mn, axis=-1, keepdims=True)
    n_valid = (row_t + 1).astype(F32)
    lo = _select_threshold(sc_ref, nkb, tk, n_valid, row_max, row_min, topk, seq)

    acc_ref[...] = jnp.zeros_like(acc_ref)

    def attend_block(kb, carry):
        ms, ls = carry
        k0 = pl.multiple_of(kb * tk, tk)
        sel = sc_ref[:, pl.ds(k0, tk)] >= lo
        new_ms, new_ls = [], []
        for h in range(n_heads):
            sl = slice(h * HEAD_DIM, (h + 1) * HEAD_DIM)
            s = _dot_nt(q_ref[0, :, sl], k_ref[0, pl.ds(k0, tk), sl])
            s = jnp.where(sel, s, NEG_BIG)
            m_new = jnp.maximum(ms[h], jnp.max(s, axis=-1, keepdims=True))
            p = jnp.exp(s - m_new)
            alpha = jnp.exp(ms[h] - m_new)
            new_ls.append(alpha * ls[h] + jnp.sum(p, axis=-1, keepdims=True))
            acc_ref[:, sl] = alpha * acc_ref[:, sl] + _dot(p.astype(BF16), v_ref[0, pl.ds(k0, tk), sl])
            new_ms.append(m_new)
        return tuple(new_ms), tuple(new_ls)

    m0 = tuple(jnp.full((tq, 1), NEG_BIG, F32) for _ in range(n_heads))
    l0 = tuple(jnp.zeros((tq, 1), F32) for _ in range(n_heads))
    _, ls = lax.fori_loop(0, nkb, attend_block, (m0, l0))
    for h in range(n_heads):
        sl = slice(h * HEAD_DIM, (h + 1) * HEAD_DIM)
        o_ref[0, :, sl] = (acc_ref[:, sl] / ls[h]).astype(o_ref.dtype)


def _prompt_attention(q, iq, sm, kb, vb, ikb, *, tq, tk, topk):
    b, s, att_w = q.shape
    idx_w = iq.shape[-1]
    n_heads = att_w // HEAD_DIM
    assert s % tq == 0 and s % tk == 0 and tk % tq == 0
    blk = lambda w: pl.BlockSpec((1, tq, w), lambda bi, qi: (bi, qi, 0))
    full = lambda w: pl.BlockSpec((1, s, w), lambda bi, qi: (bi, 0, 0), pipeline_mode=pl.Buffered(1))
    return pl.pallas_call(
        functools.partial(_pattn_kernel, tq=tq, tk=tk, topk=topk, seq=s, n_heads=n_heads),
        grid=(b, s // tq),
        in_specs=[blk(att_w), blk(idx_w), blk(LANES), full(att_w), full(att_w), full(IDX_DIM)],
        out_specs=blk(att_w),
        out_shape=jax.ShapeDtypeStruct((b, s, att_w), BF16),
        scratch_shapes=[pltpu.VMEM((tq, s), F32), pltpu.VMEM((tq, att_w), F32)],
        compiler_params=pltpu.CompilerParams(dimension_semantics=("arbitrary", "arbitrary"),
                                             vmem_limit_bytes=VMEM_LIMIT),
        name="prompt_attention",
    )(q, iq, sm, kb, vb, ikb)


def _dsel_kernel(pt_ref, iq_ref, w_ref, iknew_ref, *rest, pg, n_tok, past, topk):
    pages = rest[:pg]
    bias_ref = rest[pg]
    sc_ref = rest[pg + 1]
    c = pl.program_id(1)
    n_c = pl.num_programs(1)
    iq = iq_ref[0]
    w = w_ref[0]
    page = pages[0].shape[1]
    l_pad = sc_ref.shape[1]

    def token_scores(keys_bf16):
        s = jnp.maximum(_dot_nt(iq, keys_bf16), 0.0) * w
        return [jnp.sum(s[t * N_IDX_HEADS:(t + 1) * N_IDX_HEADS, :], axis=0, keepdims=True)
                for t in range(n_tok)]

    @pl.when(c == 0)
    def _():
        sc_ref[...] = jnp.full(sc_ref.shape, -jnp.inf, F32)

    for p in range(pg):
        rows = token_scores(pages[p][0].astype(BF16))
        k0 = pl.multiple_of((c * pg + p) * page, page)
        for t in range(n_tok):
            sc_ref[t:t + 1, pl.ds(k0, page)] = rows[t]

    @pl.when(c == n_c - 1)
    def _():
        rows = token_scores(iknew_ref[0])
        col = lax.broadcasted_iota(jnp.int32, (1, LANES), 1)
        for t in range(n_tok):
            sc_ref[t:t + 1, past:past + LANES] = jnp.where(col <= t, rows[t], -jnp.inf)
        x = sc_ref[...]
        finite = x > -jnp.inf
        row_max = jnp.max(x, axis=-1, keepdims=True)
        row_min = jnp.min(jnp.where(finite, x, jnp.inf), axis=-1, keepdims=True)
        r = lax.broadcasted_iota(jnp.int32, (sc_ref.shape[0], 1), 0)
        n_valid = jnp.where(r < n_tok, (past + 1 + r).astype(F32), 0.0)
        row_max = jnp.where(r < n_tok, row_max, 0.0)
        row_min = jnp.where(r < n_tok, row_min, 0.0)
        lo = _select_threshold(sc_ref, l_pad // LANES, LANES, n_valid, row_max, row_min, topk, l_pad)
        bias_ref[0] = jnp.where(sc_ref[...] >= lo, 0.0, NEG_BIG)


def _decode_select(page_table, iq, w, iknew, cache_idx, *, pg, n_tok, topk):
    bd, n_pages = page_table.shape
    _, page, di = cache_idx.shape
    past = n_pages * page
    assert n_pages % pg == 0
    l_pad = past + LANES
    rows = iq.shape[1]

    def page_spec(p):
        return pl.BlockSpec((1, page, di), lambda b, c, pt: (pt[b, c * pg + p], 0, 0))

    grid_spec = pltpu.PrefetchScalarGridSpec(
        num_scalar_prefetch=1,
        grid=(bd, n_pages // pg),
        in_specs=[pl.BlockSpec((1, rows, di), lambda b, c, pt: (b, 0, 0)),
                  pl.BlockSpec((1, rows, 1), lambda b, c, pt: (b, 0, 0)),
                  pl.BlockSpec((1, LANES, di), lambda b, c, pt: (b, 0, 0))]
                 + [page_spec(p) for p in range(pg)],
        out_specs=pl.BlockSpec((1, SUBLANES, l_pad), lambda b, c, pt: (b, 0, 0)),
        scratch_shapes=[pltpu.VMEM((SUBLANES, l_pad), F32)],
    )
    return pl.pallas_call(
        functools.partial(_dsel_kernel, pg=pg, n_tok=n_tok, past=past, topk=topk),
        grid_spec=grid_spec,
        out_shape=jax.ShapeDtypeStruct((bd, SUBLANES, l_pad), F32),
        compiler_params=pltpu.CompilerParams(dimension_semantics=("arbitrary", "arbitrary"),
                                             vmem_limit_bytes=VMEM_LIMIT),
        name="decode_select",
    )(page_table, iq, w, iknew, *([cache_idx] * pg))


def _dattn_kernel(pt_ref, q_ref, bias_ref, knew_ref, vnew_ref, *rest, pg, n_tok, n_heads, past):
    kpages = rest[:pg]
    vpages = rest[pg:2 * pg]
    o_ref = rest[2 * pg]
    m_ref, l_ref, acc_ref = rest[2 * pg + 1:]
    c = pl.program_id(1)
    n_c = pl.num_programs(1)
    qbd = q_ref[0]
    page = kpages[0].shape[1]

    @pl.when(c == 0)
    def _():
        m_ref[...] = jnp.full(m_ref.shape, NEG_BIG, F32)
        l_ref[...] = jnp.zeros_like(l_ref)
        acc_ref[...] = jnp.zeros_like(acc_ref)

    def update(k_bf16, v_bf16, bias_rows):
        s = _dot_nt(qbd, k_bf16)
        bias = jnp.concatenate([jnp.broadcast_to(b, (SUBLANES, b.shape[1])) for b in bias_rows], axis=0)
        s = s + bias
        m_old = m_ref[...]
        m_new = jnp.maximum(m_old, jnp.max(s, axis=-1, keepdims=True))
        p = jnp.exp(s - m_new)
        alpha = jnp.exp(m_old - m_new)
        l_ref[...] = alpha * l_ref[...] + jnp.sum(p, axis=-1, keepdims=True)
        acc_ref[...] = alpha * acc_ref[...] + _dot(p.astype(BF16), v_bf16)
        m_ref[...] = m_new

    for p in range(pg):
        k0 = pl.multiple_of((c * pg + p) * page, page)
        kp = kpages[p][0].astype(BF16)
        vp = vpages[p][0].astype(BF16)
        update(kp, vp, [bias_ref[0, t:t + 1, pl.ds(k0, page)] for t in range(n_tok)])

    @pl.when(c == n_c - 1)
    def _():
        update(knew_ref[0], vnew_ref[0], [bias_ref[0, t:t + 1, past:past + LANES] for t in range(n_tok)])
        inv_l = 1.0 / l_ref[...]
        for t in range(n_tok):
            for h in range(n_heads):
                r = t * SUBLANES + h
                sl = slice(h * HEAD_DIM, (h + 1) * HEAD_DIM)
                o_ref[0, t:t + 1, sl] = (acc_ref[r:r + 1, sl] * inv_l[r:r + 1, :]).astype(o_ref.dtype)


def _decode_attention(page_table, qbd, bias, knew, vnew, cache_k, cache_v, *, pg, n_tok, n_heads):
    bd, n_pages = page_table.shape
    _, page, att_w = cache_k.shape
    past = n_pages * page
    l_pad = bias.shape[-1]
    rows = qbd.shape[1]
    assert n_pages % pg == 0

    def page_spec(p):
        return pl.BlockSpec((1, page, att_w), lambda b, c, pt: (pt[b, c * pg + p], 0, 0))

    per_seq = lambda r, w: pl.BlockSpec((1, r, w), lambda b, c, pt: (b, 0, 0))
    grid_spec = pltpu.PrefetchScalarGridSpec(
        num_scalar_prefetch=1,
        grid=(bd, n_pages // pg),
        in_specs=[per_seq(rows, att_w), per_seq(SUBLANES, l_pad), per_seq(LANES, att_w), per_seq(LANES, att_w)]
                 + [page_spec(p) for p in range(pg)] * 2,
        out_specs=per_seq(n_tok, att_w),
        scratch_shapes=[pltpu.VMEM((rows, 1), F32), pltpu.VMEM((rows, 1), F32), pltpu.VMEM((rows, att_w), F32)],
    )
    return pl.pallas_call(
        functools.partial(_dattn_kernel, pg=pg, n_tok=n_tok, n_heads=n_heads, past=past),
        grid_spec=grid_spec,
        out_shape=jax.ShapeDtypeStruct((bd, n_tok, att_w), BF16),
        compiler_params=pltpu.CompilerParams(dimension_semantics=("arbitrary", "arbitrary"),
                                             vmem_limit_bytes=VMEM_LIMIT),
        name="decode_attention",
    )(page_table, qbd, bias, knew, vnew, *([cache_k] * pg), *([cache_v] * pg))


def _gdn_kernel(x_ref, z_ref, sm_ref, conv0_ref, s0_ref, cw_ref, alog_ref, dtb_ref, ng_ref,
                o_ref, sfin_ref, convout_ref,
                xbuf, st_ref, q_s, k_s, v_s, beta_s, g_s,
                *, tt, chunk, n_heads, t_valid, n_steps):
    step = pl.program_id(1)
    hist = CONV_W - 1
    qk_w = n_heads * GDN_DK

    @pl.when(step == 0)
    def _():
        xbuf[0:SUBLANES, :] = conv0_ref[0]
        st_ref[...] = s0_ref[0]

    @pl.when(step > 0)
    def _():
        xbuf[0:SUBLANES, :] = xbuf[tt:tt + SUBLANES, :]

    xbuf[SUBLANES:SUBLANES + tt, :] = x_ref[0]
    cw = cw_ref[...]
    conv = cw[hist:hist + 1, :] * xbuf[SUBLANES:SUBLANES + tt, :]
    for j in range(hist):
        conv = conv + cw[j:j + 1, :] * xbuf[SUBLANES - hist + j:SUBLANES - hist + j + tt, :]
    act = conv * jax.nn.sigmoid(conv)

    def l2n(xh):
        return xh * lax.rsqrt(jnp.sum(xh * xh, axis=-1, keepdims=True) + NORM_EPS)

    for h in range(n_heads):
        sl = slice(h * GDN_DK, (h + 1) * GDN_DK)
        q_s[:, sl] = l2n(act[:, sl]) * (GDN_DK ** -0.5)
        k_s[:, sl] = l2n(act[:, qk_w + h * GDN_DK: qk_w + (h + 1) * GDN_DK])
    v_s[...] = act[:, 2 * qk_w:]

    smv = sm_ref[0]
    t_glob = step * tt + lax.broadcasted_iota(jnp.int32, (tt, 1), 0)
    live = t_glob < t_valid
    beta_s[...] = jnp.where(live, jax.nn.sigmoid(smv), 0.0)
    g_s[...] = jnp.where(live, -jnp.exp(alog_ref[...]) * jax.nn.softplus(smv + dtb_ref[...]), 0.0)

    ii = lax.broadcasted_iota(jnp.int32, (chunk, chunk), 0)
    jj = lax.broadcasted_iota(jnp.int32, (chunk, chunk), 1)
    tril = ii >= jj
    strict = ii > jj
    tril_f = jnp.where(tril, 1.0, 0.0)
    eye = jnp.where(ii == jj, 1.0, 0.0)
    lane8 = lax.broadcasted_iota(jnp.int32, (SUBLANES, LANES), 1)
    row8 = lax.broadcasted_iota(jnp.int32, (SUBLANES, LANES), 0)
    g_lane0 = SM_GB + n_heads
    pick_g = jnp.where(lane8 == g_lane0 + row8, 1.0, 0.0)
    n_levels = int(math.log2(chunk))
    ng = ng_ref[...]

    def chunk_body(ci, carry):
        r0 = pl.multiple_of(ci * chunk, chunk)
        rows = pl.ds(r0, chunk)
        gc = jnp.dot(tril_f, g_s[rows, :], precision=HIGHEST, preferred_element_type=F32)
        gcr = lax.dot_general(pick_g, gc, (((1,), (1,)), ((), ())), precision=HIGHEST,
                              preferred_element_type=F32)
        beta = beta_s[rows, :]
        for h in range(n_heads):
            sl = slice(h * GDN_DK, (h + 1) * GDN_DK)
            kh = k_s[rows, sl]
            qh = q_s[rows, sl]
            vh = v_s[rows, sl]
            bcol = beta[:, SM_GB + h:SM_GB + h + 1]
            gcol = gc[:, g_lane0 + h:g_lane0 + h + 1]
            grow = gcr[h:h + 1, :]
            decay = jnp.exp(jnp.where(tril, gcol - grow, -jnp.inf))
            kbh = kh * bcol
            kh_b = kh.astype(BF16)
            lmat = jnp.where(strict, _dot_nt(kbh.astype(BF16), kh_b) * decay, 0.0)
            pw = -lmat
            inv = eye + pw
            for _ in range(n_levels - 1):
                pwb = pw.astype(BF16)
                pw = _dot(pwb, pwb)
                inv = inv + _dot(inv.astype(BF16), pw.astype(BF16))
            inv_b = inv.astype(BF16)
            u = _dot(inv_b, (vh * bcol).astype(BF16))
            w = _dot(inv_b, (kbh * jnp.exp(gcol)).astype(BF16))
            qk = _dot_nt(qh.astype(BF16), kh_b) * decay
            s_old = st_ref[h]
            s_b = s_old.astype(BF16)
            v_new = u - _dot(w.astype(BF16), s_b)
            v_new_b = v_new.astype(BF16)
            o = _dot((qh * jnp.exp(gcol)).astype(BF16), s_b) + _dot(qk.astype(BF16), v_new_b)
            g_last = gc[chunk - 1:chunk, g_lane0 + h:g_lane0 + h + 1]
            k_dec = kh * jnp.exp(g_last - gcol)
            st_ref[h] = s_old * jnp.exp(g_last) + _dot_tn(k_dec.astype(BF16), v_new_b)
            zh = z_ref[0, rows, sl]
            o_ref[0, rows, sl] = (_rms(o, ng) * (zh * jax.nn.sigmoid(zh))).astype(o_ref.dtype)
        return carry

    lax.fori_loop(0, tt // chunk, chunk_body, 0)

    @pl.when(step == n_steps - 1)
    def _():
        sfin_ref[0] = st_ref[...]
        last = t_valid - (n_steps - 1) * tt
        convout_ref[0] = xbuf[last:last + SUBLANES, :]


def _gdn(x, z, sm, conv0, s0, lw, *, tt, chunk, t_valid):
    nb, t_pad, conv_ch = x.shape
    n_heads = s0.shape[1]
    vw = z.shape[-1]
    assert t_pad % tt == 0 and tt % chunk == 0
    n_steps = t_pad // tt
    assert 0 < t_valid - (n_steps - 1) * tt <= tt and t_valid >= CONV_W - 1
    tok = lambda w: pl.BlockSpec((1, tt, w), lambda b, s: (b, s, 0))
    return pl.pallas_call(
        functools.partial(_gdn_kernel, tt=tt, chunk=chunk, n_heads=n_heads, t_valid=t_valid, n_steps=n_steps),
        grid=(nb, n_steps),
        in_specs=[tok(conv_ch), tok(vw), tok(LANES),
                  pl.BlockSpec((1, SUBLANES, conv_ch), lambda b, s: (b, 0, 0)),
                  pl.BlockSpec((1, n_heads, GDN_DK, GDN_DV), lambda b, s: (b, 0, 0, 0)),
                  pl.BlockSpec((SUBLANES, conv_ch), lambda b, s: (0, 0)),
                  pl.BlockSpec((1, LANES), lambda b, s: (0, 0)),
                  pl.BlockSpec((1, LANES), lambda b, s: (0, 0)),
                  pl.BlockSpec((1, GDN_DV), lambda b, s: (0, 0))],
        out_specs=(tok(vw),
                   pl.BlockSpec((1, n_heads, GDN_DK, GDN_DV), lambda b, s: (b, 0, 0, 0)),
                   pl.BlockSpec((1, SUBLANES, conv_ch), lambda b, s: (b, 0, 0))),
        out_shape=(jax.ShapeDtypeStruct((nb, t_pad, vw), BF16),
                   jax.ShapeDtypeStruct((nb, n_heads, GDN_DK, GDN_DV), F32),
                   jax.ShapeDtypeStruct((nb, SUBLANES, conv_ch), F32)),
        scratch_shapes=[pltpu.VMEM((tt + 2 * SUBLANES, conv_ch), F32),
                        pltpu.VMEM((n_heads, GDN_DK, GDN_DV), F32),
                        pltpu.VMEM((tt, n_heads * GDN_DK), F32),
                        pltpu.VMEM((tt, n_heads * GDN_DK), F32),
                        pltpu.VMEM((tt, vw), F32),
                        pltpu.VMEM((tt, LANES), F32),
                        pltpu.VMEM((tt, LANES), F32)],
        compiler_params=pltpu.CompilerParams(dimension_semantics=("arbitrary", "arbitrary"),
                                             vmem_limit_bytes=VMEM_LIMIT),
        name="gated_delta",
    )(x, z, sm, conv0, s0, lw["conv_w"], lw["alog"], lw["dtb"], lw["gdn_norm_g"])


def _outmlp_kernel(x_ref, att_ref, o_ref, woa_ref, wob_ref, g1_ref, g2_ref, g3_ref, wup_ref, wdn_ref,
                   y_ref, *, ff_chunk):
    x = x_ref[...]
    mix = _dot(att_ref[...], woa_ref[...]) + _dot(o_ref[...], wob_ref[...])
    x1 = x + _rms(mix, g1_ref[...])
    h2 = _rms(x1, g2_ref[...]).astype(BF16)
    d_ff = wup_ref.shape[1]
    ff = jnp.zeros(x.shape, F32)
    for c in range(d_ff // ff_chunk):
        u = jnp.maximum(_dot(h2, wup_ref[:, c * ff_chunk:(c + 1) * ff_chunk]), 0.0)
        ff = ff + _dot((u * u).astype(BF16), wdn_ref[c * ff_chunk:(c + 1) * ff_chunk, :])
    y_ref[...] = x1 + _rms(ff, g3_ref[...])


def _outmlp(x2d, att, o, lw, *, tm, ff_chunk):
    n, d = x2d.shape
    att_w = att.shape[1]
    vw = o.shape[1]
    d_ff = lw["wup"].shape[1]
    assert n % tm == 0 and d_ff % ff_chunk == 0
    row = lambda w: pl.BlockSpec((tm, w), lambda i: (i, 0))
    return pl.pallas_call(
        functools.partial(_outmlp_kernel, ff_chunk=ff_chunk),
        grid=(n // tm,),
        in_specs=[row(d), row(att_w), row(vw), _const_spec((att_w, d)), _const_spec((vw, d)),
                  _const_spec((1, d)), _const_spec((1, d)), _const_spec((1, d)),
                  _const_spec((d, d_ff)), _const_spec((d_ff, d))],
        out_specs=row(d),
        out_shape=jax.ShapeDtypeStruct((n, d), F32),
        compiler_params=pltpu.CompilerParams(dimension_semantics=("arbitrary",),
                                             vmem_limit_bytes=VMEM_LIMIT),
        name="outproj_mlp",
    )(x2d, att, o, lw["woa"], lw["wob"], lw["post_mix_g"], lw["pre_mlp_g"], lw["post_mlp_g"],
      lw["wup"], lw["wdn"])


def _layer_weights(l, att_w, conv_ch, gdn_vw, n_gdn_heads, w_in, conv_w, idx_k_norm_g, idx_k_norm_b,
                   gdn_a_log, gdn_dt_bias, gdn_norm_g, w_out, pre_mix_g, post_mix_g, pre_mlp_g,
                   post_mlp_g, w_mlp_up, w_mlp_down):
    idx_w = N_IDX_HEADS * IDX_DIM
    o_ik = 3 * att_w + idx_w
    o_iw = o_ik + IDX_DIM
    o_qkv = o_iw + N_IDX_HEADS
    o_z = o_qkv + conv_ch
    o_gb = o_z + gdn_vw
    o_ga = o_gb + n_gdn_heads
    wi = w_in[l]
    assert wi.shape[1] == o_ga + n_gdn_heads
    d = wi.shape[0]
    n_small = IDX_DIM + N_IDX_HEADS + 2 * n_gdn_heads
    ws = jnp.concatenate([wi[:, o_ik:o_qkv], wi[:, o_gb:], jnp.zeros((d, LANES - n_small), wi.dtype)], axis=1)
    pad_lanes = lambda v, at: jnp.zeros((1, LANES), F32).at[0, at:at + v.shape[0]].set(v.astype(F32))
    smul = jnp.ones((1, LANES), F32).at[0, SM_IW:SM_IW + N_IDX_HEADS].set(N_IDX_HEADS ** -0.5 * IDX_DIM ** -0.5)
    g_lane0 = SM_GB + n_gdn_heads
    vec = lambda v: v[l].astype(F32)[None, :]
    return {
        "wa": wi[:, :o_ik].astype(BF16),
        "wqkv": wi[:, o_qkv:o_z].astype(BF16),
        "wz": wi[:, o_z:o_gb].astype(BF16),
        "ws": ws.astype(BF16),
        "lng": pad_lanes(idx_k_norm_g[l], 0),
        "lnb": pad_lanes(idx_k_norm_b[l], 0),
        "smul": smul,
        "conv_w": jnp.zeros((SUBLANES, conv_ch), F32).at[:CONV_W].set(conv_w[l].astype(F32)),
        "alog": pad_lanes(gdn_a_log[l], g_lane0),
        "dtb": pad_lanes(gdn_dt_bias[l], g_lane0),
        "gdn_norm_g": vec(gdn_norm_g),
        "woa": w_out[l, :att_w].astype(BF16),
        "wob": w_out[l, att_w:].astype(BF16),
        "pre_mix_g": vec(pre_mix_g), "post_mix_g": vec(post_mix_g),
        "pre_mlp_g": vec(pre_mlp_g), "post_mlp_g": vec(post_mlp_g),
        "wup": w_mlp_up[l].astype(BF16),
        "wdn": w_mlp_down[l].astype(BF16),
    }


def _pick_tile(n, prefs):
    for t in prefs:
        if n % t == 0:
            return t
    return n


def kernel(x_prompt, x_sample, cache_k, cache_v, cache_idx_k, page_table, state_gdn, state_conv, w_in, conv_w, idx_k_norm_g, idx_k_norm_b, gdn_a_log, gdn_dt_bias, gdn_norm_g, w_out, pre_mix_g, post_mix_g, pre_mlp_g, post_mlp_g, w_mlp_up, w_mlp_down):
    b, s, d = x_prompt.shape
    bd, t, _ = x_sample.shape
    depth, n_pool, page, n_att_heads, head_dim = cache_k.shape
    n_pages = page_table.shape[1]
    past = n_pages * page
    n_gdn_heads, dk, dv = state_gdn.shape[2:]
    conv_ch = state_conv.shape[-1]
    att_w = n_att_heads * head_dim
    gdn_vw = n_gdn_heads * dv
    assert head_dim == HEAD_DIM and dk == GDN_DK and dv == GDN_DV and cache_idx_k.shape[-1] == IDX_DIM
    assert conv_ch == 2 * n_gdn_heads * dk + gdn_vw and state_conv.shape[2] == CONV_W - 1
    assert t <= SUBLANES and page == LANES

    tab_p = _rope_tables(jnp.arange(s))
    tab_s = tuple(jnp.tile(tb, (bd, 1)) for tb in _rope_tables(past + jnp.arange(t)))
    topk_p = min(TOPK_MAX, s // 4)
    topk_s = min(TOPK_MAX, (past + t) // 4)
    hist = CONV_W - 1

    tm_p = _pick_tile(b * s, (256, 128, 64, 32, 16, 8))
    tm_s = _pick_tile(bd * t, (256, 128, 64, 32, 16, 8))
    tq = _pick_tile(s, (128,))
    tk = _pick_tile(s, (512, 256, 128))
    tt_p = _pick_tile(s, (256, 128, 64))
    chunk_p = min(GDN_CHUNK, tt_p)
    chunk_s = GDN_CHUNK
    pg_sel = _pick_tile(n_pages, (16, 8, 4, 2, 1))
    pg_att = _pick_tile(n_pages, (8, 4, 2, 1))

    yp = x_prompt.reshape(b * s, d)
    ys = x_sample.reshape(bd * t, d)
    outs_p, outs_s = [], []
    for l in range(depth):
        lw = _layer_weights(l, att_w, conv_ch, gdn_vw, n_gdn_heads, w_in, conv_w, idx_k_norm_g, idx_k_norm_b,
                            gdn_a_log, gdn_dt_bias, gdn_norm_g, w_out, pre_mix_g, post_mix_g, pre_mlp_g,
                            post_mlp_g, w_mlp_up, w_mlp_down)
        (q, kf, vf, kb, vb, iq, ikf, ikb, qkv, z, sm) = _inproj(
            yp, tab_p, lw, att_w=att_w, conv_ch=conv_ch, gdn_vw=gdn_vw, tm=tm_p)
        r3 = lambda a: a.reshape(b, s, a.shape[-1])
        att = _prompt_attention(r3(q), r3(iq), r3(sm), r3(kb), r3(vb), r3(ikb), tq=tq, tk=tk, topk=topk_p)
        o, s_fin, conv_out = _gdn(
            r3(qkv), r3(z), r3(sm), jnp.zeros((b, SUBLANES, conv_ch), F32),
            jnp.zeros((b, n_gdn_heads, dk, dv), F32), lw, tt=tt_p, chunk=chunk_p, t_valid=s)
        yp = _outmlp(yp, att.reshape(b * s, att_w), o.reshape(b * s, gdn_vw), lw, tm=tm_p, ff_chunk=1024)
        outs_p.append((kf.reshape(b, s, n_att_heads, head_dim), vf.reshape(b, s, n_att_heads, head_dim),
                       ikf.reshape(b, s, IDX_DIM), s_fin, conv_out[:, SUBLANES - hist:, :]))

        (q, kf, vf, kb, vb, iq, ikf, ikb, qkv, z, sm) = _inproj(
            ys, tab_s, lw, att_w=att_w, conv_ch=conv_ch, gdn_vw=gdn_vw, tm=tm_s)
        iq_s = iq.reshape(bd, t * N_IDX_HEADS, IDX_DIM)
        w_s = sm[:, SM_IW:SM_IW + N_IDX_HEADS].reshape(bd, t * N_IDX_HEADS, 1)
        pad_new = lambda a: jnp.pad(a.reshape(bd, t, a.shape[-1]), ((0, 0), (0, LANES - t), (0, 0)))
        bias = _decode_select(page_table, iq_s, w_s, pad_new(ikb), cache_idx_k[l],
                              pg=pg_sel, n_tok=t, topk=topk_s)
        q4 = q.reshape(bd, t, n_att_heads, 1, head_dim)
        eye_h = jnp.eye(SUBLANES, n_att_heads, dtype=q.dtype)[None, None, :, :, None]
        qbd = (q4.transpose(0, 1, 3, 2, 4) * eye_h).reshape(bd, t * SUBLANES, att_w)
        att_s = _decode_attention(page_table, qbd, bias, pad_new(kb), pad_new(vb),
                                  cache_k[l].reshape(n_pool, page, att_w), cache_v[l].reshape(n_pool, page, att_w),
                                  pg=pg_att, n_tok=t, n_heads=n_att_heads)
        pad_t = lambda a: jnp.pad(a.reshape(bd, t, a.shape[-1]), ((0, 0), (0, chunk_s - t), (0, 0)))
        conv0 = jnp.pad(state_conv[l].astype(F32), ((0, 0), (SUBLANES - hist, 0), (0, 0)))
        o_s, s_fin_s, conv_out_s = _gdn(pad_t(qkv), pad_t(z), pad_t(sm), conv0, state_gdn[l].astype(F32), lw,
                                        tt=chunk_s, chunk=chunk_s, t_valid=t)
        ys = _outmlp(ys, att_s.reshape(bd * t, att_w), o_s[:, :t].reshape(bd * t, gdn_vw), lw,
                     tm=tm_s, ff_chunk=1024)
        outs_s.append((kf.reshape(bd, t, n_att_heads, head_dim), vf.reshape(bd, t, n_att_heads, head_dim),
                       ikf.reshape(bd, t, IDX_DIM), s_fin_s, conv_out_s[:, SUBLANES - hist:, :]))

    kp, vp, ikp, gp, cp = [jnp.stack([o_[i] for o_ in outs_p]) for i in range(5)]
    ks_, vs_, iks, gs, cs = [jnp.stack([o_[i] for o_ in outs_s]) for i in range(5)]
    return (yp.reshape(b, s, d), ys.reshape(bd, t, d), kp, vp, ikp, gp, cp, ks_, vs_, iks, gs, cs)
```

```python
import functools
import math

import jax
import jax.numpy as jnp
from jax import lax
from jax.experimental import pallas as pl
from jax.experimental.pallas import tpu as pltpu

F32 = jnp.float32
BF16 = jnp.bfloat16

HEAD_DIM = 128
IDX_DIM = 64
N_IDX_HEADS = 8
GDN_DK = 128
GDN_DV = 128
CONV_W = 4
TOPK_MAX = 256
GDN_CHUNK = 64
ROPE_THETA = 10000.0
NORM_EPS = 1e-6

LANES = 128
SUBLANES = 8
VMEM_LIMIT = 56 * 1024 * 1024
NEG_BIG = -1e30
BF16_EXACT_INT = 256
VALUE_STEPS = 24
SM_IW = IDX_DIM
SM_GB = IDX_DIM + N_IDX_HEADS
HIGHEST = lax.Precision.HIGHEST


def _dot(a, b):
    return jnp.dot(a, b, preferred_element_type=F32)


def _dot_nt(a, b):
    return lax.dot_general(a, b, (((1,), (1,)), ((), ())), preferred_element_type=F32)


def _dot_tn(a, b):
    return lax.dot_general(a, b, (((0,), (0,)), ((), ())), preferred_element_type=F32)


def _rms(x, g):
    return x * lax.rsqrt(jnp.mean(x * x, axis=-1, keepdims=True) + NORM_EPS) * g


def _const_spec(shape):
    n = len(shape)
    return pl.BlockSpec(shape, lambda *_: (0,) * n, pipeline_mode=pl.Buffered(1))


def _lane_fold(x, op):
    r = x[:, 0:LANES]
    for j in range(1, x.shape[1] // LANES):
        r = op(r, x[:, j * LANES:(j + 1) * LANES])
    return r


def _inproj_kernel(x_ref, g_ref, wa_ref, wqkv_ref, wz_ref, ws_ref,
                   cos_ref, sin_ref, cosi_ref, sina_ref, sinb_ref, lng_ref, lnb_ref, smul_ref,
                   q_ref, kf_ref, vf_ref, kb_ref, vb_ref, vt_ref, iq_ref, ikf_ref, ikb_ref,
                   qkv_ref, z_ref, sm_ref, *, att_w, q_scale):
    x = x_ref[...]
    h = _rms(x, g_ref[...]).astype(BF16)
    a = _dot(h, wa_ref[...])
    cos = cos_ref[...]
    sin = sin_ref[...]
    n_heads = att_w // HEAD_DIM

    def rope_full(xh):
        return xh * cos + pltpu.roll(xh, HEAD_DIM // 2, axis=1) * sin

    for j in range(n_heads):
        sl = slice(j * HEAD_DIM, (j + 1) * HEAD_DIM)
        q_ref[:, sl] = (rope_full(a[:, sl]) * q_scale).astype(BF16)
        kr = rope_full(a[:, att_w + j * HEAD_DIM: att_w + (j + 1) * HEAD_DIM])
        kf_ref[:, sl] = kr
        kb_ref[:, sl] = kr.astype(BF16)
    v = a[:, 2 * att_w:3 * att_w]
    vf_ref[...] = v
    vb_ref[...] = v.astype(BF16)
    vt_ref[0] = jnp.transpose(v).astype(BF16)

    cosi = cosi_ref[...]
    sina = sina_ref[...]
    sinb = sinb_ref[...]

    def rope_half(xh):
        return (xh * cosi + pltpu.roll(xh, LANES - IDX_DIM // 2, axis=1) * sina
                + pltpu.roll(xh, IDX_DIM // 2, axis=1) * sinb)

    idx_w = N_IDX_HEADS * IDX_DIM
    for j in range(idx_w // LANES):
        sl = slice(j * LANES, (j + 1) * LANES)
        iq_ref[:, sl] = rope_half(a[:, 3 * att_w + j * LANES: 3 * att_w + (j + 1) * LANES]).astype(BF16)

    qkv_ref[...] = _dot(h, wqkv_ref[...])
    z_ref[...] = _dot(h, wz_ref[...])

    sm = _dot(h, ws_ref[...])
    lane = lax.broadcasted_iota(jnp.int32, sm.shape, 1)
    is_ik = lane < IDX_DIM
    ikraw = jnp.where(is_ik, sm, 0.0)
    mu = jnp.sum(ikraw, axis=-1, keepdims=True) * (1.0 / IDX_DIM)
    xc = jnp.where(is_ik, sm - mu, 0.0)
    var = jnp.sum(xc * xc, axis=-1, keepdims=True) * (1.0 / IDX_DIM)
    ikn = xc * lax.rsqrt(var + NORM_EPS) * lng_ref[...] + lnb_ref[...]
    ikr = rope_half(ikn)
    ikf_ref[...] = ikr[:, :IDX_DIM]
    ikb_ref[...] = ikr[:, :IDX_DIM].astype(BF16)
    sm_ref[...] = sm * smul_ref[...]


def _rope_tables(pos):
    pos = pos.astype(F32)[:, None]
    half = HEAD_DIM // 2
    inv = jnp.power(ROPE_THETA, -jnp.arange(half, dtype=F32) / half)
    ang = pos * inv[None, :]
    c, s = jnp.cos(ang), jnp.sin(ang)
    cos = jnp.concatenate([c, c], axis=-1)
    sin = jnp.concatenate([-s, s], axis=-1)
    halfi = IDX_DIM // 2
    invi = jnp.power(ROPE_THETA, -jnp.arange(halfi, dtype=F32) / halfi)
    angi = pos * invi[None, :]
    ci, si = jnp.cos(angi), jnp.sin(angi)
    zi = jnp.zeros_like(si)
    cosi = jnp.concatenate([ci, ci, ci, ci], axis=-1)
    sina = jnp.concatenate([-si, zi, -si, zi], axis=-1)
    sinb = jnp.concatenate([zi, si, zi, si], axis=-1)
    return cos, sin, cosi, sina, sinb


def _inproj(x2d, tables, lw, *, att_w, conv_ch, gdn_vw, tm):
    n, d = x2d.shape
    idx_w = N_IDX_HEADS * IDX_DIM
    rt = tables[0].shape[0]
    assert n % tm == 0 and rt % tm == 0 and n % rt == 0
    nt = rt // tm
    wa_w = 3 * att_w + idx_w
    row = lambda w: pl.BlockSpec((tm, w), lambda i: (i, 0))
    tab = pl.BlockSpec((tm, LANES), lambda i: (i % nt, 0))
    out_shapes = (
        jax.ShapeDtypeStruct((n, att_w), BF16),
        jax.ShapeDtypeStruct((n, att_w), F32),
        jax.ShapeDtypeStruct((n, att_w), F32),
        jax.ShapeDtypeStruct((n, att_w), BF16),
        jax.ShapeDtypeStruct((n, att_w), BF16),
        jax.ShapeDtypeStruct((n // rt, att_w, rt), BF16),
        jax.ShapeDtypeStruct((n, idx_w), BF16),
        jax.ShapeDtypeStruct((n, IDX_DIM), F32),
        jax.ShapeDtypeStruct((n, IDX_DIM), BF16),
        jax.ShapeDtypeStruct((n, conv_ch), F32),
        jax.ShapeDtypeStruct((n, gdn_vw), F32),
        jax.ShapeDtypeStruct((n, LANES), F32),
    )
    vt_spec = pl.BlockSpec((1, att_w, tm), lambda i: (i // nt, 0, i % nt))
    out_specs = (row(att_w), row(att_w), row(att_w), row(att_w), row(att_w), vt_spec, row(idx_w),
                 row(IDX_DIM), row(IDX_DIM), row(conv_ch), row(gdn_vw), row(LANES))
    in_specs = [row(d), _const_spec((1, d)), _const_spec((d, wa_w)), _const_spec((d, conv_ch)),
                _const_spec((d, gdn_vw)), _const_spec((d, LANES)),
                tab, tab, tab, tab, tab,
                _const_spec((1, LANES)), _const_spec((1, LANES)), _const_spec((1, LANES))]
    return pl.pallas_call(
        functools.partial(_inproj_kernel, att_w=att_w, q_scale=HEAD_DIM ** -0.5),
        grid=(n // tm,),
        in_specs=in_specs, out_specs=out_specs, out_shape=out_shapes,
        compiler_params=pltpu.CompilerParams(dimension_semantics=("arbitrary",),
                                             vmem_limit_bytes=VMEM_LIMIT),
        name="inproj",
    )(x2d, lw["pre_mix_g"], lw["wa"], lw["wqkv"], lw["wz"], lw["ws"], *tables,
      lw["lng"], lw["lnb"], lw["smul"])


def _select_threshold(count, rewrite, n_valid, row_max, row_min, topk, col_limit):
    kf = float(topk)
    flip = jnp.int32(0x7FFFFFFF)

    def to_key(x):
        b = lax.bitcast_convert_type(jnp.where(x == 0.0, 0.0, x), jnp.int32)
        return jnp.where(b < 0, b ^ flip, b)

    def from_key(k):
        min_normal = jnp.int32(0x00800000)
        k = jnp.where(jnp.logical_and(k > 0, k < min_normal), min_normal, k)
        return lax.bitcast_convert_type(jnp.where(k < 0, k ^ flip, k), F32)

    need = n_valid > kf
    spread = jnp.maximum(row_max - row_min, jnp.abs(row_max) * (2.0 ** -20) + 2.0 ** -100)
    lo0 = jnp.where(need, row_min, NEG_BIG)
    hi0 = row_max + spread
    done0 = jnp.where(need, 0.0, 1.0)

    def narrow(c, mid, stuck, cnt):
        lo, hi, c_lo, c_hi, done = c
        ge = cnt >= kf
        upd = jnp.logical_and(done < 0.5, jnp.logical_not(stuck))
        up_lo = jnp.logical_and(upd, ge)
        up_hi = jnp.logical_and(upd, jnp.logical_not(ge))
        fin = jnp.logical_or(stuck, jnp.logical_and(upd, cnt == kf))
        return (jnp.where(up_lo, mid, lo), jnp.where(up_hi, mid, hi), jnp.where(up_lo, cnt, c_lo),
                jnp.where(up_hi, cnt, c_hi), jnp.where(fin, 1.0, done))

    def value_step(c):
        it, st = c
        lo, hi = st[0], st[1]
        mid = 0.5 * lo + 0.5 * hi
        stuck = jnp.logical_or(mid <= lo, mid >= hi)
        return it + 1, narrow(st, mid, stuck, count(lambda x, col: x >= mid))

    _, (lo, hi, c_lo, c_hi, done) = lax.while_loop(
        lambda c: jnp.logical_and(c[0] < VALUE_STEPS, jnp.min(c[1][4]) < 0.5), value_step,
        (jnp.int32(0), (lo0, hi0, n_valid, jnp.zeros_like(n_valid), done0)))

    def key_step(st):
        klo, khi = st[0], st[1]
        mid = (klo & khi) + ((klo ^ khi) >> 1)
        return narrow(st, mid, mid <= klo, count(lambda x, col: to_key(x) >= mid))

    klo, khi, c_lo, c_hi, _ = lax.while_loop(
        lambda st: jnp.min(st[4]) < 0.5, key_step, (to_key(lo), to_key(hi), c_lo, c_hi, done))
    lo, hi = from_key(klo), from_key(khi)

    tied = jnp.logical_and(need, c_lo > kf)

    @pl.when(jnp.max(jnp.where(tied, 1.0, 0.0)) > 0.5)
    def _():
        want = kf - c_hi

        def in_tie(x):
            return jnp.logical_and(x >= lo, x < hi)

        def jstep(_, c):
            jlo, jhi = c
            jm = jnp.floor((jlo + jhi) * 0.5)
            cnt = count(lambda x, col: jnp.logical_and(in_tie(x), col <= jm))
            ok = cnt >= want
            return jnp.where(ok, jlo, jm), jnp.where(ok, jm, jhi)

        n_it = max(1, math.ceil(math.log2(col_limit + 1)))
        j0 = (jnp.full_like(lo, -1.0), jnp.full_like(lo, float(col_limit - 1)))
        _, jcut = lax.fori_loop(0, n_it, jstep, j0)
        rewrite(lambda x, col: jnp.where(
            jnp.logical_and(tied, jnp.logical_and(in_tie(x), col > jcut)), -jnp.inf, x))

    return lo


def _pattn_kernel(q_ref, iq_ref, sm_ref, k_ref, vt_ref, ik_ref, o_ref, sc_ref, m_ref, l_ref, acc_ref,
                  *, tq, tk, topk, seq, n_heads):
    qi = pl.program_id(1)
    t0 = qi * tq
    nkb = (t0 + tq + tk - 1) // tk
    sub_tiles = tk // SUBLANES
    q_t = t0 + lax.broadcasted_iota(jnp.int32, (1, tq), 1)
    smt = jnp.transpose(sm_ref[0])
    w_rows = [smt[SM_IW + h:SM_IW + h + 1, :] for h in range(N_IDX_HEADS)]
    iq = iq_ref[0]
    iq_heads = [iq[:, h * IDX_DIM:(h + 1) * IDX_DIM] for h in range(N_IDX_HEADS)]
    sub_iota = lax.broadcasted_iota(jnp.int32, (SUBLANES, tq), 0)

    def sub_fold(x, op):
        r = x[0:SUBLANES]
        for j in range(1, x.shape[0] // SUBLANES):
            r = op(r, x[j * SUBLANES:(j + 1) * SUBLANES])
        return r

    def rep(row):
        return jnp.broadcast_to(row, (SUBLANES, tq))

    def score_block(kb, carry):
        mx, mn = carry
        k0 = pl.multiple_of(kb * tk, tk)
        ikb = ik_ref[0, pl.ds(k0, tk), :]
        acc = jnp.zeros((tk, tq), F32)
        for h in range(N_IDX_HEADS):
            acc = acc + w_rows[h] * jnp.maximum(_dot_nt(ikb, iq_heads[h]), 0.0)
        key = k0 + lax.broadcasted_iota(jnp.int32, (tk, 1), 0)
        valid = key <= q_t
        sc_ref[pl.ds(k0, tk), :] = jnp.where(valid, acc, -jnp.inf)
        mx = jnp.maximum(mx, sub_fold(jnp.where(valid, acc, -jnp.inf), jnp.maximum))
        mn = jnp.minimum(mn, sub_fold(jnp.where(valid, acc, jnp.inf), jnp.minimum))
        return mx, mn

    mx, mn = lax.fori_loop(0, nkb, score_block,
                           (jnp.full((SUBLANES, tq), -jnp.inf, F32), jnp.full((SUBLANES, tq), jnp.inf, F32)))
    row_max = rep(jnp.max(mx, axis=0, keepdims=True))
    row_min = rep(jnp.min(mn, axis=0, keepdims=True))
    n_valid = rep((q_t + 1).astype(F32))

    def count(pred):
        def body(kb, acc):
            k0 = pl.multiple_of(kb * tk, tk)
            x = sc_ref[pl.ds(k0, tk), :]
            for j in range(sub_tiles):
                hit = pred(x[j * SUBLANES:(j + 1) * SUBLANES], (k0 + j * SUBLANES + sub_iota).astype(F32))
                acc = acc + jnp.where(hit, 1.0, 0.0)
            return acc
        acc = lax.fori_loop(0, nkb, body, jnp.zeros((SUBLANES, tq), F32))
        return rep(jnp.sum(acc, axis=0, keepdims=True))

    def rewrite(fn):
        def body(kb, carry):
            k0 = pl.multiple_of(kb * tk, tk)
            x = sc_ref[pl.ds(k0, tk), :]
            sc_ref[pl.ds(k0, tk), :] = jnp.concatenate(
                [fn(x[j * SUBLANES:(j + 1) * SUBLANES], (k0 + j * SUBLANES + sub_iota).astype(F32))
                 for j in range(sub_tiles)], axis=0)
            return carry
        lax.fori_loop(0, nkb, body, 0)

    lo = _select_threshold(count, rewrite, n_valid, row_max, row_min, topk, seq)
    tile_rows = lambda v8: jnp.concatenate([v8] * sub_tiles, axis=0)
    lo_t = tile_rows(lo)

    def masked_scores(kb, h):
        k0 = pl.multiple_of(kb * tk, tk)
        sl = slice(h * HEAD_DIM, (h + 1) * HEAD_DIM)
        s = _dot_nt(k_ref[0, pl.ds(k0, tk), sl], q_ref[0, :, sl])
        return jnp.where(sc_ref[pl.ds(k0, tk), :] >= lo_t, s, NEG_BIG)

    m_ref[...] = jnp.full(m_ref.shape, NEG_BIG, F32)

    def max_block(kb, carry):
        for h in range(n_heads):
            hs = slice(h * SUBLANES, (h + 1) * SUBLANES)
            m_ref[hs, :] = jnp.maximum(m_ref[hs, :], sub_fold(masked_scores(kb, h), jnp.maximum))
        return carry
    lax.fori_loop(0, nkb, max_block, 0)
    m_t = [tile_rows(rep(jnp.max(m_ref[h * SUBLANES:(h + 1) * SUBLANES, :], axis=0, keepdims=True)))
           for h in range(n_heads)]

    l_ref[...] = jnp.zeros_like(l_ref)
    acc_ref[...] = jnp.zeros_like(acc_ref)

    def pv_block(kb, carry):
        k0 = pl.multiple_of(kb * tk, tk)
        ps = [jnp.exp(s - m_t[h]) for h, s in enumerate([masked_scores(kb, h) for h in range(n_heads)])]
        for h in range(n_heads):
            hs = slice(h * SUBLANES, (h + 1) * SUBLANES)
            sl = slice(h * HEAD_DIM, (h + 1) * HEAD_DIM)
            l_ref[hs, :] = l_ref[hs, :] + sub_fold(ps[h], jnp.add)
            acc_ref[sl, :] = acc_ref[sl, :] + _dot(vt_ref[0, sl, pl.ds(k0, tk)], ps[h].astype(BF16))
        return carry
    lax.fori_loop(0, nkb, pv_block, 0)

    for h in range(n_heads):
        sl = slice(h * HEAD_DIM, (h + 1) * HEAD_DIM)
        l_row = jnp.sum(l_ref[h * SUBLANES:(h + 1) * SUBLANES, :], axis=0, keepdims=True)
        o_ref[0, :, sl] = jnp.transpose(acc_ref[sl, :] / l_row).astype(o_ref.dtype)


def _prompt_attention(q, iq, sm, kb, vt, ikb, *, tq, tk, topk):
    b, s, att_w = q.shape
    idx_w = iq.shape[-1]
    n_heads = att_w // HEAD_DIM
    assert s % tq == 0 and s % tk == 0 and tk % tq == 0 and tq % LANES == 0
    blk = lambda w: pl.BlockSpec((1, tq, w), lambda bi, qi: (bi, qi, 0))
    full = lambda r, w: pl.BlockSpec((1, r, w), lambda bi, qi: (bi, 0, 0), pipeline_mode=pl.Buffered(1))
    return pl.pallas_call(
        functools.partial(_pattn_kernel, tq=tq, tk=tk, topk=topk, seq=s, n_heads=n_heads),
        grid=(b, s // tq),
        in_specs=[blk(att_w), blk(idx_w), blk(LANES), full(s, att_w), full(att_w, s), full(s, IDX_DIM)],
        out_specs=blk(att_w),
        out_shape=jax.ShapeDtypeStruct((b, s, att_w), BF16),
        scratch_shapes=[pltpu.VMEM((s, tq), F32), pltpu.VMEM((n_heads * SUBLANES, tq), F32),
                        pltpu.VMEM((n_heads * SUBLANES, tq), F32), pltpu.VMEM((att_w, tq), F32)],
        compiler_params=pltpu.CompilerParams(dimension_semantics=("arbitrary", "arbitrary"),
                                             vmem_limit_bytes=VMEM_LIMIT),
        name="prompt_attention",
    )(q, iq, sm, kb, vt, ikb)


def _dsel_kernel(pt_ref, iq_ref, w_ref, iknew_ref, *rest, sb, pg, n_tok, past, topk):
    pages = rest[:sb * pg]
    sel_ref = rest[sb * pg]
    sc_ref = rest[sb * pg + 1]
    c = pl.program_id(1)
    n_c = pl.num_programs(1)
    page = pages[0].shape[2]
    rows, l_pad = sc_ref.shape

    def token_rows(s):
        return [jnp.sum(s[t * N_IDX_HEADS:(t + 1) * N_IDX_HEADS, :], axis=0, keepdims=True)
                for t in range(n_tok)]

    @pl.when(c == 0)
    def _():
        sc_ref[...] = jnp.full(sc_ref.shape, -jnp.inf, F32)

    for j in range(sb):
        iq = iq_ref[j]
        w = w_ref[j]
        for p in range(pg):
            kt = pages[j * pg + p][0].astype(BF16)
            tr = token_rows(jnp.maximum(_dot(iq, kt), 0.0) * w)
            k0 = pl.multiple_of((c * pg + p) * page, page)
            for t in range(n_tok):
                sc_ref[j * n_tok + t:j * n_tok + t + 1, pl.ds(k0, page)] = tr[t]

    @pl.when(c == n_c - 1)
    def _():
        col = lax.broadcasted_iota(jnp.int32, (1, LANES), 1)
        for j in range(sb):
            tr = token_rows(jnp.maximum(_dot_nt(iq_ref[j], iknew_ref[j]), 0.0) * w_ref[j])
            for t in range(n_tok):
                sc_ref[j * n_tok + t:j * n_tok + t + 1, past:past + LANES] = jnp.where(col <= t, tr[t], -jnp.inf)
        x = sc_ref[...]
        rep = lambda v: jnp.broadcast_to(v, (rows, LANES))
        row_max = rep(jnp.max(x, axis=-1, keepdims=True))
        row_min = rep(jnp.min(jnp.where(x > -jnp.inf, x, jnp.inf), axis=-1, keepdims=True))
        r = lax.broadcasted_iota(jnp.int32, (rows, LANES), 0).astype(F32)
        tok = r - n_tok * jnp.floor(r * (1.0 / n_tok))
        n_valid = past + 1.0 + tok
        n_blocks = l_pad // LANES
        assert n_blocks <= BF16_EXACT_INT
        ones_b = jnp.ones((LANES, LANES), BF16)
        lane_iota = lax.broadcasted_iota(jnp.int32, (rows, LANES), 1)

        def count(pred):
            def body(kb, acc):
                c0 = pl.multiple_of(kb * LANES, LANES)
                hit = pred(sc_ref[:, pl.ds(c0, LANES)], (c0 + lane_iota).astype(F32))
                return acc + jnp.where(hit, 1.0, 0.0)
            acc = lax.fori_loop(0, n_blocks, body, jnp.zeros((rows, LANES), F32))
            return _dot(acc.astype(BF16), ones_b)

        def rewrite(fn):
            def body(kb, carry):
                c0 = pl.multiple_of(kb * LANES, LANES)
                sc_ref[:, pl.ds(c0, LANES)] = fn(sc_ref[:, pl.ds(c0, LANES)], (c0 + lane_iota).astype(F32))
                return carry
            lax.fori_loop(0, n_blocks, body, 0)

        lo = _select_threshold(count, rewrite, n_valid, row_max, row_min, topk, l_pad)

        def emit(kb, carry):
            c0 = pl.multiple_of(kb * LANES, LANES)
            sel_ref[:, pl.ds(c0, LANES)] = jnp.where(sc_ref[:, pl.ds(c0, LANES)] >= lo, 1.0, 0.0)
            return carry
        lax.fori_loop(0, l_pad // LANES, emit, 0)


def _decode_select(page_table, iq, w, iknew, cache_idx_t, *, sb, pg, n_tok, topk):
    bd, n_pages = page_table.shape
    _, di, page = cache_idx_t.shape
    past = n_pages * page
    assert n_pages % pg == 0 and bd % sb == 0 and (sb * n_tok) % SUBLANES == 0
    l_pad = past + LANES
    rows = iq.shape[1]

    def page_spec(j, p):
        return pl.BlockSpec((1, di, page), lambda g, c, pt: (pt[g * sb + j, c * pg + p], 0, 0))

    grid_spec = pltpu.PrefetchScalarGridSpec(
        num_scalar_prefetch=1,
        grid=(bd // sb, n_pages // pg),
        in_specs=[pl.BlockSpec((sb, rows, di), lambda g, c, pt: (g, 0, 0)),
                  pl.BlockSpec((sb, rows, 1), lambda g, c, pt: (g, 0, 0)),
                  pl.BlockSpec((sb, LANES, di), lambda g, c, pt: (g, 0, 0))]
                 + [page_spec(j, p) for j in range(sb) for p in range(pg)],
        out_specs=pl.BlockSpec((sb * n_tok, l_pad), lambda g, c, pt: (g, 0)),
        scratch_shapes=[pltpu.VMEM((sb * n_tok, l_pad), F32)],
    )
    return pl.pallas_call(
        functools.partial(_dsel_kernel, sb=sb, pg=pg, n_tok=n_tok, past=past, topk=topk),
        grid_spec=grid_spec,
        out_shape=jax.ShapeDtypeStruct((bd * n_tok, l_pad), F32),
        compiler_params=pltpu.CompilerParams(dimension_semantics=("arbitrary", "arbitrary"),
                                             vmem_limit_bytes=VMEM_LIMIT),
        name="decode_select",
    )(page_table, iq, w, iknew, *([cache_idx_t] * (sb * pg)))


def _dattn_kernel(pt_ref, q_ref, sel_ref, knew_ref, vnew_ref, *rest, pg, n_tok, n_heads, past):
    kpages = rest[:pg]
    vpages = rest[pg:2 * pg]
    o_ref = rest[2 * pg]
    m_ref, l_ref, acc_ref = rest[2 * pg + 1:]
    c = pl.program_id(1)
    n_c = pl.num_programs(1)
    page = kpages[0].shape[1] // n_heads
    qh = [q_ref[0, h * SUBLANES:(h + 1) * SUBLANES, :] for h in range(n_heads)]

    @pl.when(c == 0)
    def _():
        m_ref[...] = jnp.full(m_ref.shape, NEG_BIG, F32)
        l_ref[...] = jnp.zeros_like(l_ref)
        acc_ref[...] = jnp.zeros_like(acc_ref)

    def update(k_of, v_of, n_blk, sel):
        s = jnp.concatenate(
            [jnp.concatenate([_dot_nt(qh[h], k_of(p, h)) for p in range(n_blk)], axis=1)
             for h in range(n_heads)], axis=0)
        s = jnp.where(jnp.concatenate([sel] * n_heads, axis=0) > 0.5, s, NEG_BIG)
        m_old = m_ref[...]
        m_new = jnp.maximum(m_old, jnp.max(s, axis=-1, keepdims=True))
        pr = jnp.exp(s - m_new).astype(BF16)
        alpha = jnp.exp(m_old - m_new)
        l_ref[...] = alpha * l_ref[...] + jnp.sum(pr.astype(F32), axis=-1, keepdims=True)
        pv = []
        for h in range(n_heads):
            a = jnp.zeros((SUBLANES, HEAD_DIM), F32)
            for p in range(n_blk):
                a = a + _dot(pr[h * SUBLANES:(h + 1) * SUBLANES, p * page:(p + 1) * page], v_of(p, h))
            pv.append(a)
        acc_ref[...] = alpha * acc_ref[...] + jnp.concatenate(pv, axis=0)
        m_ref[...] = m_new

    head_rows = lambda ref, h: ref[0, pl.ds(h, page, stride=n_heads), :].astype(BF16)
    k0 = pl.multiple_of(c * pg * page, pg * page)
    update(lambda p, h: head_rows(kpages[p], h), lambda p, h: head_rows(vpages[p], h), pg,
           sel_ref[0, :, pl.ds(k0, pg * page)])

    @pl.when(c == n_c - 1)
    def _():
        new = lambda ref, h: ref[0, :, h * HEAD_DIM:(h + 1) * HEAD_DIM]
        update(lambda p, h: new(knew_ref, h), lambda p, h: new(vnew_ref, h), 1,
               sel_ref[0, :, past:past + LANES])
        inv_l = 1.0 / l_ref[...]
        out = acc_ref[...] * inv_l
        for h in range(n_heads):
            o_ref[0, :, h * HEAD_DIM:(h + 1) * HEAD_DIM] = out[h * SUBLANES:h * SUBLANES + n_tok, :].astype(o_ref.dtype)


def _decode_attention(page_table, q8, sel, knew, vnew, cache_k, cache_v, *, pg, n_tok, n_heads):
    bd, n_pages = page_table.shape
    _, page_rows, hd = cache_k.shape
    page = page_rows // n_heads
    past = n_pages * page
    l_pad = sel.shape[-1]
    att_w = n_heads * hd
    rows = n_heads * SUBLANES
    assert n_pages % pg == 0 and page == LANES

    def page_spec(p):
        return pl.BlockSpec((1, page_rows, hd), lambda b, c, pt: (pt[b, c * pg + p], 0, 0))

    per_seq = lambda r, w: pl.BlockSpec((1, r, w), lambda b, c, pt: (b, 0, 0))
    grid_spec = pltpu.PrefetchScalarGridSpec(
        num_scalar_prefetch=1,
        grid=(bd, n_pages // pg),
        in_specs=[per_seq(rows, hd), per_seq(SUBLANES, l_pad), per_seq(LANES, att_w), per_seq(LANES, att_w)]
                 + [page_spec(p) for p in range(pg)] * 2,
        out_specs=per_seq(n_tok, att_w),
        scratch_shapes=[pltpu.VMEM((rows, 1), F32), pltpu.VMEM((rows, 1), F32), pltpu.VMEM((rows, hd), F32)],
    )
    return pl.pallas_call(
        functools.partial(_dattn_kernel, pg=pg, n_tok=n_tok, n_heads=n_heads, past=past),
        grid_spec=grid_spec,
        out_shape=jax.ShapeDtypeStruct((bd, n_tok, att_w), BF16),
        compiler_params=pltpu.CompilerParams(dimension_semantics=("arbitrary", "arbitrary"),
                                             vmem_limit_bytes=VMEM_LIMIT),
        name="decode_attention",
    )(page_table, q8, sel, knew, vnew, *([cache_k] * pg), *([cache_v] * pg))


def _gdn_kernel(x_ref, z_ref, sm_ref, conv0_ref, s0_ref, cw_ref, alog_ref, dtb_ref, ng_ref,
                o_ref, sfin_ref, convout_ref,
                xbuf, st_ref, q_s, k_s, v_s, beta_s, g_s,
                *, tt, chunk, n_heads, t_valid, n_steps):
    step = pl.program_id(1)
    hist = CONV_W - 1
    qk_w = n_heads * GDN_DK

    @pl.when(step == 0)
    def _():
        xbuf[0:SUBLANES, :] = conv0_ref[0]
        st_ref[...] = s0_ref[0]

    @pl.when(step > 0)
    def _():
        xbuf[0:SUBLANES, :] = xbuf[tt:tt + SUBLANES, :]

    xbuf[SUBLANES:SUBLANES + tt, :] = x_ref[0]
    cw = cw_ref[...]
    conv = cw[hist:hist + 1, :] * xbuf[SUBLANES:SUBLANES + tt, :]
    for j in range(hist):
        conv = conv + cw[j:j + 1, :] * xbuf[SUBLANES - hist + j:SUBLANES - hist + j + tt, :]
    act = conv * jax.nn.sigmoid(conv)

    def l2n(xh):
        return xh * lax.rsqrt(jnp.sum(xh * xh, axis=-1, keepdims=True) + NORM_EPS)

    for h in range(n_heads):
        sl = slice(h * GDN_DK, (h + 1) * GDN_DK)
        q_s[:, sl] = l2n(act[:, sl]) * (GDN_DK ** -0.5)
        k_s[:, sl] = l2n(act[:, qk_w + h * GDN_DK: qk_w + (h + 1) * GDN_DK])
    v_s[...] = act[:, 2 * qk_w:]

    smv = sm_ref[0]
    t_glob = step * tt + lax.broadcasted_iota(jnp.int32, (tt, 1), 0)
    live = t_glob < t_valid
    beta_s[...] = jnp.where(live, jax.nn.sigmoid(smv), 0.0)
    g_s[...] = jnp.where(live, -jnp.exp(alog_ref[...]) * jax.nn.softplus(smv + dtb_ref[...]), 0.0)

    ii = lax.broadcasted_iota(jnp.int32, (chunk, chunk), 0)
    jj = lax.broadcasted_iota(jnp.int32, (chunk, chunk), 1)
    tril = ii >= jj
    strict = ii > jj
    tril_f = jnp.where(tril, 1.0, 0.0)
    eye = jnp.where(ii == jj, 1.0, 0.0)
    lane8 = lax.broadcasted_iota(jnp.int32, (SUBLANES, LANES), 1)
    row8 = lax.broadcasted_iota(jnp.int32, (SUBLANES, LANES), 0)
    g_lane0 = SM_GB + n_heads
    pick_g = jnp.where(lane8 == g_lane0 + row8, 1.0, 0.0)
    n_levels = int(math.log2(chunk))
    ng = ng_ref[...]

    def chunk_body(ci, carry):
        r0 = pl.multiple_of(ci * chunk, chunk)
        rows = pl.ds(r0, chunk)
        gc = jnp.dot(tril_f, g_s[rows, :], precision=HIGHEST, preferred_element_type=F32)
        gcr = lax.dot_general(pick_g, gc, (((1,), (1,)), ((), ())), precision=HIGHEST,
                              preferred_element_type=F32)
        beta = beta_s[rows, :]
        for h in range(n_heads):
            sl = slice(h * GDN_DK, (h + 1) * GDN_DK)
            kh = k_s[rows, sl]
            qh = q_s[rows, sl]
            vh = v_s[rows, sl]
            bcol = beta[:, SM_GB + h:SM_GB + h + 1]
            gcol = gc[:, g_lane0 + h:g_lane0 + h + 1]
            grow = gcr[h:h + 1, :]
            decay = jnp.exp(jnp.where(tril, gcol - grow, -jnp.inf))
            kbh = kh * bcol
            kh_b = kh.astype(BF16)
            lmat = jnp.where(strict, _dot_nt(kbh.astype(BF16), kh_b) * decay, 0.0)
            pw = -lmat
            inv = eye + pw
            for _ in range(n_levels - 1):
                pwb = pw.astype(BF16)
                pw = _dot(pwb, pwb)
                inv = inv + _dot(inv.astype(BF16), pw.astype(BF16))
            inv_b = inv.astype(BF16)
            u = _dot(inv_b, (vh * bcol).astype(BF16))
            w = _dot(inv_b, (kbh * jnp.exp(gcol)).astype(BF16))
            qk = _dot_nt(qh.astype(BF16), kh_b) * decay
            s_old = st_ref[h]
            s_b = s_old.astype(BF16)
            v_new = u - _dot(w.astype(BF16), s_b)
            v_new_b = v_new.astype(BF16)
            o = _dot((qh * jnp.exp(gcol)).astype(BF16), s_b) + _dot(qk.astype(BF16), v_new_b)
            g_last = gc[chunk - 1:chunk, g_lane0 + h:g_lane0 + h + 1]
            k_dec = kh * jnp.exp(g_last - gcol)
            st_ref[h] = s_old * jnp.exp(g_last) + _dot_tn(k_dec.astype(BF16), v_new_b)
            zh = z_ref[0, rows, sl]
            o_ref[0, rows, sl] = (_rms(o, ng) * (zh * jax.nn.sigmoid(zh))).astype(o_ref.dtype)
        return carry

    lax.fori_loop(0, tt // chunk, chunk_body, 0)

    @pl.when(step == n_steps - 1)
    def _():
        sfin_ref[0] = st_ref[...]
        last = t_valid - (n_steps - 1) * tt
        convout_ref[0] = xbuf[last:last + SUBLANES, :]


def _gdn(x, z, sm, conv0, s0, lw, *, tt, chunk, t_valid):
    nb, t_pad, conv_ch = x.shape
    n_heads = s0.shape[1]
    vw = z.shape[-1]
    assert t_pad % tt == 0 and tt % chunk == 0
    n_steps = t_pad // tt
    assert 0 < t_valid - (n_steps - 1) * tt <= tt and t_valid >= CONV_W - 1
    tok = lambda w: pl.BlockSpec((1, tt, w), lambda b, s: (b, s, 0))
    return pl.pallas_call(
        functools.partial(_gdn_kernel, tt=tt, chunk=chunk, n_heads=n_heads, t_valid=t_valid, n_steps=n_steps),
        grid=(nb, n_steps),
        in_specs=[tok(conv_ch), tok(vw), tok(LANES),
                  pl.BlockSpec((1, SUBLANES, conv_ch), lambda b, s: (b, 0, 0)),
                  pl.BlockSpec((1, n_heads, GDN_DK, GDN_DV), lambda b, s: (b, 0, 0, 0)),
                  pl.BlockSpec((SUBLANES, conv_ch), lambda b, s: (0, 0)),
                  pl.BlockSpec((1, LANES), lambda b, s: (0, 0)),
                  pl.BlockSpec((1, LANES), lambda b, s: (0, 0)),
                  pl.BlockSpec((1, GDN_DV), lambda b, s: (0, 0))],
        out_specs=(tok(vw),
                   pl.BlockSpec((1, n_heads, GDN_DK, GDN_DV), lambda b, s: (b, 0, 0, 0)),
                   pl.BlockSpec((1, SUBLANES, conv_ch), lambda b, s: (b, 0, 0))),
        out_shape=(jax.ShapeDtypeStruct((nb, t_pad, vw), BF16),
                   jax.ShapeDtypeStruct((nb, n_heads, GDN_DK, GDN_DV), F32),
                   jax.ShapeDtypeStruct((nb, SUBLANES, conv_ch), F32)),
        scratch_shapes=[pltpu.VMEM((tt + 2 * SUBLANES, conv_ch), F32),
                        pltpu.VMEM((n_heads, GDN_DK, GDN_DV), F32),
                        pltpu.VMEM((tt, n_heads * GDN_DK), F32),
                        pltpu.VMEM((tt, n_heads * GDN_DK), F32),
                        pltpu.VMEM((tt, vw), F32),
                        pltpu.VMEM((tt, LANES), F32),
                        pltpu.VMEM((tt, LANES), F32)],
        compiler_params=pltpu.CompilerParams(dimension_semantics=("arbitrary", "arbitrary"),
                                             vmem_limit_bytes=VMEM_LIMIT),
        name="gated_delta",
    )(x, z, sm, conv0, s0, lw["conv_w"], lw["alog"], lw["dtb"], lw["gdn_norm_g"])


def _outmlp_kernel(x_ref, att_ref, o_ref, woa_ref, wob_ref, g1_ref, g2_ref, g3_ref, wup_ref, wdn_ref,
                   y_ref, *, ff_chunk):
    x = x_ref[...]
    mix = _dot(att_ref[...], woa_ref[...]) + _dot(o_ref[...], wob_ref[...])
    x1 = x + _rms(mix, g1_ref[...])
    h2 = _rms(x1, g2_ref[...]).astype(BF16)
    d_ff = wup_ref.shape[1]
    ff = jnp.zeros(x.shape, F32)
    for c in range(d_ff // ff_chunk):
        u = jnp.maximum(_dot(h2, wup_ref[:, c * ff_chunk:(c + 1) * ff_chunk]), 0.0)
        ff = ff + _dot((u * u).astype(BF16), wdn_ref[c * ff_chunk:(c + 1) * ff_chunk, :])
    y_ref[...] = x1 + _rms(ff, g3_ref[...])


def _outmlp(x2d, att, o, lw, *, tm, ff_chunk):
    n, d = x2d.shape
    att_w = att.shape[1]
    vw = o.shape[1]
    d_ff = lw["wup"].shape[1]
    assert n % tm == 0 and d_ff % ff_chunk == 0
    row = lambda w: pl.BlockSpec((tm, w), lambda i: (i, 0))
    return pl.pallas_call(
        functools.partial(_outmlp_kernel, ff_chunk=ff_chunk),
        grid=(n // tm,),
        in_specs=[row(d), row(att_w), row(vw), _const_spec((att_w, d)), _const_spec((vw, d)),
                  _const_spec((1, d)), _const_spec((1, d)), _const_spec((1, d)),
                  _const_spec((d, d_ff)), _const_spec((d_ff, d))],
        out_specs=row(d),
        out_shape=jax.ShapeDtypeStruct((n, d), F32),
        compiler_params=pltpu.CompilerParams(dimension_semantics=("arbitrary",),
                                             vmem_limit_bytes=VMEM_LIMIT),
        name="outproj_mlp",
    )(x2d, att, o, lw["woa"], lw["wob"], lw["post_mix_g"], lw["pre_mlp_g"], lw["post_mlp_g"],
      lw["wup"], lw["wdn"])


def _layer_weights(l, att_w, conv_ch, gdn_vw, n_gdn_heads, w_in, conv_w, idx_k_norm_g, idx_k_norm_b,
                   gdn_a_log, gdn_dt_bias, gdn_norm_g, w_out, pre_mix_g, post_mix_g, pre_mlp_g,
                   post_mlp_g, w_mlp_up, w_mlp_down):
    idx_w = N_IDX_HEADS * IDX_DIM
    o_ik = 3 * att_w + idx_w
    o_iw = o_ik + IDX_DIM
    o_qkv = o_iw + N_IDX_HEADS
    o_z = o_qkv + conv_ch
    o_gb = o_z + gdn_vw
    o_ga = o_gb + n_gdn_heads
    wi = w_in[l]
    assert wi.shape[1] == o_ga + n_gdn_heads
    d = wi.shape[0]
    n_small = IDX_DIM + N_IDX_HEADS + 2 * n_gdn_heads
    ws = jnp.concatenate([wi[:, o_ik:o_qkv], wi[:, o_gb:], jnp.zeros((d, LANES - n_small), wi.dtype)], axis=1)
    pad_lanes = lambda v, at: jnp.zeros((1, LANES), F32).at[0, at:at + v.shape[0]].set(v.astype(F32))
    smul = jnp.ones((1, LANES), F32).at[0, SM_IW:SM_IW + N_IDX_HEADS].set(N_IDX_HEADS ** -0.5 * IDX_DIM ** -0.5)
    g_lane0 = SM_GB + n_gdn_heads
    vec = lambda v: v[l].astype(F32)[None, :]
    return {
        "wa": wi[:, :o_ik].astype(BF16),
        "wqkv": wi[:, o_qkv:o_z].astype(BF16),
        "wz": wi[:, o_z:o_gb].astype(BF16),
        "ws": ws.astype(BF16),
        "lng": pad_lanes(idx_k_norm_g[l], 0),
        "lnb": pad_lanes(idx_k_norm_b[l], 0),
        "smul": smul,
        "conv_w": jnp.zeros((SUBLANES, conv_ch), F32).at[:CONV_W].set(conv_w[l].astype(F32)),
        "alog": pad_lanes(gdn_a_log[l], g_lane0),
        "dtb": pad_lanes(gdn_dt_bias[l], g_lane0),
        "gdn_norm_g": vec(gdn_norm_g),
        "woa": w_out[l, :att_w].astype(BF16),
        "wob": w_out[l, att_w:].astype(BF16),
        "pre_mix_g": vec(pre_mix_g), "post_mix_g": vec(post_mix_g),
        "pre_mlp_g": vec(pre_mlp_g), "post_mlp_g": vec(post_mlp_g),
        "wup": w_mlp_up[l].astype(BF16),
        "wdn": w_mlp_down[l].astype(BF16),
    }


def _pick_tile(n, prefs):
    for t in prefs:
        if n % t == 0:
            return t
    return n


def kernel(x_prompt, x_sample, cache_k, cache_v, cache_idx_k, page_table, state_gdn, state_conv, w_in, conv_w, idx_k_norm_g, idx_k_norm_b, gdn_a_log, gdn_dt_bias, gdn_norm_g, w_out, pre_mix_g, post_mix_g, pre_mlp_g, post_mlp_g, w_mlp_up, w_mlp_down):
    b, s, d = x_prompt.shape
    bd, t, _ = x_sample.shape
    depth, n_pool, page, n_att_heads, head_dim = cache_k.shape
    n_pages = page_table.shape[1]
    past = n_pages * page
    n_gdn_heads, dk, dv = state_gdn.shape[2:]
    conv_ch = state_conv.shape[-1]
    att_w = n_att_heads * head_dim
    gdn_vw = n_gdn_heads * dv
    assert head_dim == HEAD_DIM and dk == GDN_DK and dv == GDN_DV and cache_idx_k.shape[-1] == IDX_DIM
    assert conv_ch == 2 * n_gdn_heads * dk + gdn_vw and state_conv.shape[2] == CONV_W - 1
    assert t <= SUBLANES and page == LANES

    tab_p = _rope_tables(jnp.arange(s))
    tab_s = tuple(jnp.tile(tb, (bd, 1)) for tb in _rope_tables(past + jnp.arange(t)))
    topk_p = min(TOPK_MAX, s // 4)
    topk_s = min(TOPK_MAX, (past + t) // 4)
    hist = CONV_W - 1

    tm_p = _pick_tile(b * s, (256, 128, 64, 32, 16, 8))
    tm_s = _pick_tile(bd * t, (256, 128, 64, 32, 16, 8))
    tq = _pick_tile(s, (256, 128))
    tk = _pick_tile(s, (512, 256, 128))
    tt_p = _pick_tile(s, (256, 128, 64))
    chunk_p = min(GDN_CHUNK, tt_p)
    chunk_s = 16
    sb_sel = _pick_tile(bd, (8, 4, 2))
    pg_sel = _pick_tile(n_pages, (8, 4, 2, 1))
    pg_att = _pick_tile(n_pages, (8, 4, 2, 1))

    yp = x_prompt.reshape(b * s, d)
    ys = x_sample.reshape(bd * t, d)
    outs_p, outs_s = [], []
    for l in range(depth):
        lw = _layer_weights(l, att_w, conv_ch, gdn_vw, n_gdn_heads, w_in, conv_w, idx_k_norm_g, idx_k_norm_b,
                            gdn_a_log, gdn_dt_bias, gdn_norm_g, w_out, pre_mix_g, post_mix_g, pre_mlp_g,
                            post_mlp_g, w_mlp_up, w_mlp_down)
        (q, kf, vf, kb, vb, vt, iq, ikf, ikb, qkv, z, sm) = _inproj(
            yp, tab_p, lw, att_w=att_w, conv_ch=conv_ch, gdn_vw=gdn_vw, tm=tm_p)
        r3 = lambda a: a.reshape(b, s, a.shape[-1])
        att = _prompt_attention(r3(q), r3(iq), r3(sm), r3(kb), vt, r3(ikb), tq=tq, tk=tk, topk=topk_p)
        o, s_fin, conv_out = _gdn(
            r3(qkv), r3(z), r3(sm), jnp.zeros((b, SUBLANES, conv_ch), F32),
            jnp.zeros((b, n_gdn_heads, dk, dv), F32), lw, tt=tt_p, chunk=chunk_p, t_valid=s)
        yp = _outmlp(yp, att.reshape(b * s, att_w), o.reshape(b * s, gdn_vw), lw, tm=tm_p, ff_chunk=1024)
        outs_p.append((kf.reshape(b, s, n_att_heads, head_dim), vf.reshape(b, s, n_att_heads, head_dim),
                       ikf.reshape(b, s, IDX_DIM), s_fin, conv_out[:, SUBLANES - hist:, :]))

        (q, kf, vf, kb, vb, vt, iq, ikf, ikb, qkv, z, sm) = _inproj(
            ys, tab_s, lw, att_w=att_w, conv_ch=conv_ch, gdn_vw=gdn_vw, tm=tm_s)
        iq_s = iq.reshape(bd, t * N_IDX_HEADS, IDX_DIM)
        w_s = sm[:, SM_IW:SM_IW + N_IDX_HEADS].reshape(bd, t * N_IDX_HEADS, 1)
        pad_new = lambda a: jnp.pad(a.reshape(bd, t, a.shape[-1]), ((0, 0), (0, LANES - t), (0, 0)))
        sel = _decode_select(page_table, iq_s, w_s, pad_new(ikb), jnp.swapaxes(cache_idx_k[l], 1, 2),
                             sb=sb_sel, pg=pg_sel, n_tok=t, topk=topk_s)
        sel8 = jnp.pad(sel.reshape(bd, t, sel.shape[-1]), ((0, 0), (0, SUBLANES - t), (0, 0)))
        q8 = jnp.pad(q.reshape(bd, t, n_att_heads, head_dim).transpose(0, 2, 1, 3),
                     ((0, 0), (0, 0), (0, SUBLANES - t), (0, 0))).reshape(bd, n_att_heads * SUBLANES, head_dim)
        att_s = _decode_attention(page_table, q8, sel8, pad_new(kb), pad_new(vb),
                                  cache_k[l].reshape(n_pool, page * n_att_heads, head_dim),
                                  cache_v[l].reshape(n_pool, page * n_att_heads, head_dim),
                                  pg=pg_att, n_tok=t, n_heads=n_att_heads)
        pad_t = lambda a: jnp.pad(a.reshape(bd, t, a.shape[-1]), ((0, 0), (0, chunk_s - t), (0, 0)))
        conv0 = jnp.pad(state_conv[l].astype(F32), ((0, 0), (SUBLANES - hist, 0), (0, 0)))
        o_s, s_fin_s, conv_out_s = _gdn(pad_t(qkv), pad_t(z), pad_t(sm), conv0, state_gdn[l].astype(F32), lw,
                                        tt=chunk_s, chunk=chunk_s, t_valid=t)
        ys = _outmlp(ys, att_s.reshape(bd * t, att_w), o_s[:, :t].reshape(bd * t, gdn_vw), lw,
                     tm=tm_s, ff_chunk=1024)
        outs_s.append((kf.reshape(bd, t, n_att_heads, head_dim), vf.reshape(bd, t, n_att_heads, head_dim),
                       ikf.reshape(bd, t, IDX_DIM), s_fin_s, conv_out_s[:, SUBLANES - hist:, :]))

    kp, vp, ikp, gp, cp = [jnp.stack([o_[i] for o_ in outs_p]) for i in range(5)]
    ks_, vs_, iks, gs, cs = [jnp.stack([o_[i] for o_ in outs_s]) for i in range(5)]
    return (yp.reshape(b, s, d), ys.reshape(bd, t, d), kp, vp, ikp, gp, cp, ks_, vs_, iks, gs, cs)
```

```python
import functools
import math

import jax
import jax.numpy as jnp
from jax import lax
from jax.experimental import pallas as pl
from jax.experimental.pallas import tpu as pltpu

F32 = jnp.float32
BF16 = jnp.bfloat16

HEAD_DIM = 128
IDX_DIM = 64
N_IDX_HEADS = 8
GDN_DK = 128
GDN_DV = 128
CONV_W = 4
TOPK_MAX = 256
GDN_CHUNK = 64
ROPE_THETA = 10000.0
NORM_EPS = 1e-6

LANES = 128
SUBLANES = 8
VMEM_LIMIT = 56 * 1024 * 1024
NEG_BIG = -1e30
BF16_EXACT_INT = 256
VALUE_STEPS = 28
MIN_NORMAL = 2.0 ** -126
SM_IW = IDX_DIM
SM_GB = IDX_DIM + N_IDX_HEADS
HIGHEST = lax.Precision.HIGHEST


def _dot(a, b):
    return jnp.dot(a, b, preferred_element_type=F32)


def _dot_nt(a, b):
    return lax.dot_general(a, b, (((1,), (1,)), ((), ())), preferred_element_type=F32)


def _dot_tn(a, b):
    return lax.dot_general(a, b, (((0,), (0,)), ((), ())), preferred_element_type=F32)


def _rms(x, g):
    return x * lax.rsqrt(jnp.mean(x * x, axis=-1, keepdims=True) + NORM_EPS) * g


def _const_spec(shape):
    n = len(shape)
    return pl.BlockSpec(shape, lambda *_: (0,) * n, pipeline_mode=pl.Buffered(1))


def _lane_fold(x, op):
    r = x[:, 0:LANES]
    for j in range(1, x.shape[1] // LANES):
        r = op(r, x[:, j * LANES:(j + 1) * LANES])
    return r


def _inproj_kernel(x_ref, g_ref, wa_ref, wqkv_ref, wz_ref, ws_ref,
                   cos_ref, sin_ref, cosi_ref, sina_ref, sinb_ref, lng_ref, lnb_ref, smul_ref,
                   q_ref, kf_ref, vf_ref, kb_ref, vb_ref, vt_ref, iq_ref, ikf_ref, ikb_ref,
                   qkv_ref, z_ref, sm_ref, *, att_w, q_scale):
    x = x_ref[...]
    h = _rms(x, g_ref[...]).astype(BF16)
    a = _dot(h, wa_ref[...])
    cos = cos_ref[...]
    sin = sin_ref[...]
    n_heads = att_w // HEAD_DIM

    def rope_full(xh):
        return xh * cos + pltpu.roll(xh, HEAD_DIM // 2, axis=1) * sin

    for j in range(n_heads):
        sl = slice(j * HEAD_DIM, (j + 1) * HEAD_DIM)
        q_ref[:, sl] = (rope_full(a[:, sl]) * q_scale).astype(BF16)
        kr = rope_full(a[:, att_w + j * HEAD_DIM: att_w + (j + 1) * HEAD_DIM])
        kf_ref[:, sl] = kr
        kb_ref[:, sl] = kr.astype(BF16)
    v = a[:, 2 * att_w:3 * att_w]
    vf_ref[...] = v
    vb_ref[...] = v.astype(BF16)
    vt_ref[0] = jnp.transpose(v).astype(BF16)

    cosi = cosi_ref[...]
    sina = sina_ref[...]
    sinb = sinb_ref[...]

    def rope_half(xh):
        return (xh * cosi + pltpu.roll(xh, LANES - IDX_DIM // 2, axis=1) * sina
                + pltpu.roll(xh, IDX_DIM // 2, axis=1) * sinb)

    idx_w = N_IDX_HEADS * IDX_DIM
    for j in range(idx_w // LANES):
        sl = slice(j * LANES, (j + 1) * LANES)
        iq_ref[:, sl] = rope_half(a[:, 3 * att_w + j * LANES: 3 * att_w + (j + 1) * LANES]).astype(BF16)

    qkv_ref[...] = _dot(h, wqkv_ref[...])
    z_ref[...] = _dot(h, wz_ref[...])

    sm = _dot(h, ws_ref[...])
    lane = lax.broadcasted_iota(jnp.int32, sm.shape, 1)
    is_ik = lane < IDX_DIM
    ikraw = jnp.where(is_ik, sm, 0.0)
    mu = jnp.sum(ikraw, axis=-1, keepdims=True) * (1.0 / IDX_DIM)
    xc = jnp.where(is_ik, sm - mu, 0.0)
    var = jnp.sum(xc * xc, axis=-1, keepdims=True) * (1.0 / IDX_DIM)
    ikn = xc * lax.rsqrt(var + NORM_EPS) * lng_ref[...] + lnb_ref[...]
    ikr = rope_half(ikn)
    ikf_ref[...] = ikr[:, :IDX_DIM]
    ikb_ref[...] = ikr[:, :IDX_DIM].astype(BF16)
    sm_ref[...] = sm * smul_ref[...]


def _rope_tables(pos):
    pos = pos.astype(F32)[:, None]
    half = HEAD_DIM // 2
    inv = jnp.power(ROPE_THETA, -jnp.arange(half, dtype=F32) / half)
    ang = pos * inv[None, :]
    c, s = jnp.cos(ang), jnp.sin(ang)
    cos = jnp.concatenate([c, c], axis=-1)
    sin = jnp.concatenate([-s, s], axis=-1)
    halfi = IDX_DIM // 2
    invi = jnp.power(ROPE_THETA, -jnp.arange(halfi, dtype=F32) / halfi)
    angi = pos * invi[None, :]
    ci, si = jnp.cos(angi), jnp.sin(angi)
    zi = jnp.zeros_like(si)
    cosi = jnp.concatenate([ci, ci, ci, ci], axis=-1)
    sina = jnp.concatenate([-si, zi, -si, zi], axis=-1)
    sinb = jnp.concatenate([zi, si, zi, si], axis=-1)
    return cos, sin, cosi, sina, sinb


def _inproj(x2d, tables, lw, *, att_w, conv_ch, gdn_vw, tm):
    n, d = x2d.shape
    idx_w = N_IDX_HEADS * IDX_DIM
    rt = tables[0].shape[0]
    assert n % tm == 0 and rt % tm == 0 and n % rt == 0
    nt = rt // tm
    wa_w = 3 * att_w + idx_w
    row = lambda w: pl.BlockSpec((tm, w), lambda i: (i, 0))
    tab = pl.BlockSpec((tm, LANES), lambda i: (i % nt, 0))
    out_shapes = (
        jax.ShapeDtypeStruct((n, att_w), BF16),
        jax.ShapeDtypeStruct((n, att_w), F32),
        jax.ShapeDtypeStruct((n, att_w), F32),
        jax.ShapeDtypeStruct((n, att_w), BF16),
        jax.ShapeDtypeStruct((n, att_w), BF16),
        jax.ShapeDtypeStruct((n // rt, att_w, rt), BF16),
        jax.ShapeDtypeStruct((n, idx_w), BF16),
        jax.ShapeDtypeStruct((n, IDX_DIM), F32),
        jax.ShapeDtypeStruct((n, IDX_DIM), BF16),
        jax.ShapeDtypeStruct((n, conv_ch), F32),
        jax.ShapeDtypeStruct((n, gdn_vw), F32),
        jax.ShapeDtypeStruct((n, LANES), F32),
    )
    vt_spec = pl.BlockSpec((1, att_w, tm), lambda i: (i // nt, 0, i % nt))
    out_specs = (row(att_w), row(att_w), row(att_w), row(att_w), row(att_w), vt_spec, row(idx_w),
                 row(IDX_DIM), row(IDX_DIM), row(conv_ch), row(gdn_vw), row(LANES))
    in_specs = [row(d), _const_spec((1, d)), _const_spec((d, wa_w)), _const_spec((d, conv_ch)),
                _const_spec((d, gdn_vw)), _const_spec((d, LANES)),
                tab, tab, tab, tab, tab,
                _const_spec((1, LANES)), _const_spec((1, LANES)), _const_spec((1, LANES))]
    return pl.pallas_call(
        functools.partial(_inproj_kernel, att_w=att_w, q_scale=HEAD_DIM ** -0.5),
        grid=(n // tm,),
        in_specs=in_specs, out_specs=out_specs, out_shape=out_shapes,
        compiler_params=pltpu.CompilerParams(dimension_semantics=("arbitrary",),
                                             vmem_limit_bytes=VMEM_LIMIT),
        name="inproj",
    )(x2d, lw["pre_mix_g"], lw["wa"], lw["wqkv"], lw["wz"], lw["ws"], *tables,
      lw["lng"], lw["lnb"], lw["smul"])


def _select_threshold(count, rewrite, n_valid, row_max, row_min, topk, col_limit):
    kf = float(topk)
    flip = jnp.int32(0x7FFFFFFF)

    def to_key(x):
        b = lax.bitcast_convert_type(jnp.where(x == 0.0, 0.0, x), jnp.int32)
        return jnp.where(b < 0, b ^ flip, b)

    def from_key(k):
        min_normal = jnp.int32(0x00800000)
        k = jnp.where(jnp.logical_and(k > 0, k < min_normal), min_normal, k)
        return lax.bitcast_convert_type(jnp.where(k < 0, k ^ flip, k), F32)

    need = n_valid > kf
    spread = jnp.maximum(row_max - row_min, jnp.abs(row_max) * (2.0 ** -20) + 2.0 ** -100)
    c_ge0 = count(lambda x, col: x >= 0.0)
    c_gt0 = count(lambda x, col: x > 0.0)
    non_neg = jnp.logical_and(need, c_ge0 >= kf)
    at_zero = jnp.logical_and(non_neg, c_gt0 < kf)
    negative = jnp.logical_and(need, c_ge0 < kf)
    lo0 = jnp.where(non_neg, 0.0, jnp.where(need, row_min, NEG_BIG))
    c_lo0 = jnp.where(non_neg, c_ge0, n_valid)
    hi0 = jnp.where(at_zero, MIN_NORMAL, jnp.where(negative, 0.0, row_max + spread))
    c_hi0 = jnp.where(at_zero, c_gt0, jnp.where(negative, c_ge0, 0.0))
    open0 = jnp.logical_and(need, jnp.logical_not(jnp.logical_or(at_zero, c_lo0 == kf)))
    done0 = jnp.where(open0, 0.0, 1.0)

    def narrow(c, mid, stuck, cnt):
        lo, hi, c_lo, c_hi, done = c
        ge = cnt >= kf
        upd = jnp.logical_and(done < 0.5, jnp.logical_not(stuck))
        up_lo = jnp.logical_and(upd, ge)
        up_hi = jnp.logical_and(upd, jnp.logical_not(ge))
        fin = jnp.logical_or(stuck, jnp.logical_and(upd, cnt == kf))
        return (jnp.where(up_lo, mid, lo), jnp.where(up_hi, mid, hi), jnp.where(up_lo, cnt, c_lo),
                jnp.where(up_hi, cnt, c_hi), jnp.where(fin, 1.0, done))

    def value_step(c):
        it, st = c
        lo, hi = st[0], st[1]
        mid = 0.5 * lo + 0.5 * hi
        stuck = jnp.logical_or(mid <= lo, mid >= hi)
        return it + 1, narrow(st, mid, stuck, count(lambda x, col: x >= mid))

    _, (lo, hi, c_lo, c_hi, done) = lax.while_loop(
        lambda c: jnp.logical_and(c[0] < VALUE_STEPS, jnp.min(c[1][4]) < 0.5), value_step,
        (jnp.int32(0), (lo0, hi0, c_lo0, c_hi0, done0)))

    def key_step(st):
        klo, khi = st[0], st[1]
        mid = (klo & khi) + ((klo ^ khi) >> 1)
        return narrow(st, mid, mid <= klo, count(lambda x, col: to_key(x) >= mid))

    klo, khi, c_lo, c_hi, _ = lax.while_loop(
        lambda st: jnp.min(st[4]) < 0.5, key_step, (to_key(lo), to_key(hi), c_lo, c_hi, done))
    lo, hi = from_key(klo), from_key(khi)

    tied = jnp.logical_and(need, c_lo > kf)

    @pl.when(jnp.max(jnp.where(tied, 1.0, 0.0)) > 0.5)
    def _():
        want = kf - c_hi

        def in_tie(x):
            return jnp.logical_and(x >= lo, x < hi)

        def jstep(_, c):
            jlo, jhi = c
            jm = jnp.floor((jlo + jhi) * 0.5)
            cnt = count(lambda x, col: jnp.logical_and(in_tie(x), col <= jm))
            ok = cnt >= want
            return jnp.where(ok, jlo, jm), jnp.where(ok, jm, jhi)

        n_it = max(1, math.ceil(math.log2(col_limit + 1)))
        j0 = (jnp.full_like(lo, -1.0), jnp.full_like(lo, float(col_limit - 1)))
        _, jcut = lax.fori_loop(0, n_it, jstep, j0)
        rewrite(lambda x, col: jnp.where(
            jnp.logical_and(tied, jnp.logical_and(in_tie(x), col > jcut)), -jnp.inf, x))

    return lo


def _pattn_kernel(q_ref, iq_ref, sm_ref, k_ref, vt_ref, ik_ref, o_ref, sc_ref, m_ref, l_ref, acc_ref,
                  *, tq, tk, topk, seq, n_heads):
    qi = pl.program_id(1)
    t0 = qi * tq
    nkb = (t0 + tq + tk - 1) // tk
    sub_tiles = tk // SUBLANES
    q_t = t0 + lax.broadcasted_iota(jnp.int32, (1, tq), 1)
    smt = jnp.transpose(sm_ref[0])
    w_rows = [smt[SM_IW + h:SM_IW + h + 1, :] for h in range(N_IDX_HEADS)]
    iq = iq_ref[0]
    iq_heads = [iq[:, h * IDX_DIM:(h + 1) * IDX_DIM] for h in range(N_IDX_HEADS)]
    sub_iota = lax.broadcasted_iota(jnp.int32, (SUBLANES, tq), 0)

    def sub_fold(x, op):
        r = x[0:SUBLANES]
        for j in range(1, x.shape[0] // SUBLANES):
            r = op(r, x[j * SUBLANES:(j + 1) * SUBLANES])
        return r

    def rep(row):
        return jnp.broadcast_to(row, (SUBLANES, tq))

    def score_block(kb, carry):
        mx, mn = carry
        k0 = pl.multiple_of(kb * tk, tk)
        ikb = ik_ref[0, pl.ds(k0, tk), :]
        acc = jnp.zeros((tk, tq), F32)
        for h in range(N_IDX_HEADS):
            acc = acc + w_rows[h] * jnp.maximum(_dot_nt(ikb, iq_heads[h]), 0.0)
        key = k0 + lax.broadcasted_iota(jnp.int32, (tk, 1), 0)
        valid = key <= q_t
        sc_ref[pl.ds(k0, tk), :] = jnp.where(valid, acc, -jnp.inf)
        mx = jnp.maximum(mx, sub_fold(jnp.where(valid, acc, -jnp.inf), jnp.maximum))
        mn = jnp.minimum(mn, sub_fold(jnp.where(valid, acc, jnp.inf), jnp.minimum))
        return mx, mn

    mx, mn = lax.fori_loop(0, nkb, score_block,
                           (jnp.full((SUBLANES, tq), -jnp.inf, F32), jnp.full((SUBLANES, tq), jnp.inf, F32)))
    row_max = rep(jnp.max(mx, axis=0, keepdims=True))
    row_min = rep(jnp.min(mn, axis=0, keepdims=True))
    n_valid = rep((q_t + 1).astype(F32))

    def count(pred):
        def body(kb, acc):
            k0 = pl.multiple_of(kb * tk, tk)
            x = sc_ref[pl.ds(k0, tk), :]
            for j in range(sub_tiles):
                hit = pred(x[j * SUBLANES:(j + 1) * SUBLANES], (k0 + j * SUBLANES + sub_iota).astype(F32))
                acc = acc + jnp.where(hit, 1.0, 0.0)
            return acc
        acc = lax.fori_loop(0, nkb, body, jnp.zeros((SUBLANES, tq), F32))
        return rep(jnp.sum(acc, axis=0, keepdims=True))

    def rewrite(fn):
        def body(kb, carry):
            k0 = pl.multiple_of(kb * tk, tk)
            x = sc_ref[pl.ds(k0, tk), :]
            sc_ref[pl.ds(k0, tk), :] = jnp.concatenate(
                [fn(x[j * SUBLANES:(j + 1) * SUBLANES], (k0 + j * SUBLANES + sub_iota).astype(F32))
                 for j in range(sub_tiles)], axis=0)
            return carry
        lax.fori_loop(0, nkb, body, 0)

    lo = _select_threshold(count, rewrite, n_valid, row_max, row_min, topk, seq)
    tile_rows = lambda v8: jnp.concatenate([v8] * sub_tiles, axis=0)
    lo_t = tile_rows(lo)

    def masked_scores(kb, h):
        k0 = pl.multiple_of(kb * tk, tk)
        sl = slice(h * HEAD_DIM, (h + 1) * HEAD_DIM)
        s = _dot_nt(k_ref[0, pl.ds(k0, tk), sl], q_ref[0, :, sl])
        return jnp.where(sc_ref[pl.ds(k0, tk), :] >= lo_t, s, NEG_BIG)

    m_ref[...] = jnp.full(m_ref.shape, NEG_BIG, F32)

    def max_block(kb, carry):
        for h in range(n_heads):
            hs = slice(h * SUBLANES, (h + 1) * SUBLANES)
            m_ref[hs, :] = jnp.maximum(m_ref[hs, :], sub_fold(masked_scores(kb, h), jnp.maximum))
        return carry
    lax.fori_loop(0, nkb, max_block, 0)
    m_t = [tile_rows(rep(jnp.max(m_ref[h * SUBLANES:(h + 1) * SUBLANES, :], axis=0, keepdims=True)))
           for h in range(n_heads)]

    l_ref[...] = jnp.zeros_like(l_ref)
    acc_ref[...] = jnp.zeros_like(acc_ref)

    def pv_block(kb, carry):
        k0 = pl.multiple_of(kb * tk, tk)
        ps = [jnp.exp(s - m_t[h]) for h, s in enumerate([masked_scores(kb, h) for h in range(n_heads)])]
        for h in range(n_heads):
            hs = slice(h * SUBLANES, (h + 1) * SUBLANES)
            sl = slice(h * HEAD_DIM, (h + 1) * HEAD_DIM)
            l_ref[hs, :] = l_ref[hs, :] + sub_fold(ps[h], jnp.add)
            acc_ref[sl, :] = acc_ref[sl, :] + _dot(vt_ref[0, sl, pl.ds(k0, tk)], ps[h].astype(BF16))
        return carry
    lax.fori_loop(0, nkb, pv_block, 0)

    for h in range(n_heads):
        sl = slice(h * HEAD_DIM, (h + 1) * HEAD_DIM)
        l_row = jnp.sum(l_ref[h * SUBLANES:(h + 1) * SUBLANES, :], axis=0, keepdims=True)
        o_ref[0, :, sl] = jnp.transpose(acc_ref[sl, :] / l_row).astype(o_ref.dtype)


def _prompt_attention(q, iq, sm, kb, vt, ikb, *, tq, tk, topk):
    b, s, att_w = q.shape
    idx_w = iq.shape[-1]
    n_heads = att_w // HEAD_DIM
    assert s % tq == 0 and s % tk == 0 and tk % tq == 0 and tq % LANES == 0
    blk = lambda w: pl.BlockSpec((1, tq, w), lambda bi, qi: (bi, qi, 0))
    full = lambda r, w: pl.BlockSpec((1, r, w), lambda bi, qi: (bi, 0, 0), pipeline_mode=pl.Buffered(1))
    return pl.pallas_call(
        functools.partial(_pattn_kernel, tq=tq, tk=tk, topk=topk, seq=s, n_heads=n_heads),
        grid=(b, s // tq),
        in_specs=[blk(att_w), blk(idx_w), blk(LANES), full(s, att_w), full(att_w, s), full(s, IDX_DIM)],
        out_specs=blk(att_w),
        out_shape=jax.ShapeDtypeStruct((b, s, att_w), BF16),
        scratch_shapes=[pltpu.VMEM((s, tq), F32), pltpu.VMEM((n_heads * SUBLANES, tq), F32),
                        pltpu.VMEM((n_heads * SUBLANES, tq), F32), pltpu.VMEM((att_w, tq), F32)],
        compiler_params=pltpu.CompilerParams(dimension_semantics=("arbitrary", "arbitrary"),
                                             vmem_limit_bytes=VMEM_LIMIT),
        name="prompt_attention",
    )(q, iq, sm, kb, vt, ikb)


def _dsel_kernel(pt_ref, iq_ref, w_ref, iknew_ref, *rest, sb, pg, n_tok, past, topk):
    pages = rest[:sb * pg]
    sel_ref = rest[sb * pg]
    sc_ref = rest[sb * pg + 1]
    c = pl.program_id(1)
    n_c = pl.num_programs(1)
    page = pages[0].shape[2]
    rows, l_pad = sc_ref.shape

    def token_rows(s):
        return [jnp.sum(s[t * N_IDX_HEADS:(t + 1) * N_IDX_HEADS, :], axis=0, keepdims=True)
                for t in range(n_tok)]

    @pl.when(c == 0)
    def _():
        sc_ref[...] = jnp.full(sc_ref.shape, -jnp.inf, F32)

    for j in range(sb):
        iq = iq_ref[j]
        w = w_ref[j]
        for p in range(pg):
            kt = pages[j * pg + p][0].astype(BF16)
            tr = token_rows(jnp.maximum(_dot(iq, kt), 0.0) * w)
            k0 = pl.multiple_of((c * pg + p) * page, page)
            for t in range(n_tok):
                sc_ref[j * n_tok + t:j * n_tok + t + 1, pl.ds(k0, page)] = tr[t]

    @pl.when(c == n_c - 1)
    def _():
        col = lax.broadcasted_iota(jnp.int32, (1, LANES), 1)
        for j in range(sb):
            tr = token_rows(jnp.maximum(_dot_nt(iq_ref[j], iknew_ref[j]), 0.0) * w_ref[j])
            for t in range(n_tok):
                sc_ref[j * n_tok + t:j * n_tok + t + 1, past:past + LANES] = jnp.where(col <= t, tr[t], -jnp.inf)
        x = sc_ref[...]
        rep = lambda v: jnp.broadcast_to(v, (rows, LANES))
        row_max = rep(jnp.max(x, axis=-1, keepdims=True))
        row_min = rep(jnp.min(jnp.where(x > -jnp.inf, x, jnp.inf), axis=-1, keepdims=True))
        r = lax.broadcasted_iota(jnp.int32, (rows, LANES), 0).astype(F32)
        tok = r - n_tok * jnp.floor(r * (1.0 / n_tok))
        n_valid = past + 1.0 + tok
        n_blocks = l_pad // LANES
        assert n_blocks <= BF16_EXACT_INT
        ones_b = jnp.ones((LANES, LANES), BF16)
        lane_iota = lax.broadcasted_iota(jnp.int32, (rows, LANES), 1)

        def count(pred):
            def body(kb, acc):
                c0 = pl.multiple_of(kb * LANES, LANES)
                hit = pred(sc_ref[:, pl.ds(c0, LANES)], (c0 + lane_iota).astype(F32))
                return acc + jnp.where(hit, 1.0, 0.0)
            acc = lax.fori_loop(0, n_blocks, body, jnp.zeros((rows, LANES), F32))
            return _dot(acc.astype(BF16), ones_b)

        def rewrite(fn):
            def body(kb, carry):
                c0 = pl.multiple_of(kb * LANES, LANES)
                sc_ref[:, pl.ds(c0, LANES)] = fn(sc_ref[:, pl.ds(c0, LANES)], (c0 + lane_iota).astype(F32))
                return carry
            lax.fori_loop(0, n_blocks, body, 0)

        lo = _select_threshold(count, rewrite, n_valid, row_max, row_min, topk, l_pad)

        def emit(kb, carry):
            c0 = pl.multiple_of(kb * LANES, LANES)
            sel_ref[:, pl.ds(c0, LANES)] = jnp.where(sc_ref[:, pl.ds(c0, LANES)] >= lo, 1.0, 0.0)
            return carry
        lax.fori_loop(0, l_pad // LANES, emit, 0)


def _decode_select(page_table, iq, w, iknew, cache_idx_t, *, sb, pg, n_tok, topk):
    bd, n_pages = page_table.shape
    _, di, page = cache_idx_t.shape
    past = n_pages * page
    assert n_pages % pg == 0 and bd % sb == 0 and (sb * n_tok) % SUBLANES == 0
    l_pad = past + LANES
    rows = iq.shape[1]

    def page_spec(j, p):
        return pl.BlockSpec((1, di, page), lambda g, c, pt: (pt[g * sb + j, c * pg + p], 0, 0))

    grid_spec = pltpu.PrefetchScalarGridSpec(
        num_scalar_prefetch=1,
        grid=(bd // sb, n_pages // pg),
        in_specs=[pl.BlockSpec((sb, rows, di), lambda g, c, pt: (g, 0, 0)),
                  pl.BlockSpec((sb, rows, 1), lambda g, c, pt: (g, 0, 0)),
                  pl.BlockSpec((sb, LANES, di), lambda g, c, pt: (g, 0, 0))]
                 + [page_spec(j, p) for j in range(sb) for p in range(pg)],
        out_specs=pl.BlockSpec((sb * n_tok, l_pad), lambda g, c, pt: (g, 0)),
        scratch_shapes=[pltpu.VMEM((sb * n_tok, l_pad), F32)],
    )
    return pl.pallas_call(
        functools.partial(_dsel_kernel, sb=sb, pg=pg, n_tok=n_tok, past=past, topk=topk),
        grid_spec=grid_spec,
        out_shape=jax.ShapeDtypeStruct((bd * n_tok, l_pad), F32),
        compiler_params=pltpu.CompilerParams(dimension_semantics=("arbitrary", "arbitrary"),
                                             vmem_limit_bytes=VMEM_LIMIT),
        name="decode_select",
    )(page_table, iq, w, iknew, *([cache_idx_t] * (sb * pg)))


def _dattn_kernel(pt_ref, q_ref, sel_ref, knew_ref, vnew_ref, *rest, pg, n_tok, n_heads, past):
    kpages = rest[:pg]
    vpages = rest[pg:2 * pg]
    o_ref = rest[2 * pg]
    m_ref, l_ref, acc_ref = rest[2 * pg + 1:]
    c = pl.program_id(1)
    n_c = pl.num_programs(1)
    page = kpages[0].shape[1] // n_heads
    qh = [q_ref[0, h * SUBLANES:(h + 1) * SUBLANES, :] for h in range(n_heads)]

    @pl.when(c == 0)
    def _():
        m_ref[...] = jnp.full(m_ref.shape, NEG_BIG, F32)
        l_ref[...] = jnp.zeros_like(l_ref)
        acc_ref[...] = jnp.zeros_like(acc_ref)

    def update(k_of, v_of, n_blk, sel):
        s = jnp.concatenate(
            [jnp.concatenate([_dot_nt(qh[h], k_of(p, h)) for p in range(n_blk)], axis=1)
             for h in range(n_heads)], axis=0)
        s = jnp.where(jnp.concatenate([sel] * n_heads, axis=0) > 0.5, s, NEG_BIG)
        m_old = m_ref[...]
        m_new = jnp.maximum(m_old, jnp.max(s, axis=-1, keepdims=True))
        pr = jnp.exp(s - m_new).astype(BF16)
        alpha = jnp.exp(m_old - m_new)
        l_ref[...] = alpha * l_ref[...] + jnp.sum(pr.astype(F32), axis=-1, keepdims=True)
        pv = []
        for h in range(n_heads):
            a = jnp.zeros((SUBLANES, HEAD_DIM), F32)
            for p in range(n_blk):
                a = a + _dot(pr[h * SUBLANES:(h + 1) * SUBLANES, p * page:(p + 1) * page], v_of(p, h))
            pv.append(a)
        acc_ref[...] = alpha * acc_ref[...] + jnp.concatenate(pv, axis=0)
        m_ref[...] = m_new

    head_rows = lambda ref, h: ref[0, pl.ds(h, page, stride=n_heads), :].astype(BF16)
    k0 = pl.multiple_of(c * pg * page, pg * page)
    update(lambda p, h: head_rows(kpages[p], h), lambda p, h: head_rows(vpages[p], h), pg,
           sel_ref[0, :, pl.ds(k0, pg * page)])

    @pl.when(c == n_c - 1)
    def _():
        new = lambda ref, h: ref[0, :, h * HEAD_DIM:(h + 1) * HEAD_DIM]
        update(lambda p, h: new(knew_ref, h), lambda p, h: new(vnew_ref, h), 1,
               sel_ref[0, :, past:past + LANES])
        inv_l = 1.0 / l_ref[...]
        out = acc_ref[...] * inv_l
        for h in range(n_heads):
            o_ref[0, :, h * HEAD_DIM:(h + 1) * HEAD_DIM] = out[h * SUBLANES:h * SUBLANES + n_tok, :].astype(o_ref.dtype)


def _decode_attention(page_table, q8, sel, knew, vnew, cache_k, cache_v, *, pg, n_tok, n_heads):
    bd, n_pages = page_table.shape
    _, page_rows, hd = cache_k.shape
    page = page_rows // n_heads
    past = n_pages * page
    l_pad = sel.shape[-1]
    att_w = n_heads * hd
    rows = n_heads * SUBLANES
    assert n_pages % pg == 0 and page == LANES

    def page_spec(p):
        return pl.BlockSpec((1, page_rows, hd), lambda b, c, pt: (pt[b, c * pg + p], 0, 0))

    per_seq = lambda r, w: pl.BlockSpec((1, r, w), lambda b, c, pt: (b, 0, 0))
    grid_spec = pltpu.PrefetchScalarGridSpec(
        num_scalar_prefetch=1,
        grid=(bd, n_pages // pg),
        in_specs=[per_seq(rows, hd), per_seq(SUBLANES, l_pad), per_seq(LANES, att_w), per_seq(LANES, att_w)]
                 + [page_spec(p) for p in range(pg)] * 2,
        out_specs=per_seq(n_tok, att_w),
        scratch_shapes=[pltpu.VMEM((rows, 1), F32), pltpu.VMEM((rows, 1), F32), pltpu.VMEM((rows, hd), F32)],
    )
    return pl.pallas_call(
        functools.partial(_dattn_kernel, pg=pg, n_tok=n_tok, n_heads=n_heads, past=past),
        grid_spec=grid_spec,
        out_shape=jax.ShapeDtypeStruct((bd, n_tok, att_w), BF16),
        compiler_params=pltpu.CompilerParams(dimension_semantics=("arbitrary", "arbitrary"),
                                             vmem_limit_bytes=VMEM_LIMIT),
        name="decode_attention",
    )(page_table, q8, sel, knew, vnew, *([cache_k] * pg), *([cache_v] * pg))


def _gdn_kernel(x_ref, z_ref, sm_ref, conv0_ref, s0_ref, cw_ref, alog_ref, dtb_ref, ng_ref,
                o_ref, sfin_ref, convout_ref,
                xbuf, st_ref, q_s, k_s, v_s, beta_s, g_s,
                *, tt, chunk, n_heads, t_valid, n_steps):
    step = pl.program_id(1)
    hist = CONV_W - 1
    qk_w = n_heads * GDN_DK

    @pl.when(step == 0)
    def _():
        xbuf[0:SUBLANES, :] = conv0_ref[0]
        st_ref[...] = s0_ref[0]

    @pl.when(step > 0)
    def _():
        xbuf[0:SUBLANES, :] = xbuf[tt:tt + SUBLANES, :]

    xbuf[SUBLANES:SUBLANES + tt, :] = x_ref[0]
    cw = cw_ref[...]
    conv = cw[hist:hist + 1, :] * xbuf[SUBLANES:SUBLANES + tt, :]
    for j in range(hist):
        conv = conv + cw[j:j + 1, :] * xbuf[SUBLANES - hist + j:SUBLANES - hist + j + tt, :]
    act = conv * jax.nn.sigmoid(conv)

    def l2n(xh):
        return xh * lax.rsqrt(jnp.sum(xh * xh, axis=-1, keepdims=True) + NORM_EPS)

    for h in range(n_heads):
        sl = slice(h * GDN_DK, (h + 1) * GDN_DK)
        q_s[:, sl] = l2n(act[:, sl]) * (GDN_DK ** -0.5)
        k_s[:, sl] = l2n(act[:, qk_w + h * GDN_DK: qk_w + (h + 1) * GDN_DK])
    v_s[...] = act[:, 2 * qk_w:]

    smv = sm_ref[0]
    t_glob = step * tt + lax.broadcasted_iota(jnp.int32, (tt, 1), 0)
    live = t_glob < t_valid
    beta_s[...] = jnp.where(live, jax.nn.sigmoid(smv), 0.0)
    g_s[...] = jnp.where(live, -jnp.exp(alog_ref[...]) * jax.nn.softplus(smv + dtb_ref[...]), 0.0)

    ii = lax.broadcasted_iota(jnp.int32, (chunk, chunk), 0)
    jj = lax.broadcasted_iota(jnp.int32, (chunk, chunk), 1)
    tril = ii >= jj
    strict = ii > jj
    tril_f = jnp.where(tril, 1.0, 0.0)
    eye = jnp.where(ii == jj, 1.0, 0.0)
    lane8 = lax.broadcasted_iota(jnp.int32, (SUBLANES, LANES), 1)
    row8 = lax.broadcasted_iota(jnp.int32, (SUBLANES, LANES), 0)
    g_lane0 = SM_GB + n_heads
    pick_g = jnp.where(lane8 == g_lane0 + row8, 1.0, 0.0)
    n_levels = int(math.log2(chunk))
    ng = ng_ref[...]

    n_chunks = tt // chunk
    units = [(ci, h) for ci in range(n_chunks) for h in range(n_heads)]
    rows_of = lambda ci: slice(ci * chunk, (ci + 1) * chunk)
    lanes_of = lambda h: slice(h * GDN_DK, (h + 1) * GDN_DK)
    gcs = [jnp.dot(tril_f, g_s[rows_of(ci), :], precision=HIGHEST, preferred_element_type=F32)
           for ci in range(n_chunks)]
    gcrs = [lax.dot_general(pick_g, gc, (((1,), (1,)), ((), ())), precision=HIGHEST,
                            preferred_element_type=F32) for gc in gcs]
    gcol = {(ci, h): gcs[ci][:, g_lane0 + h:g_lane0 + h + 1] for ci, h in units}
    bcol = {(ci, h): beta_s[rows_of(ci), SM_GB + h:SM_GB + h + 1] for ci, h in units}
    decay = {u: jnp.exp(jnp.where(tril, gcol[u] - gcrs[u[0]][u[1]:u[1] + 1, :], -jnp.inf)) for u in units}
    k_b = {(ci, h): k_s[rows_of(ci), lanes_of(h)].astype(BF16) for ci, h in units}
    kb_f = {(ci, h): k_s[rows_of(ci), lanes_of(h)] * bcol[(ci, h)] for ci, h in units}
    lmat = {u: jnp.where(strict, _dot_nt(kb_f[u].astype(BF16), k_b[u]) * decay[u], 0.0) for u in units}
    qk = {(ci, h): (_dot_nt(q_s[rows_of(ci), lanes_of(h)].astype(BF16), k_b[(ci, h)]) * decay[(ci, h)]).astype(BF16)
          for ci, h in units}
    pw = {u: -lmat[u] for u in units}
    inv = {u: eye + pw[u] for u in units}
    for _ in range(n_levels - 1):
        pw = {u: _dot(pw[u].astype(BF16), pw[u].astype(BF16)) for u in units}
        inv = {u: inv[u] + _dot(inv[u].astype(BF16), pw[u].astype(BF16)) for u in units}
    inv_b = {u: inv[u].astype(BF16) for u in units}
    u_rhs = {(ci, h): _dot(inv_b[(ci, h)], (v_s[rows_of(ci), lanes_of(h)] * bcol[(ci, h)]).astype(BF16))
             for ci, h in units}
    w_rhs = {u: _dot(inv_b[u], (kb_f[u] * jnp.exp(gcol[u])).astype(BF16)).astype(BF16) for u in units}

    state = [st_ref[h] for h in range(n_heads)]
    for ci in range(n_chunks):
        heads = [(ci, h) for h in range(n_heads)]
        s_b = [s.astype(BF16) for s in state]
        ws = [_dot(w_rhs[u], s_b[u[1]]) for u in heads]
        qs = [_dot((q_s[rows_of(ci), lanes_of(h)] * jnp.exp(gcol[(ci, h)])).astype(BF16), s_b[h]) for _, h in heads]
        v_new = [(u_rhs[u] - ws[u[1]]).astype(BF16) for u in heads]
        outs = [qs[h] + _dot(qk[(ci, h)], v_new[h]) for _, h in heads]
        g_last = [gcs[ci][chunk - 1:chunk, g_lane0 + h:g_lane0 + h + 1] for _, h in heads]
        k_dec = [(k_s[rows_of(ci), lanes_of(h)] * jnp.exp(g_last[h] - gcol[(ci, h)])).astype(BF16) for _, h in heads]
        state = [state[h] * jnp.exp(g_last[h]) + _dot_tn(k_dec[h], v_new[h]) for _, h in heads]
        for _, h in heads:
            zh = z_ref[0, rows_of(ci), lanes_of(h)]
            o_ref[0, rows_of(ci), lanes_of(h)] = (_rms(outs[h], ng) * (zh * jax.nn.sigmoid(zh))).astype(o_ref.dtype)
    for h in range(n_heads):
        st_ref[h] = state[h]

    @pl.when(step == n_steps - 1)
    def _():
        sfin_ref[0] = st_ref[...]
        last = t_valid - (n_steps - 1) * tt
        convout_ref[0] = xbuf[last:last + SUBLANES, :]


def _gdn(x, z, sm, conv0, s0, lw, *, tt, chunk, t_valid):
    nb, t_pad, conv_ch = x.shape
    n_heads = s0.shape[1]
    vw = z.shape[-1]
    assert t_pad % tt == 0 and tt % chunk == 0
    n_steps = t_pad // tt
    assert 0 < t_valid - (n_steps - 1) * tt <= tt and t_valid >= CONV_W - 1
    tok = lambda w: pl.BlockSpec((1, tt, w), lambda b, s: (b, s, 0))
    return pl.pallas_call(
        functools.partial(_gdn_kernel, tt=tt, chunk=chunk, n_heads=n_heads, t_valid=t_valid, n_steps=n_steps),
        grid=(nb, n_steps),
        in_specs=[tok(conv_ch), tok(vw), tok(LANES),
                  pl.BlockSpec((1, SUBLANES, conv_ch), lambda b, s: (b, 0, 0)),
                  pl.BlockSpec((1, n_heads, GDN_DK, GDN_DV), lambda b, s: (b, 0, 0, 0)),
                  pl.BlockSpec((SUBLANES, conv_ch), lambda b, s: (0, 0)),
                  pl.BlockSpec((1, LANES), lambda b, s: (0, 0)),
                  pl.BlockSpec((1, LANES), lambda b, s: (0, 0)),
                  pl.BlockSpec((1, GDN_DV), lambda b, s: (0, 0))],
        out_specs=(tok(vw),
                   pl.BlockSpec((1, n_heads, GDN_DK, GDN_DV), lambda b, s: (b, 0, 0, 0)),
                   pl.BlockSpec((1, SUBLANES, conv_ch), lambda b, s: (b, 0, 0))),
        out_shape=(jax.ShapeDtypeStruct((nb, t_pad, vw), BF16),
                   jax.ShapeDtypeStruct((nb, n_heads, GDN_DK, GDN_DV), F32),
                   jax.ShapeDtypeStruct((nb, SUBLANES, conv_ch), F32)),
        scratch_shapes=[pltpu.VMEM((tt + 2 * SUBLANES, conv_ch), F32),
                        pltpu.VMEM((n_heads, GDN_DK, GDN_DV), F32),
                        pltpu.VMEM((tt, n_heads * GDN_DK), F32),
                        pltpu.VMEM((tt, n_heads * GDN_DK), F32),
                        pltpu.VMEM((tt, vw), F32),
                        pltpu.VMEM((tt, LANES), F32),
                        pltpu.VMEM((tt, LANES), F32)],
        compiler_params=pltpu.CompilerParams(dimension_semantics=("arbitrary", "arbitrary"),
                                             vmem_limit_bytes=VMEM_LIMIT),
        name="gated_delta",
    )(x, z, sm, conv0, s0, lw["conv_w"], lw["alog"], lw["dtb"], lw["gdn_norm_g"])


def _outmlp_kernel(x_ref, att_ref, o_ref, woa_ref, wob_ref, g1_ref, g2_ref, g3_ref, wup_ref, wdn_ref,
                   y_ref, *, ff_chunk):
    x = x_ref[...]
    mix = _dot(att_ref[...], woa_ref[...]) + _dot(o_ref[...], wob_ref[...])
    x1 = x + _rms(mix, g1_ref[...])
    h2 = _rms(x1, g2_ref[...]).astype(BF16)
    d_ff = wup_ref.shape[1]
    ff = jnp.zeros(x.shape, F32)
    for c in range(d_ff // ff_chunk):
        u = jnp.maximum(_dot(h2, wup_ref[:, c * ff_chunk:(c + 1) * ff_chunk]), 0.0)
        ff = ff + _dot((u * u).astype(BF16), wdn_ref[c * ff_chunk:(c + 1) * ff_chunk, :])
    y_ref[...] = x1 + _rms(ff, g3_ref[...])


def _outmlp(x2d, att, o, lw, *, tm, ff_chunk):
    n, d = x2d.shape
    att_w = att.shape[1]
    vw = o.shape[1]
    d_ff = lw["wup"].shape[1]
    assert n % tm == 0 and d_ff % ff_chunk == 0
    row = lambda w: pl.BlockSpec((tm, w), lambda i: (i, 0))
    return pl.pallas_call(
        functools.partial(_outmlp_kernel, ff_chunk=ff_chunk),
        grid=(n // tm,),
        in_specs=[row(d), row(att_w), row(vw), _const_spec((att_w, d)), _const_spec((vw, d)),
                  _const_spec((1, d)), _const_spec((1, d)), _const_spec((1, d)),
                  _const_spec((d, d_ff)), _const_spec((d_ff, d))],
        out_specs=row(d),
        out_shape=jax.ShapeDtypeStruct((n, d), F32),
        compiler_params=pltpu.CompilerParams(dimension_semantics=("arbitrary",),
                                             vmem_limit_bytes=VMEM_LIMIT),
        name="outproj_mlp",
    )(x2d, att, o, lw["woa"], lw["wob"], lw["post_mix_g"], lw["pre_mlp_g"], lw["post_mlp_g"],
      lw["wup"], lw["wdn"])


def _layer_weights(l, att_w, conv_ch, gdn_vw, n_gdn_heads, w_in, conv_w, idx_k_norm_g, idx_k_norm_b,
                   gdn_a_log, gdn_dt_bias, gdn_norm_g, w_out, pre_mix_g, post_mix_g, pre_mlp_g,
                   post_mlp_g, w_mlp_up, w_mlp_down):
    idx_w = N_IDX_HEADS * IDX_DIM
    o_ik = 3 * att_w + idx_w
    o_iw = o_ik + IDX_DIM
    o_qkv = o_iw + N_IDX_HEADS
    o_z = o_qkv + conv_ch
    o_gb = o_z + gdn_vw
    o_ga = o_gb + n_gdn_heads
    wi = w_in[l]
    assert wi.shape[1] == o_ga + n_gdn_heads
    d = wi.shape[0]
    n_small = IDX_DIM + N_IDX_HEADS + 2 * n_gdn_heads
    ws = jnp.concatenate([wi[:, o_ik:o_qkv], wi[:, o_gb:], jnp.zeros((d, LANES - n_small), wi.dtype)], axis=1)
    pad_lanes = lambda v, at: jnp.zeros((1, LANES), F32).at[0, at:at + v.shape[0]].set(v.astype(F32))
    smul = jnp.ones((1, LANES), F32).at[0, SM_IW:SM_IW + N_IDX_HEADS].set(N_IDX_HEADS ** -0.5 * IDX_DIM ** -0.5)
    g_lane0 = SM_GB + n_gdn_heads
    vec = lambda v: v[l].astype(F32)[None, :]
    return {
        "wa": wi[:, :o_ik].astype(BF16),
        "wqkv": wi[:, o_qkv:o_z].astype(BF16),
        "wz": wi[:, o_z:o_gb].astype(BF16),
        "ws": ws.astype(BF16),
        "lng": pad_lanes(idx_k_norm_g[l], 0),
        "lnb": pad_lanes(idx_k_norm_b[l], 0),
        "smul": smul,
        "conv_w": jnp.zeros((SUBLANES, conv_ch), F32).at[:CONV_W].set(conv_w[l].astype(F32)),
        "alog": pad_lanes(gdn_a_log[l], g_lane0),
        "dtb": pad_lanes(gdn_dt_bias[l], g_lane0),
        "gdn_norm_g": vec(gdn_norm_g),
        "woa": w_out[l, :att_w].astype(BF16),
        "wob": w_out[l, att_w:].astype(BF16),
        "pre_mix_g": vec(pre_mix_g), "post_mix_g": vec(post_mix_g),
        "pre_mlp_g": vec(pre_mlp_g), "post_mlp_g": vec(post_mlp_g),
        "wup": w_mlp_up[l].astype(BF16),
        "wdn": w_mlp_down[l].astype(BF16),
    }


def _pick_tile(n, prefs):
    for t in prefs:
        if n % t == 0:
            return t
    return n


def kernel(x_prompt, x_sample, cache_k, cache_v, cache_idx_k, page_table, state_gdn, state_conv, w_in, conv_w, idx_k_norm_g, idx_k_norm_b, gdn_a_log, gdn_dt_bias, gdn_norm_g, w_out, pre_mix_g, post_mix_g, pre_mlp_g, post_mlp_g, w_mlp_up, w_mlp_down):
    b, s, d = x_prompt.shape
    bd, t, _ = x_sample.shape
    depth, n_pool, page, n_att_heads, head_dim = cache_k.shape
    n_pages = page_table.shape[1]
    past = n_pages * page
    n_gdn_heads, dk, dv = state_gdn.shape[2:]
    conv_ch = state_conv.shape[-1]
    att_w = n_att_heads * head_dim
    gdn_vw = n_gdn_heads * dv
    assert head_dim == HEAD_DIM and dk == GDN_DK and dv == GDN_DV and cache_idx_k.shape[-1] == IDX_DIM
    assert conv_ch == 2 * n_gdn_heads * dk + gdn_vw and state_conv.shape[2] == CONV_W - 1
    assert t <= SUBLANES and page == LANES

    tab_p = _rope_tables(jnp.arange(s))
    tab_s = tuple(jnp.tile(tb, (bd, 1)) for tb in _rope_tables(past + jnp.arange(t)))
    topk_p = min(TOPK_MAX, s // 4)
    topk_s = min(TOPK_MAX, (past + t) // 4)
    hist = CONV_W - 1

    tm_p = _pick_tile(b * s, (256, 128, 64, 32, 16, 8))
    tm_s = _pick_tile(bd * t, (256, 128, 64, 32, 16, 8))
    tq = _pick_tile(s, (256, 128))
    tk = _pick_tile(s, (512, 256, 128))
    tt_p = _pick_tile(s, (256, 128, 64))
    chunk_p = min(GDN_CHUNK, tt_p)
    chunk_s = 16
    sb_sel = _pick_tile(bd, (8, 4, 2))
    pg_sel = _pick_tile(n_pages, (8, 4, 2, 1))
    pg_att = _pick_tile(n_pages, (16, 8, 4, 2, 1))

    yp = x_prompt.reshape(b * s, d)
    ys = x_sample.reshape(bd * t, d)
    outs_p, outs_s = [], []
    for l in range(depth):
        lw = _layer_weights(l, att_w, conv_ch, gdn_vw, n_gdn_heads, w_in, conv_w, idx_k_norm_g, idx_k_norm_b,
                            gdn_a_log, gdn_dt_bias, gdn_norm_g, w_out, pre_mix_g, post_mix_g, pre_mlp_g,
                            post_mlp_g, w_mlp_up, w_mlp_down)
        (q, kf, vf, kb, vb, vt, iq, ikf, ikb, qkv, z, sm) = _inproj(
            yp, tab_p, lw, att_w=att_w, conv_ch=conv_ch, gdn_vw=gdn_vw, tm=tm_p)
        r3 = lambda a: a.reshape(b, s, a.shape[-1])
        att = _prompt_attention(r3(q), r3(iq), r3(sm), r3(kb), vt, r3(ikb), tq=tq, tk=tk, topk=topk_p)
        o, s_fin, conv_out = _gdn(
            r3(qkv), r3(z), r3(sm), jnp.zeros((b, SUBLANES, conv_ch), F32),
            jnp.zeros((b, n_gdn_heads, dk, dv), F32), lw, tt=tt_p, chunk=chunk_p, t_valid=s)
        yp = _outmlp(yp, att.reshape(b * s, att_w), o.reshape(b * s, gdn_vw), lw, tm=tm_p, ff_chunk=1024)
        outs_p.append((kf.reshape(b, s, n_att_heads, head_dim), vf.reshape(b, s, n_att_heads, head_dim),
                       ikf.reshape(b, s, IDX_DIM), s_fin, conv_out[:, SUBLANES - hist:, :]))

        (q, kf, vf, kb, vb, vt, iq, ikf, ikb, qkv, z, sm) = _inproj(
            ys, tab_s, lw, att_w=att_w, conv_ch=conv_ch, gdn_vw=gdn_vw, tm=tm_s)
        iq_s = iq.reshape(bd, t * N_IDX_HEADS, IDX_DIM)
        w_s = sm[:, SM_IW:SM_IW + N_IDX_HEADS].reshape(bd, t * N_IDX_HEADS, 1)
        pad_new = lambda a: jnp.pad(a.reshape(bd, t, a.shape[-1]), ((0, 0), (0, LANES - t), (0, 0)))
        sel = _decode_select(page_table, iq_s, w_s, pad_new(ikb), jnp.swapaxes(cache_idx_k[l], 1, 2),
                             sb=sb_sel, pg=pg_sel, n_tok=t, topk=topk_s)
        sel8 = jnp.pad(sel.reshape(bd, t, sel.shape[-1]), ((0, 0), (0, SUBLANES - t), (0, 0)))
        q8 = jnp.pad(q.reshape(bd, t, n_att_heads, head_dim).transpose(0, 2, 1, 3),
                     ((0, 0), (0, 0), (0, SUBLANES - t), (0, 0))).reshape(bd, n_att_heads * SUBLANES, head_dim)
        att_s = _decode_attention(page_table, q8, sel8, pad_new(kb), pad_new(vb),
                                  cache_k[l].reshape(n_pool, page * n_att_heads, head_dim),
                                  cache_v[l].reshape(n_pool, page * n_att_heads, head_dim),
                                  pg=pg_att, n_tok=t, n_heads=n_att_heads)
        pad_t = lambda a: jnp.pad(a.reshape(bd, t, a.shape[-1]), ((0, 0), (0, chunk_s - t), (0, 0)))
        conv0 = jnp.pad(state_conv[l].astype(F32), ((0, 0), (SUBLANES - hist, 0), (0, 0)))
        o_s, s_fin_s, conv_out_s = _gdn(pad_t(qkv), pad_t(z), pad_t(sm), conv0, state_gdn[l].astype(F32), lw,
                                        tt=chunk_s, chunk=chunk_s, t_valid=t)
        ys = _outmlp(ys, att_s.reshape(bd * t, att_w), o_s[:, :t].reshape(bd * t, gdn_vw), lw,
                     tm=tm_s, ff_chunk=1024)
        outs_s.append((kf.reshape(bd, t, n_att_heads, head_dim), vf.reshape(bd, t, n_att_heads, head_dim),
                       ikf.reshape(bd, t, IDX_DIM), s_fin_s, conv_out_s[:, SUBLANES - hist:, :]))

    kp, vp, ikp, gp, cp = [jnp.stack([o_[i] for o_ in outs_p]) for i in range(5)]
    ks_, vs_, iks, gs, cs = [jnp.stack([o_[i] for o_ in outs_s]) for i in range(5)]
    return (yp.reshape(b, s, d), ys.reshape(bd, t, d), kp, vp, ikp, gp, cp, ks_, vs_, iks, gs, cs)
```

```python
import functools
import math

import jax
import jax.numpy as jnp
from jax import lax
from jax.experimental import pallas as pl
from jax.experimental.pallas import tpu as pltpu

F32 = jnp.float32
BF16 = jnp.bfloat16

HEAD_DIM = 128
IDX_DIM = 64
N_IDX_HEADS = 8
GDN_DK = 128
GDN_DV = 128
CONV_W = 4
TOPK_MAX = 256
GDN_CHUNK = 64
ROPE_THETA = 10000.0
NORM_EPS = 1e-6

LANES = 128
SUBLANES = 8
VMEM_LIMIT = 56 * 1024 * 1024
NEG_BIG = -1e30
BF16_EXACT_INT = 256
VALUE_STEPS = 28
MIN_NORMAL = 2.0 ** -126
SCORE_ROWS = 256
SM_IW = IDX_DIM
SM_GB = IDX_DIM + N_IDX_HEADS
HIGHEST = lax.Precision.HIGHEST


def _dot(a, b):
    return jnp.dot(a, b, preferred_element_type=F32)


def _dot_nt(a, b):
    return lax.dot_general(a, b, (((1,), (1,)), ((), ())), preferred_element_type=F32)


def _dot_tn(a, b):
    return lax.dot_general(a, b, (((0,), (0,)), ((), ())), preferred_element_type=F32)


def _rms(x, g):
    return x * lax.rsqrt(jnp.mean(x * x, axis=-1, keepdims=True) + NORM_EPS) * g


def _const_spec(shape):
    n = len(shape)
    return pl.BlockSpec(shape, lambda *_: (0,) * n, pipeline_mode=pl.Buffered(1))


def _lane_fold(x, op):
    r = x[:, 0:LANES]
    for j in range(1, x.shape[1] // LANES):
        r = op(r, x[:, j * LANES:(j + 1) * LANES])
    return r


def _inproj_kernel(x_ref, g_ref, wa_ref, wqkv_ref, wz_ref, ws_ref,
                   cos_ref, sin_ref, cosi_ref, sina_ref, sinb_ref, lng_ref, lnb_ref, smul_ref,
                   q_ref, kf_ref, vf_ref, kb_ref, vb_ref, vt_ref, iq_ref, ikf_ref, ikb_ref,
                   qkv_ref, z_ref, sm_ref, *, att_w, q_scale):
    x = x_ref[...]
    h = _rms(x, g_ref[...]).astype(BF16)
    a = _dot(h, wa_ref[...])
    cos = cos_ref[...]
    sin = sin_ref[...]
    n_heads = att_w // HEAD_DIM

    def rope_full(xh):
        return xh * cos + pltpu.roll(xh, HEAD_DIM // 2, axis=1) * sin

    for j in range(n_heads):
        sl = slice(j * HEAD_DIM, (j + 1) * HEAD_DIM)
        q_ref[:, sl] = (rope_full(a[:, sl]) * q_scale).astype(BF16)
        kr = rope_full(a[:, att_w + j * HEAD_DIM: att_w + (j + 1) * HEAD_DIM])
        kf_ref[pl.ds(j, x.shape[0], stride=n_heads), :] = kr
        kb_ref[:, sl] = kr.astype(BF16)
    v = a[:, 2 * att_w:3 * att_w]
    for j in range(n_heads):
        vf_ref[pl.ds(j, x.shape[0], stride=n_heads), :] = v[:, j * HEAD_DIM:(j + 1) * HEAD_DIM]
    vb_ref[...] = v.astype(BF16)
    vt_ref[0] = jnp.transpose(v).astype(BF16)

    cosi = cosi_ref[...]
    sina = sina_ref[...]
    sinb = sinb_ref[...]

    def rope_half(xh):
        return (xh * cosi + pltpu.roll(xh, LANES - IDX_DIM // 2, axis=1) * sina
                + pltpu.roll(xh, IDX_DIM // 2, axis=1) * sinb)

    idx_w = N_IDX_HEADS * IDX_DIM
    for j in range(idx_w // LANES):
        sl = slice(j * LANES, (j + 1) * LANES)
        iq_ref[:, sl] = rope_half(a[:, 3 * att_w + j * LANES: 3 * att_w + (j + 1) * LANES]).astype(BF16)

    qkv_ref[...] = _dot(h, wqkv_ref[...])
    z_ref[...] = _dot(h, wz_ref[...])

    sm = _dot(h, ws_ref[...])
    lane = lax.broadcasted_iota(jnp.int32, sm.shape, 1)
    is_ik = lane < IDX_DIM
    ikraw = jnp.where(is_ik, sm, 0.0)
    mu = jnp.sum(ikraw, axis=-1, keepdims=True) * (1.0 / IDX_DIM)
    xc = jnp.where(is_ik, sm - mu, 0.0)
    var = jnp.sum(xc * xc, axis=-1, keepdims=True) * (1.0 / IDX_DIM)
    ikn = xc * lax.rsqrt(var + NORM_EPS) * lng_ref[...] + lnb_ref[...]
    ikr = rope_half(ikn)
    ikf_ref[...] = ikr[:, :IDX_DIM]
    ikb_ref[...] = ikr[:, :IDX_DIM].astype(BF16)
    sm_ref[...] = sm * smul_ref[...]


def _rope_tables(pos):
    pos = pos.astype(F32)[:, None]
    half = HEAD_DIM // 2
    inv = jnp.power(ROPE_THETA, -jnp.arange(half, dtype=F32) / half)
    ang = pos * inv[None, :]
    c, s = jnp.cos(ang), jnp.sin(ang)
    cos = jnp.concatenate([c, c], axis=-1)
    sin = jnp.concatenate([-s, s], axis=-1)
    halfi = IDX_DIM // 2
    invi = jnp.power(ROPE_THETA, -jnp.arange(halfi, dtype=F32) / halfi)
    angi = pos * invi[None, :]
    ci, si = jnp.cos(angi), jnp.sin(angi)
    zi = jnp.zeros_like(si)
    cosi = jnp.concatenate([ci, ci, ci, ci], axis=-1)
    sina = jnp.concatenate([-si, zi, -si, zi], axis=-1)
    sinb = jnp.concatenate([zi, si, zi, si], axis=-1)
    return cos, sin, cosi, sina, sinb


def _inproj(x2d, tables, lw, *, att_w, conv_ch, gdn_vw, tm):
    n, d = x2d.shape
    idx_w = N_IDX_HEADS * IDX_DIM
    rt = tables[0].shape[0]
    assert n % tm == 0 and rt % tm == 0 and n % rt == 0
    nt = rt // tm
    n_heads = att_w // HEAD_DIM
    wa_w = 3 * att_w + idx_w
    row = lambda w: pl.BlockSpec((tm, w), lambda i: (i, 0))
    tab = pl.BlockSpec((tm, LANES), lambda i: (i % nt, 0))
    out_shapes = (
        jax.ShapeDtypeStruct((n, att_w), BF16),
        jax.ShapeDtypeStruct((n * n_heads, HEAD_DIM), F32),
        jax.ShapeDtypeStruct((n * n_heads, HEAD_DIM), F32),
        jax.ShapeDtypeStruct((n, att_w), BF16),
        jax.ShapeDtypeStruct((n, att_w), BF16),
        jax.ShapeDtypeStruct((n // rt, att_w, rt), BF16),
        jax.ShapeDtypeStruct((n, idx_w), BF16),
        jax.ShapeDtypeStruct((n, IDX_DIM), F32),
        jax.ShapeDtypeStruct((n, IDX_DIM), BF16),
        jax.ShapeDtypeStruct((n, conv_ch), F32),
        jax.ShapeDtypeStruct((n, gdn_vw), F32),
        jax.ShapeDtypeStruct((n, LANES), F32),
    )
    vt_spec = pl.BlockSpec((1, att_w, tm), lambda i: (i // nt, 0, i % nt))
    head_rows = pl.BlockSpec((tm * n_heads, HEAD_DIM), lambda i: (i, 0))
    out_specs = (row(att_w), head_rows, head_rows, row(att_w), row(att_w), vt_spec, row(idx_w),
                 row(IDX_DIM), row(IDX_DIM), row(conv_ch), row(gdn_vw), row(LANES))
    in_specs = [row(d), _const_spec((1, d)), _const_spec((d, wa_w)), _const_spec((d, conv_ch)),
                _const_spec((d, gdn_vw)), _const_spec((d, LANES)),
                tab, tab, tab, tab, tab,
                _const_spec((1, LANES)), _const_spec((1, LANES)), _const_spec((1, LANES))]
    return pl.pallas_call(
        functools.partial(_inproj_kernel, att_w=att_w, q_scale=HEAD_DIM ** -0.5),
        grid=(n // tm,),
        in_specs=in_specs, out_specs=out_specs, out_shape=out_shapes,
        compiler_params=pltpu.CompilerParams(dimension_semantics=("arbitrary",),
                                             vmem_limit_bytes=VMEM_LIMIT),
        name="inproj",
    )(x2d, lw["pre_mix_g"], lw["wa"], lw["wqkv"], lw["wz"], lw["ws"], *tables,
      lw["lng"], lw["lnb"], lw["smul"])


def _select_threshold(count, rewrite, n_valid, row_max, row_min, topk, col_limit):
    kf = float(topk)
    flip = jnp.int32(0x7FFFFFFF)

    def to_key(x):
        b = lax.bitcast_convert_type(jnp.where(x == 0.0, 0.0, x), jnp.int32)
        return jnp.where(b < 0, b ^ flip, b)

    def from_key(k):
        min_normal = jnp.int32(0x00800000)
        k = jnp.where(jnp.logical_and(k > 0, k < min_normal), min_normal, k)
        return lax.bitcast_convert_type(jnp.where(k < 0, k ^ flip, k), F32)

    need = n_valid > kf
    spread = jnp.abs(row_max) * (2.0 ** -20) + 2.0 ** -100
    c_ge0 = count(lambda x, col: x >= 0.0)
    c_gt0 = count(lambda x, col: x > 0.0)
    non_neg = jnp.logical_and(need, c_ge0 >= kf)
    at_zero = jnp.logical_and(non_neg, c_gt0 < kf)
    negative = jnp.logical_and(need, c_ge0 < kf)
    lo0 = jnp.where(non_neg, 0.0, jnp.where(need, row_min, NEG_BIG))
    c_lo0 = jnp.where(non_neg, c_ge0, n_valid)
    hi0 = jnp.where(at_zero, MIN_NORMAL, jnp.where(negative, 0.0, row_max + spread))
    c_hi0 = jnp.where(at_zero, c_gt0, jnp.where(negative, c_ge0, 0.0))
    open0 = jnp.logical_and(need, jnp.logical_not(jnp.logical_or(at_zero, c_lo0 == kf)))
    done0 = jnp.where(open0, 0.0, 1.0)

    def narrow(c, mid, stuck, cnt):
        lo, hi, c_lo, c_hi, done = c
        ge = cnt >= kf
        upd = jnp.logical_and(done < 0.5, jnp.logical_not(stuck))
        up_lo = jnp.logical_and(upd, ge)
        up_hi = jnp.logical_and(upd, jnp.logical_not(ge))
        fin = jnp.logical_or(stuck, jnp.logical_and(upd, cnt == kf))
        return (jnp.where(up_lo, mid, lo), jnp.where(up_hi, mid, hi), jnp.where(up_lo, cnt, c_lo),
                jnp.where(up_hi, cnt, c_hi), jnp.where(fin, 1.0, done))

    def value_step(c):
        it, st = c
        lo, hi = st[0], st[1]
        mid = 0.5 * lo + 0.5 * hi
        stuck = jnp.logical_or(mid <= lo, mid >= hi)
        return it + 1, narrow(st, mid, stuck, count(lambda x, col: x >= mid))

    _, (lo, hi, c_lo, c_hi, done) = lax.while_loop(
        lambda c: jnp.logical_and(c[0] < VALUE_STEPS, jnp.min(c[1][4]) < 0.5), value_step,
        (jnp.int32(0), (lo0, hi0, c_lo0, c_hi0, done0)))

    def key_step(st):
        klo, khi = st[0], st[1]
        mid = (klo & khi) + ((klo ^ khi) >> 1)
        return narrow(st, mid, mid <= klo, count(lambda x, col: to_key(x) >= mid))

    klo, khi, c_lo, c_hi, _ = lax.while_loop(
        lambda st: jnp.min(st[4]) < 0.5, key_step, (to_key(lo), to_key(hi), c_lo, c_hi, done))
    lo, hi = from_key(klo), from_key(khi)

    tied = jnp.logical_and(need, c_lo > kf)

    @pl.when(jnp.max(jnp.where(tied, 1.0, 0.0)) > 0.5)
    def _():
        want = kf - c_hi

        def in_tie(x):
            return jnp.logical_and(x >= lo, x < hi)

        def jstep(_, c):
            jlo, jhi = c
            jm = jnp.floor((jlo + jhi) * 0.5)
            cnt = count(lambda x, col: jnp.logical_and(in_tie(x), col <= jm))
            ok = cnt >= want
            return jnp.where(ok, jlo, jm), jnp.where(ok, jm, jhi)

        n_it = max(1, math.ceil(math.log2(col_limit + 1)))
        j0 = (jnp.full_like(lo, -1.0), jnp.full_like(lo, float(col_limit - 1)))
        _, jcut = lax.fori_loop(0, n_it, jstep, j0)
        rewrite(lambda x, col: jnp.where(
            jnp.logical_and(tied, jnp.logical_and(in_tie(x), col > jcut)), -jnp.inf, x))

    return lo


def _pattn_kernel(q_ref, iq_ref, sm_ref, k_ref, vt_ref, ik_ref, o_ref, sc_ref, m_ref, l_ref, acc_ref,
                  *, tq, tk, topk, seq, n_heads):
    qi = pl.program_id(1)
    t0 = qi * tq
    nkb = (t0 + tq + tk - 1) // tk
    sub_tiles = tk // SUBLANES
    q_t = t0 + lax.broadcasted_iota(jnp.int32, (1, tq), 1)
    smt = jnp.transpose(sm_ref[0])
    w_rows = [smt[SM_IW + h:SM_IW + h + 1, :] for h in range(N_IDX_HEADS)]
    iq = iq_ref[0]
    iq_heads = [iq[:, h * IDX_DIM:(h + 1) * IDX_DIM] for h in range(N_IDX_HEADS)]
    sub_iota = lax.broadcasted_iota(jnp.int32, (SUBLANES, tq), 0)

    def sub_fold(x, op):
        r = x[0:SUBLANES]
        for j in range(1, x.shape[0] // SUBLANES):
            r = op(r, x[j * SUBLANES:(j + 1) * SUBLANES])
        return r

    def rep(row):
        return jnp.broadcast_to(row, (SUBLANES, tq))

    def score_block(kb, carry):
        mx, mn = carry
        k0 = pl.multiple_of(kb * tk, tk)
        for c in range(tk // SCORE_ROWS):
            r0 = k0 + c * SCORE_ROWS
            ikc = ik_ref[0, pl.ds(r0, SCORE_ROWS), :]
            acc = jnp.zeros((SCORE_ROWS, tq), F32)
            for h in range(N_IDX_HEADS):
                acc = acc + w_rows[h] * jnp.maximum(_dot_nt(ikc, iq_heads[h]), 0.0)
            key = r0 + lax.broadcasted_iota(jnp.int32, (SCORE_ROWS, 1), 0)
            valid = key <= q_t
            sc_ref[pl.ds(r0, SCORE_ROWS), :] = jnp.where(valid, acc, -jnp.inf)
            mx = jnp.maximum(mx, sub_fold(jnp.where(valid, acc, -jnp.inf), jnp.maximum))
            mn = jnp.minimum(mn, sub_fold(jnp.where(valid, acc, jnp.inf), jnp.minimum))
        return mx, mn

    mx, mn = lax.fori_loop(0, nkb, score_block,
                           (jnp.full((SUBLANES, tq), -jnp.inf, F32), jnp.full((SUBLANES, tq), jnp.inf, F32)))
    row_max = rep(jnp.max(mx, axis=0, keepdims=True))
    row_min = rep(jnp.min(mn, axis=0, keepdims=True))
    n_valid = rep((q_t + 1).astype(F32))

    def count(pred):
        def body(kb, acc):
            k0 = pl.multiple_of(kb * tk, tk)
            x = sc_ref[pl.ds(k0, tk), :]
            for j in range(sub_tiles):
                hit = pred(x[j * SUBLANES:(j + 1) * SUBLANES], (k0 + j * SUBLANES + sub_iota).astype(F32))
                acc = acc + jnp.where(hit, 1.0, 0.0)
            return acc
        acc = lax.fori_loop(0, nkb, body, jnp.zeros((SUBLANES, tq), F32))
        return rep(jnp.sum(acc, axis=0, keepdims=True))

    def rewrite(fn):
        def body(kb, carry):
            k0 = pl.multiple_of(kb * tk, tk)
            x = sc_ref[pl.ds(k0, tk), :]
            sc_ref[pl.ds(k0, tk), :] = jnp.concatenate(
                [fn(x[j * SUBLANES:(j + 1) * SUBLANES], (k0 + j * SUBLANES + sub_iota).astype(F32))
                 for j in range(sub_tiles)], axis=0)
            return carry
        lax.fori_loop(0, nkb, body, 0)

    lo = _select_threshold(count, rewrite, n_valid, row_max, row_min, topk, seq)
    tile_rows = lambda v8: jnp.concatenate([v8] * sub_tiles, axis=0)
    lo_t = tile_rows(lo)

    def masked_scores(kb, h):
        k0 = pl.multiple_of(kb * tk, tk)
        sl = slice(h * HEAD_DIM, (h + 1) * HEAD_DIM)
        s = _dot_nt(k_ref[0, pl.ds(k0, tk), sl], q_ref[0, :, sl])
        return jnp.where(sc_ref[pl.ds(k0, tk), :] >= lo_t, s, NEG_BIG)

    m_ref[...] = jnp.full(m_ref.shape, NEG_BIG, F32)
    l_ref[...] = jnp.zeros_like(l_ref)
    acc_ref[...] = jnp.zeros_like(acc_ref)

    def pv_block(kb, carry):
        k0 = pl.multiple_of(kb * tk, tk)
        ss = [masked_scores(kb, h) for h in range(n_heads)]
        hss = [slice(h * SUBLANES, (h + 1) * SUBLANES) for h in range(n_heads)]
        m_old = [m_ref[hs, :] for hs in hss]
        m_new = [jnp.maximum(m_old[h], rep(jnp.max(sub_fold(ss[h], jnp.maximum), axis=0, keepdims=True)))
                 for h in range(n_heads)]
        ps = [jnp.exp(ss[h] - tile_rows(m_new[h])) for h in range(n_heads)]
        for h in range(n_heads):
            sl = slice(h * HEAD_DIM, (h + 1) * HEAD_DIM)
            alpha = jnp.exp(m_old[h] - m_new[h])
            m_ref[hss[h], :] = m_new[h]
            l_ref[hss[h], :] = alpha * l_ref[hss[h], :] + sub_fold(ps[h], jnp.add)
            acc_ref[sl, :] = (alpha[0:1, :] * acc_ref[sl, :]
                              + _dot(vt_ref[0, sl, pl.ds(k0, tk)], ps[h].astype(BF16)))
        return carry
    lax.fori_loop(0, nkb, pv_block, 0)

    for h in range(n_heads):
        sl = slice(h * HEAD_DIM, (h + 1) * HEAD_DIM)
        l_row = jnp.sum(l_ref[h * SUBLANES:(h + 1) * SUBLANES, :], axis=0, keepdims=True)
        o_ref[0, :, sl] = jnp.transpose(acc_ref[sl, :] / l_row).astype(o_ref.dtype)


def _prompt_attention(q, iq, sm, kb, vt, ikb, *, tq, tk, topk):
    b, s, att_w = q.shape
    idx_w = iq.shape[-1]
    n_heads = att_w // HEAD_DIM
    assert s % tq == 0 and s % tk == 0 and tk % tq == 0 and tq % LANES == 0
    blk = lambda w: pl.BlockSpec((1, tq, w), lambda bi, qi: (bi, qi, 0))
    full = lambda r, w: pl.BlockSpec((1, r, w), lambda bi, qi: (bi, 0, 0), pipeline_mode=pl.Buffered(1))
    return pl.pallas_call(
        functools.partial(_pattn_kernel, tq=tq, tk=tk, topk=topk, seq=s, n_heads=n_heads),
        grid=(b, s // tq),
        in_specs=[blk(att_w), blk(idx_w), blk(LANES), full(s, att_w), full(att_w, s), full(s, IDX_DIM)],
        out_specs=blk(att_w),
        out_shape=jax.ShapeDtypeStruct((b, s, att_w), BF16),
        scratch_shapes=[pltpu.VMEM((s, tq), F32), pltpu.VMEM((n_heads * SUBLANES, tq), F32),
                        pltpu.VMEM((n_heads * SUBLANES, tq), F32), pltpu.VMEM((att_w, tq), F32)],
        compiler_params=pltpu.CompilerParams(dimension_semantics=("arbitrary", "arbitrary"),
                                             vmem_limit_bytes=VMEM_LIMIT),
        name="prompt_attention",
    )(q, iq, sm, kb, vt, ikb)


def _dsel_kernel(pt_ref, iq_ref, w_ref, iknew_ref, *rest, sb, pg, n_tok, past, topk):
    pages = rest[:sb * pg]
    sel_ref = rest[sb * pg]
    sc_ref = rest[sb * pg + 1]
    c = pl.program_id(1)
    n_c = pl.num_programs(1)
    page = pages[0].shape[2]
    rows, l_pad = sc_ref.shape

    def token_rows(s):
        return [jnp.sum(s[t * N_IDX_HEADS:(t + 1) * N_IDX_HEADS, :], axis=0, keepdims=True)
                for t in range(n_tok)]

    @pl.when(c == 0)
    def _():
        sc_ref[...] = jnp.full(sc_ref.shape, -jnp.inf, F32)

    for j in range(sb):
        iq = iq_ref[j]
        w = w_ref[j]
        kt = jnp.concatenate([pages[j * pg + p][0] for p in range(pg)], axis=1).astype(BF16)
        tr = token_rows(jnp.maximum(_dot(iq, kt), 0.0) * w)
        k0 = pl.multiple_of(c * pg * page, pg * page)
        for t in range(n_tok):
            sc_ref[j * n_tok + t:j * n_tok + t + 1, pl.ds(k0, pg * page)] = tr[t]

    @pl.when(c == n_c - 1)
    def _():
        col = lax.broadcasted_iota(jnp.int32, (1, LANES), 1)
        for j in range(sb):
            tr = token_rows(jnp.maximum(_dot_nt(iq_ref[j], iknew_ref[j]), 0.0) * w_ref[j])
            for t in range(n_tok):
                sc_ref[j * n_tok + t:j * n_tok + t + 1, past:past + LANES] = jnp.where(col <= t, tr[t], -jnp.inf)
        x = sc_ref[...]
        rep = lambda v: jnp.broadcast_to(v, (rows, LANES))
        row_max = rep(jnp.max(x, axis=-1, keepdims=True))
        row_min = rep(jnp.min(jnp.where(x > -jnp.inf, x, jnp.inf), axis=-1, keepdims=True))
        r = lax.broadcasted_iota(jnp.int32, (rows, LANES), 0).astype(F32)
        tok = r - n_tok * jnp.floor(r * (1.0 / n_tok))
        n_valid = past + 1.0 + tok
        n_blocks = l_pad // LANES
        assert n_blocks <= BF16_EXACT_INT
        ones_b = jnp.ones((LANES, LANES), BF16)
        lane_iota = lax.broadcasted_iota(jnp.int32, (rows, LANES), 1)

        def count(pred):
            def body(kb, acc):
                c0 = pl.multiple_of(kb * LANES, LANES)
                hit = pred(sc_ref[:, pl.ds(c0, LANES)], (c0 + lane_iota).astype(F32))
                return acc + jnp.where(hit, 1.0, 0.0)
            acc = lax.fori_loop(0, n_blocks, body, jnp.zeros((rows, LANES), F32))
            return _dot(acc.astype(BF16), ones_b)

        def rewrite(fn):
            def body(kb, carry):
                c0 = pl.multiple_of(kb * LANES, LANES)
                sc_ref[:, pl.ds(c0, LANES)] = fn(sc_ref[:, pl.ds(c0, LANES)], (c0 + lane_iota).astype(F32))
                return carry
            lax.fori_loop(0, n_blocks, body, 0)

        lo = _select_threshold(count, rewrite, n_valid, row_max, row_min, topk, l_pad)

        def emit(kb, carry):
            c0 = pl.multiple_of(kb * LANES, LANES)
            x = jnp.where(sc_ref[:, pl.ds(c0, LANES)] >= lo, 1.0, 0.0)
            for j in range(sb):
                sel_ref[j, 0:n_tok, pl.ds(c0, LANES)] = x[j * n_tok:(j + 1) * n_tok]
                sel_ref[j, n_tok:SUBLANES, pl.ds(c0, LANES)] = jnp.zeros((SUBLANES - n_tok, LANES), F32)
            return carry
        lax.fori_loop(0, l_pad // LANES, emit, 0)


def _decode_select(page_table, iq, w, iknew, cache_idx_t, *, sb, pg, n_tok, topk):
    bd, n_pages = page_table.shape
    _, di, page = cache_idx_t.shape
    past = n_pages * page
    assert n_pages % pg == 0 and bd % sb == 0 and (sb * n_tok) % SUBLANES == 0
    l_pad = past + LANES
    rows = iq.shape[1]

    def page_spec(j, p):
        return pl.BlockSpec((1, di, page), lambda g, c, pt: (pt[g * sb + j, c * pg + p], 0, 0))

    grid_spec = pltpu.PrefetchScalarGridSpec(
        num_scalar_prefetch=1,
        grid=(bd // sb, n_pages // pg),
        in_specs=[pl.BlockSpec((sb, rows, di), lambda g, c, pt: (g, 0, 0)),
                  pl.BlockSpec((sb, rows, 1), lambda g, c, pt: (g, 0, 0)),
                  pl.BlockSpec((sb, LANES, di), lambda g, c, pt: (g, 0, 0))]
                 + [page_spec(j, p) for j in range(sb) for p in range(pg)],
        out_specs=pl.BlockSpec((sb, SUBLANES, l_pad), lambda g, c, pt: (g, 0, 0)),
        scratch_shapes=[pltpu.VMEM((sb * n_tok, l_pad), F32)],
    )
    return pl.pallas_call(
        functools.partial(_dsel_kernel, sb=sb, pg=pg, n_tok=n_tok, past=past, topk=topk),
        grid_spec=grid_spec,
        out_shape=jax.ShapeDtypeStruct((bd, SUBLANES, l_pad), F32),
        compiler_params=pltpu.CompilerParams(dimension_semantics=("arbitrary", "arbitrary"),
                                             vmem_limit_bytes=VMEM_LIMIT),
        name="decode_select",
    )(page_table, iq, w, iknew, *([cache_idx_t] * (sb * pg)))


def _dattn_kernel(pt_ref, q_ref, sel_ref, knew_ref, vnew_ref, *rest, pg, n_tok, n_heads, past):
    kpages = rest[:pg]
    vpages = rest[pg:2 * pg]
    o_ref = rest[2 * pg]
    m_ref, l_ref, acc_ref = rest[2 * pg + 1:]
    c = pl.program_id(1)
    n_c = pl.num_programs(1)
    page = kpages[0].shape[1] // n_heads
    qh = [q_ref[0, h * SUBLANES:(h + 1) * SUBLANES, :] for h in range(n_heads)]

    @pl.when(c == 0)
    def _():
        m_ref[...] = jnp.full(m_ref.shape, NEG_BIG, F32)
        l_ref[...] = jnp.zeros_like(l_ref)
        acc_ref[...] = jnp.zeros_like(acc_ref)

    def update(k_of, v_of, n_blk, sel):
        s = jnp.concatenate(
            [jnp.concatenate([_dot_nt(qh[h], k_of(p, h)) for p in range(n_blk)], axis=1)
             for h in range(n_heads)], axis=0)
        s = jnp.where(jnp.concatenate([sel] * n_heads, axis=0) > 0.5, s, NEG_BIG)
        m_old = m_ref[...]
        m_new = jnp.maximum(m_old, jnp.max(s, axis=-1, keepdims=True))
        pr = jnp.exp(s - m_new).astype(BF16)
        alpha = jnp.exp(m_old - m_new)
        l_ref[...] = alpha * l_ref[...] + jnp.sum(pr.astype(F32), axis=-1, keepdims=True)
        pv = []
        for h in range(n_heads):
            a = jnp.zeros((SUBLANES, HEAD_DIM), F32)
            for p in range(n_blk):
                a = a + _dot(pr[h * SUBLANES:(h + 1) * SUBLANES, p * page:(p + 1) * page], v_of(p, h))
            pv.append(a)
        acc_ref[...] = alpha * acc_ref[...] + jnp.concatenate(pv, axis=0)
        m_ref[...] = m_new

    head_rows = lambda ref, h: ref[0, pl.ds(h, page, stride=n_heads), :].astype(BF16)
    k0 = pl.multiple_of(c * pg * page, pg * page)
    update(lambda p, h: head_rows(kpages[p], h), lambda p, h: head_rows(vpages[p], h), pg,
           sel_ref[0, :, pl.ds(k0, pg * page)])

    @pl.when(c == n_c - 1)
    def _():
        new = lambda ref, h: ref[0, :, h * HEAD_DIM:(h + 1) * HEAD_DIM]
        update(lambda p, h: new(knew_ref, h), lambda p, h: new(vnew_ref, h), 1,
               sel_ref[0, :, past:past + LANES])
        inv_l = 1.0 / l_ref[...]
        out = acc_ref[...] * inv_l
        for h in range(n_heads):
            o_ref[0, :, h * HEAD_DIM:(h + 1) * HEAD_DIM] = out[h * SUBLANES:h * SUBLANES + n_tok, :].astype(o_ref.dtype)


def _decode_attention(page_table, q8, sel, knew, vnew, cache_k, cache_v, *, pg, n_tok, n_heads):
    bd, n_pages = page_table.shape
    _, page_rows, hd = cache_k.shape
    page = page_rows // n_heads
    past = n_pages * page
    l_pad = sel.shape[-1]
    att_w = n_heads * hd
    rows = n_heads * SUBLANES
    assert n_pages % pg == 0 and page == LANES

    def page_spec(p):
        return pl.BlockSpec((1, page_rows, hd), lambda b, c, pt: (pt[b, c * pg + p], 0, 0))

    per_seq = lambda r, w: pl.BlockSpec((1, r, w), lambda b, c, pt: (b, 0, 0))
    grid_spec = pltpu.PrefetchScalarGridSpec(
        num_scalar_prefetch=1,
        grid=(bd, n_pages // pg),
        in_specs=[per_seq(rows, hd), per_seq(SUBLANES, l_pad), per_seq(LANES, att_w), per_seq(LANES, att_w)]
                 + [page_spec(p) for p in range(pg)] * 2,
        out_specs=per_seq(n_tok, att_w),
        scratch_shapes=[pltpu.VMEM((rows, 1), F32), pltpu.VMEM((rows, 1), F32), pltpu.VMEM((rows, hd), F32)],
    )
    return pl.pallas_call(
        functools.partial(_dattn_kernel, pg=pg, n_tok=n_tok, n_heads=n_heads, past=past),
        grid_spec=grid_spec,
        out_shape=jax.ShapeDtypeStruct((bd, n_tok, att_w), BF16),
        compiler_params=pltpu.CompilerParams(dimension_semantics=("arbitrary", "arbitrary"),
                                             vmem_limit_bytes=VMEM_LIMIT),
        name="decode_attention",
    )(page_table, q8, sel, knew, vnew, *([cache_k] * pg), *([cache_v] * pg))


def _gdn_kernel(x_ref, z_ref, sm_ref, conv0_ref, s0_ref, cw_ref, alog_ref, dtb_ref, ng_ref,
                o_ref, sfin_ref, convout_ref,
                xbuf, st_ref, q_s, k_s, v_s, beta_s, g_s,
                *, tt, chunk, n_heads, t_valid, n_steps):
    step = pl.program_id(1)
    hist = CONV_W - 1
    qk_w = n_heads * GDN_DK

    @pl.when(step == 0)
    def _():
        xbuf[0:SUBLANES, :] = conv0_ref[0]
        st_ref[...] = s0_ref[0]

    @pl.when(step > 0)
    def _():
        xbuf[0:SUBLANES, :] = xbuf[tt:tt + SUBLANES, :]

    xbuf[SUBLANES:SUBLANES + tt, :] = x_ref[0]
    cw = cw_ref[...]
    conv = cw[hist:hist + 1, :] * xbuf[SUBLANES:SUBLANES + tt, :]
    for j in range(hist):
        conv = conv + cw[j:j + 1, :] * xbuf[SUBLANES - hist + j:SUBLANES - hist + j + tt, :]
    act = conv * jax.nn.sigmoid(conv)

    def l2n(xh):
        return xh * lax.rsqrt(jnp.sum(xh * xh, axis=-1, keepdims=True) + NORM_EPS)

    for h in range(n_heads):
        sl = slice(h * GDN_DK, (h + 1) * GDN_DK)
        q_s[:, sl] = l2n(act[:, sl]) * (GDN_DK ** -0.5)
        k_s[:, sl] = l2n(act[:, qk_w + h * GDN_DK: qk_w + (h + 1) * GDN_DK])
    v_s[...] = act[:, 2 * qk_w:]

    smv = sm_ref[0]
    t_glob = step * tt + lax.broadcasted_iota(jnp.int32, (tt, 1), 0)
    live = t_glob < t_valid
    beta_s[...] = jnp.where(live, jax.nn.sigmoid(smv), 0.0)
    g_s[...] = jnp.where(live, -jnp.exp(alog_ref[...]) * jax.nn.softplus(smv + dtb_ref[...]), 0.0)

    ii = lax.broadcasted_iota(jnp.int32, (chunk, chunk), 0)
    jj = lax.broadcasted_iota(jnp.int32, (chunk, chunk), 1)
    tril = ii >= jj
    strict = ii > jj
    tril_f = jnp.where(tril, 1.0, 0.0)
    eye = jnp.where(ii == jj, 1.0, 0.0)
    lane8 = lax.broadcasted_iota(jnp.int32, (SUBLANES, LANES), 1)
    row8 = lax.broadcasted_iota(jnp.int32, (SUBLANES, LANES), 0)
    g_lane0 = SM_GB + n_heads
    pick_g = jnp.where(lane8 == g_lane0 + row8, 1.0, 0.0)
    n_levels = int(math.log2(chunk))
    ng = ng_ref[...]

    n_chunks = tt // chunk
    units = [(ci, h) for ci in range(n_chunks) for h in range(n_heads)]
    rows_of = lambda ci: slice(ci * chunk, (ci + 1) * chunk)
    lanes_of = lambda h: slice(h * GDN_DK, (h + 1) * GDN_DK)
    gcs = [jnp.dot(tril_f, g_s[rows_of(ci), :], precision=HIGHEST, preferred_element_type=F32)
           for ci in range(n_chunks)]
    gcrs = [lax.dot_general(pick_g, gc, (((1,), (1,)), ((), ())), precision=HIGHEST,
                            preferred_element_type=F32) for gc in gcs]
    gcol = {(ci, h): gcs[ci][:, g_lane0 + h:g_lane0 + h + 1] for ci, h in units}
    bcol = {(ci, h): beta_s[rows_of(ci), SM_GB + h:SM_GB + h + 1] for ci, h in units}
    decay = {u: jnp.exp(jnp.where(tril, gcol[u] - gcrs[u[0]][u[1]:u[1] + 1, :], -jnp.inf)) for u in units}
    k_b = {(ci, h): k_s[rows_of(ci), lanes_of(h)].astype(BF16) for ci, h in units}
    kb_f = {(ci, h): k_s[rows_of(ci), lanes_of(h)] * bcol[(ci, h)] for ci, h in units}
    lmat = {u: jnp.where(strict, _dot_nt(kb_f[u].astype(BF16), k_b[u]) * decay[u], 0.0) for u in units}
    qk = {(ci, h): (_dot_nt(q_s[rows_of(ci), lanes_of(h)].astype(BF16), k_b[(ci, h)]) * decay[(ci, h)]).astype(BF16)
          for ci, h in units}
    pw = {u: -lmat[u] for u in units}
    inv = {u: eye + pw[u] for u in units}
    for _ in range(n_levels - 1):
        pw = {u: _dot(pw[u].astype(BF16), pw[u].astype(BF16)) for u in units}
        inv = {u: inv[u] + _dot(inv[u].astype(BF16), pw[u].astype(BF16)) for u in units}
    inv_b = {u: inv[u].astype(BF16) for u in units}
    u_rhs = {(ci, h): _dot(inv_b[(ci, h)], (v_s[rows_of(ci), lanes_of(h)] * bcol[(ci, h)]).astype(BF16))
             for ci, h in units}
    w_rhs = {u: _dot(inv_b[u], (kb_f[u] * jnp.exp(gcol[u])).astype(BF16)).astype(BF16) for u in units}

    state = [st_ref[h] for h in range(n_heads)]
    for ci in range(n_chunks):
        heads = [(ci, h) for h in range(n_heads)]
        s_b = [s.astype(BF16) for s in state]
        ws = [_dot(w_rhs[u], s_b[u[1]]) for u in heads]
        qs = [_dot((q_s[rows_of(ci), lanes_of(h)] * jnp.exp(gcol[(ci, h)])).astype(BF16), s_b[h]) for _, h in heads]
        v_new = [(u_rhs[u] - ws[u[1]]).astype(BF16) for u in heads]
        outs = [qs[h] + _dot(qk[(ci, h)], v_new[h]) for _, h in heads]
        g_last = [gcs[ci][chunk - 1:chunk, g_lane0 + h:g_lane0 + h + 1] for _, h in heads]
        k_dec = [(k_s[rows_of(ci), lanes_of(h)] * jnp.exp(g_last[h] - gcol[(ci, h)])).astype(BF16) for _, h in heads]
        state = [state[h] * jnp.exp(g_last[h]) + _dot_tn(k_dec[h], v_new[h]) for _, h in heads]
        for _, h in heads:
            zh = z_ref[0, rows_of(ci), lanes_of(h)]
            o_ref[0, rows_of(ci), lanes_of(h)] = (_rms(outs[h], ng) * (zh * jax.nn.sigmoid(zh))).astype(o_ref.dtype)
    for h in range(n_heads):
        st_ref[h] = state[h]

    @pl.when(step == n_steps - 1)
    def _():
        sfin_ref[0] = st_ref[...]
        last = t_valid - (n_steps - 1) * tt
        convout_ref[0] = xbuf[last:last + SUBLANES, :]


def _gdn(x, z, sm, conv0, s0, lw, *, tt, chunk, t_valid):
    nb, t_pad, conv_ch = x.shape
    n_heads = s0.shape[1]
    vw = z.shape[-1]
    assert t_pad % tt == 0 and tt % chunk == 0
    n_steps = t_pad // tt
    assert 0 < t_valid - (n_steps - 1) * tt <= tt and t_valid >= CONV_W - 1
    tok = lambda w: pl.BlockSpec((1, tt, w), lambda b, s: (b, s, 0))
    return pl.pallas_call(
        functools.partial(_gdn_kernel, tt=tt, chunk=chunk, n_heads=n_heads, t_valid=t_valid, n_steps=n_steps),
        grid=(nb, n_steps),
        in_specs=[tok(conv_ch), tok(vw), tok(LANES),
                  pl.BlockSpec((1, SUBLANES, conv_ch), lambda b, s: (b, 0, 0)),
                  pl.BlockSpec((1, n_heads, GDN_DK, GDN_DV), lambda b, s: (b, 0, 0, 0)),
                  pl.BlockSpec((SUBLANES, conv_ch), lambda b, s: (0, 0)),
                  pl.BlockSpec((1, LANES), lambda b, s: (0, 0)),
                  pl.BlockSpec((1, LANES), lambda b, s: (0, 0)),
                  pl.BlockSpec((1, GDN_DV), lambda b, s: (0, 0))],
        out_specs=(tok(vw),
                   pl.BlockSpec((1, n_heads, GDN_DK, GDN_DV), lambda b, s: (b, 0, 0, 0)),
                   pl.BlockSpec((1, SUBLANES, conv_ch), lambda b, s: (b, 0, 0))),
        out_shape=(jax.ShapeDtypeStruct((nb, t_pad, vw), BF16),
                   jax.ShapeDtypeStruct((nb, n_heads, GDN_DK, GDN_DV), F32),
                   jax.ShapeDtypeStruct((nb, SUBLANES, conv_ch), F32)),
        scratch_shapes=[pltpu.VMEM((tt + 2 * SUBLANES, conv_ch), F32),
                        pltpu.VMEM((n_heads, GDN_DK, GDN_DV), F32),
                        pltpu.VMEM((tt, n_heads * GDN_DK), F32),
                        pltpu.VMEM((tt, n_heads * GDN_DK), F32),
                        pltpu.VMEM((tt, vw), F32),
                        pltpu.VMEM((tt, LANES), F32),
                        pltpu.VMEM((tt, LANES), F32)],
        compiler_params=pltpu.CompilerParams(dimension_semantics=("arbitrary", "arbitrary"),
                                             vmem_limit_bytes=VMEM_LIMIT),
        name="gated_delta",
    )(x, z, sm, conv0, s0, lw["conv_w"], lw["alog"], lw["dtb"], lw["gdn_norm_g"])


def _outmlp_kernel(x_ref, att_ref, o_ref, woa_ref, wob_ref, g1_ref, g2_ref, g3_ref, wup_ref, wdn_ref,
                   y_ref, *, ff_chunk):
    x = x_ref[...]
    mix = _dot(att_ref[...], woa_ref[...]) + _dot(o_ref[...], wob_ref[...])
    x1 = x + _rms(mix, g1_ref[...])
    h2 = _rms(x1, g2_ref[...]).astype(BF16)
    d_ff = wup_ref.shape[1]
    ff = jnp.zeros(x.shape, F32)
    for c in range(d_ff // ff_chunk):
        u = jnp.maximum(_dot(h2, wup_ref[:, c * ff_chunk:(c + 1) * ff_chunk]), 0.0)
        ff = ff + _dot((u * u).astype(BF16), wdn_ref[c * ff_chunk:(c + 1) * ff_chunk, :])
    y_ref[...] = x1 + _rms(ff, g3_ref[...])


def _outmlp(x2d, att, o, lw, *, tm, ff_chunk):
    n, d = x2d.shape
    att_w = att.shape[1]
    vw = o.shape[1]
    d_ff = lw["wup"].shape[1]
    assert n % tm == 0 and d_ff % ff_chunk == 0
    row = lambda w: pl.BlockSpec((tm, w), lambda i: (i, 0))
    return pl.pallas_call(
        functools.partial(_outmlp_kernel, ff_chunk=ff_chunk),
        grid=(n // tm,),
        in_specs=[row(d), row(att_w), row(vw), _const_spec((att_w, d)), _const_spec((vw, d)),
                  _const_spec((1, d)), _const_spec((1, d)), _const_spec((1, d)),
                  _const_spec((d, d_ff)), _const_spec((d_ff, d))],
        out_specs=row(d),
        out_shape=jax.ShapeDtypeStruct((n, d), F32),
        compiler_params=pltpu.CompilerParams(dimension_semantics=("arbitrary",),
                                             vmem_limit_bytes=VMEM_LIMIT),
        name="outproj_mlp",
    )(x2d, att, o, lw["woa"], lw["wob"], lw["post_mix_g"], lw["pre_mlp_g"], lw["post_mlp_g"],
      lw["wup"], lw["wdn"])


def _layer_weights(l, att_w, conv_ch, gdn_vw, n_gdn_heads, w_in, conv_w, idx_k_norm_g, idx_k_norm_b,
                   gdn_a_log, gdn_dt_bias, gdn_norm_g, w_out, pre_mix_g, post_mix_g, pre_mlp_g,
                   post_mlp_g, w_mlp_up, w_mlp_down):
    idx_w = N_IDX_HEADS * IDX_DIM
    o_ik = 3 * att_w + idx_w
    o_iw = o_ik + IDX_DIM
    o_qkv = o_iw + N_IDX_HEADS
    o_z = o_qkv + conv_ch
    o_gb = o_z + gdn_vw
    o_ga = o_gb + n_gdn_heads
    wi = w_in[l]
    assert wi.shape[1] == o_ga + n_gdn_heads
    d = wi.shape[0]
    n_small = IDX_DIM + N_IDX_HEADS + 2 * n_gdn_heads
    ws = jnp.concatenate([wi[:, o_ik:o_qkv], wi[:, o_gb:], jnp.zeros((d, LANES - n_small), wi.dtype)], axis=1)
    pad_lanes = lambda v, at: jnp.zeros((1, LANES), F32).at[0, at:at + v.shape[0]].set(v.astype(F32))
    smul = jnp.ones((1, LANES), F32).at[0, SM_IW:SM_IW + N_IDX_HEADS].set(N_IDX_HEADS ** -0.5 * IDX_DIM ** -0.5)
    g_lane0 = SM_GB + n_gdn_heads
    vec = lambda v: v[l].astype(F32)[None, :]
    return {
        "wa": wi[:, :o_ik].astype(BF16),
        "wqkv": wi[:, o_qkv:o_z].astype(BF16),
        "wz": wi[:, o_z:o_gb].astype(BF16),
        "ws": ws.astype(BF16),
        "lng": pad_lanes(idx_k_norm_g[l], 0),
        "lnb": pad_lanes(idx_k_norm_b[l], 0),
        "smul": smul,
        "conv_w": jnp.zeros((SUBLANES, conv_ch), F32).at[:CONV_W].set(conv_w[l].astype(F32)),
        "alog": pad_lanes(gdn_a_log[l], g_lane0),
        "dtb": pad_lanes(gdn_dt_bias[l], g_lane0),
        "gdn_norm_g": vec(gdn_norm_g),
        "woa": w_out[l, :att_w].astype(BF16),
        "wob": w_out[l, att_w:].astype(BF16),
        "pre_mix_g": vec(pre_mix_g), "post_mix_g": vec(post_mix_g),
        "pre_mlp_g": vec(pre_mlp_g), "post_mlp_g": vec(post_mlp_g),
        "wup": w_mlp_up[l].astype(BF16),
        "wdn": w_mlp_down[l].astype(BF16),
    }


def _pick_tile(n, prefs):
    for t in prefs:
        if n % t == 0:
            return t
    return n


def kernel(x_prompt, x_sample, cache_k, cache_v, cache_idx_k, page_table, state_gdn, state_conv, w_in, conv_w, idx_k_norm_g, idx_k_norm_b, gdn_a_log, gdn_dt_bias, gdn_norm_g, w_out, pre_mix_g, post_mix_g, pre_mlp_g, post_mlp_g, w_mlp_up, w_mlp_down):
    b, s, d = x_prompt.shape
    bd, t, _ = x_sample.shape
    depth, n_pool, page, n_att_heads, head_dim = cache_k.shape
    n_pages = page_table.shape[1]
    past = n_pages * page
    n_gdn_heads, dk, dv = state_gdn.shape[2:]
    conv_ch = state_conv.shape[-1]
    att_w = n_att_heads * head_dim
    gdn_vw = n_gdn_heads * dv
    assert head_dim == HEAD_DIM and dk == GDN_DK and dv == GDN_DV and cache_idx_k.shape[-1] == IDX_DIM
    assert conv_ch == 2 * n_gdn_heads * dk + gdn_vw and state_conv.shape[2] == CONV_W - 1
    assert t <= SUBLANES and page == LANES

    tab_p = _rope_tables(jnp.arange(s))
    tab_s = tuple(jnp.tile(tb, (bd, 1)) for tb in _rope_tables(past + jnp.arange(t)))
    topk_p = min(TOPK_MAX, s // 4)
    topk_s = min(TOPK_MAX, (past + t) // 4)
    hist = CONV_W - 1

    tm_p = _pick_tile(b * s, (256, 128, 64, 32, 16, 8))
    tm_s = _pick_tile(bd * t, (256, 128, 64, 32, 16, 8))
    tq = _pick_tile(s, (512, 256, 128))
    tk = _pick_tile(s, (512, 256, 128))
    tt_p = _pick_tile(s, (256, 128, 64))
    chunk_p = min(GDN_CHUNK, tt_p)
    chunk_s = 16
    sb_sel = _pick_tile(bd, (8, 4, 2))
    pg_sel = _pick_tile(n_pages, (8, 4, 2, 1))
    pg_att = _pick_tile(n_pages, (16, 8, 4, 2, 1))

    yp = x_prompt.reshape(b * s, d)
    ys = x_sample.reshape(bd * t, d)
    outs_p, outs_s = [], []
    for l in range(depth):
        lw = _layer_weights(l, att_w, conv_ch, gdn_vw, n_gdn_heads, w_in, conv_w, idx_k_norm_g, idx_k_norm_b,
                            gdn_a_log, gdn_dt_bias, gdn_norm_g, w_out, pre_mix_g, post_mix_g, pre_mlp_g,
                            post_mlp_g, w_mlp_up, w_mlp_down)
        (q, kf, vf, kb, vb, vt, iq, ikf, ikb, qkv, z, sm) = _inproj(
            yp, tab_p, lw, att_w=att_w, conv_ch=conv_ch, gdn_vw=gdn_vw, tm=tm_p)
        r3 = lambda a: a.reshape(b, s, a.shape[-1])
        att = _prompt_attention(r3(q), r3(iq), r3(sm), r3(kb), vt, r3(ikb), tq=tq, tk=tk, topk=topk_p)
        o, s_fin, conv_out = _gdn(
            r3(qkv), r3(z), r3(sm), jnp.zeros((b, SUBLANES, conv_ch), F32),
            jnp.zeros((b, n_gdn_heads, dk, dv), F32), lw, tt=tt_p, chunk=chunk_p, t_valid=s)
        yp = _outmlp(yp, att.reshape(b * s, att_w), o.reshape(b * s, gdn_vw), lw, tm=tm_p, ff_chunk=1024)
        outs_p.append((kf.reshape(b, s, n_att_heads, head_dim), vf.reshape(b, s, n_att_heads, head_dim),
                       ikf.reshape(b, s, IDX_DIM), s_fin, conv_out[:, SUBLANES - hist:, :]))

        (q, kf, vf, kb, vb, vt, iq, ikf, ikb, qkv, z, sm) = _inproj(
            ys, tab_s, lw, att_w=att_w, conv_ch=conv_ch, gdn_vw=gdn_vw, tm=tm_s)
        iq_s = iq.reshape(bd, t * N_IDX_HEADS, IDX_DIM)
        w_s = sm[:, SM_IW:SM_IW + N_IDX_HEADS].reshape(bd, t * N_IDX_HEADS, 1)
        pad_new = lambda a: jnp.pad(a.reshape(bd, t, a.shape[-1]), ((0, 0), (0, LANES - t), (0, 0)))
        sel8 = _decode_select(page_table, iq_s, w_s, pad_new(ikb), jnp.swapaxes(cache_idx_k[l], 1, 2),
                              sb=sb_sel, pg=pg_sel, n_tok=t, topk=topk_s)
        q8 = jnp.pad(q.reshape(bd, t, n_att_heads, head_dim).transpose(0, 2, 1, 3),
                     ((0, 0), (0, 0), (0, SUBLANES - t), (0, 0))).reshape(bd, n_att_heads * SUBLANES, head_dim)
        att_s = _decode_attention(page_table, q8, sel8, pad_new(kb), pad_new(vb),
                                  cache_k[l].reshape(n_pool, page * n_att_heads, head_dim),
                                  cache_v[l].reshape(n_pool, page * n_att_heads, head_dim),
                                  pg=pg_att, n_tok=t, n_heads=n_att_heads)
        pad_t = lambda a: jnp.pad(a.reshape(bd, t, a.shape[-1]), ((0, 0), (0, chunk_s - t), (0, 0)))
        conv0 = jnp.pad(state_conv[l].astype(F32), ((0, 0), (SUBLANES - hist, 0), (0, 0)))
        o_s, s_fin_s, conv_out_s = _gdn(pad_t(qkv), pad_t(z), pad_t(sm), conv0, state_gdn[l].astype(F32), lw,
                                        tt=chunk_s, chunk=chunk_s, t_valid=t)
        ys = _outmlp(ys, att_s.reshape(bd * t, att_w), o_s[:, :t].reshape(bd * t, gdn_vw), lw,
                     tm=tm_s, ff_chunk=1024)
        outs_s.append((kf.reshape(bd, t, n_att_heads, head_dim), vf.reshape(bd, t, n_att_heads, head_dim),
                       ikf.reshape(bd, t, IDX_DIM), s_fin_s, conv_out_s[:, SUBLANES - hist:, :]))

    kp, vp, ikp, gp, cp = [jnp.stack([o_[i] for o_ in outs_p]) for i in range(5)]
    ks_, vs_, iks, gs, cs = [jnp.stack([o_[i] for o_ in outs_s]) for i in range(5)]
    return (yp.reshape(b, s, d), ys.reshape(bd, t, d), kp, vp, ikp, gp, cp, ks_, vs_, iks, gs, cs)
```

```python
import functools
import math

import jax
import jax.numpy as jnp
from jax import lax
from jax.experimental import pallas as pl
from jax.experimental.pallas import tpu as pltpu

F32 = jnp.float32
BF16 = jnp.bfloat16

HEAD_DIM = 128
IDX_DIM = 64
N_IDX_HEADS = 8
GDN_DK = 128
GDN_DV = 128
CONV_W = 4
TOPK_MAX = 256
GDN_CHUNK = 64
ROPE_THETA = 10000.0
NORM_EPS = 1e-6

LANES = 128
SUBLANES = 8
VMEM_LIMIT = 56 * 1024 * 1024
NEG_BIG = -1e30
BF16_EXACT_INT = 256
VALUE_STEPS = 28
MIN_NORMAL = 2.0 ** -126
SCORE_ROWS = 256
SM_IW = IDX_DIM
SM_GB = IDX_DIM + N_IDX_HEADS
HIGHEST = lax.Precision.HIGHEST


def _dot(a, b):
    return jnp.dot(a, b, preferred_element_type=F32)


def _dot_nt(a, b):
    return lax.dot_general(a, b, (((1,), (1,)), ((), ())), preferred_element_type=F32)


def _dot_tn(a, b):
    return lax.dot_general(a, b, (((0,), (0,)), ((), ())), preferred_element_type=F32)


def _rms(x, g):
    return x * lax.rsqrt(jnp.mean(x * x, axis=-1, keepdims=True) + NORM_EPS) * g


def _const_spec(shape):
    n = len(shape)
    return pl.BlockSpec(shape, lambda *_: (0,) * n, pipeline_mode=pl.Buffered(1))


def _lane_fold(x, op):
    r = x[:, 0:LANES]
    for j in range(1, x.shape[1] // LANES):
        r = op(r, x[:, j * LANES:(j + 1) * LANES])
    return r


def _inproj_kernel(x_ref, g_ref, wa_ref, wqkv_ref, wz_ref, ws_ref,
                   cos_ref, sin_ref, cosi_ref, sina_ref, sinb_ref, lng_ref, lnb_ref, smul_ref,
                   q_ref, kf_ref, vf_ref, kb_ref, vb_ref, vt_ref, iq_ref, ikf_ref, ikb_ref,
                   qkv_ref, z_ref, sm_ref, *, att_w, q_scale):
    x = x_ref[...]
    h = _rms(x, g_ref[...]).astype(BF16)
    a = _dot(h, wa_ref[...])
    cos = cos_ref[...]
    sin = sin_ref[...]
    n_heads = att_w // HEAD_DIM

    def rope_full(xh):
        return xh * cos + pltpu.roll(xh, HEAD_DIM // 2, axis=1) * sin

    for j in range(n_heads):
        sl = slice(j * HEAD_DIM, (j + 1) * HEAD_DIM)
        q_ref[:, sl] = (rope_full(a[:, sl]) * q_scale).astype(BF16)
        kr = rope_full(a[:, att_w + j * HEAD_DIM: att_w + (j + 1) * HEAD_DIM])
        kf_ref[pl.ds(j, x.shape[0], stride=n_heads), :] = kr
        kb_ref[:, sl] = kr.astype(BF16)
    v = a[:, 2 * att_w:3 * att_w]
    for j in range(n_heads):
        vf_ref[pl.ds(j, x.shape[0], stride=n_heads), :] = v[:, j * HEAD_DIM:(j + 1) * HEAD_DIM]
    vb_ref[...] = v.astype(BF16)
    vt_ref[0] = jnp.transpose(v).astype(BF16)

    cosi = cosi_ref[...]
    sina = sina_ref[...]
    sinb = sinb_ref[...]

    def rope_half(xh):
        return (xh * cosi + pltpu.roll(xh, LANES - IDX_DIM // 2, axis=1) * sina
                + pltpu.roll(xh, IDX_DIM // 2, axis=1) * sinb)

    idx_w = N_IDX_HEADS * IDX_DIM
    for j in range(idx_w // LANES):
        sl = slice(j * LANES, (j + 1) * LANES)
        iq_ref[:, sl] = rope_half(a[:, 3 * att_w + j * LANES: 3 * att_w + (j + 1) * LANES]).astype(BF16)

    qkv_ref[...] = _dot(h, wqkv_ref[...])
    z_ref[...] = _dot(h, wz_ref[...])

    sm = _dot(h, ws_ref[...])
    lane = lax.broadcasted_iota(jnp.int32, sm.shape, 1)
    is_ik = lane < IDX_DIM
    ikraw = jnp.where(is_ik, sm, 0.0)
    mu = jnp.sum(ikraw, axis=-1, keepdims=True) * (1.0 / IDX_DIM)
    xc = jnp.where(is_ik, sm - mu, 0.0)
    var = jnp.sum(xc * xc, axis=-1, keepdims=True) * (1.0 / IDX_DIM)
    ikn = xc * lax.rsqrt(var + NORM_EPS) * lng_ref[...] + lnb_ref[...]
    ikr = rope_half(ikn)
    ikf_ref[...] = ikr[:, :IDX_DIM]
    ikb_ref[...] = ikr[:, :IDX_DIM].astype(BF16)
    sm_ref[...] = sm * smul_ref[...]


def _rope_tables(pos):
    pos = pos.astype(F32)[:, None]
    half = HEAD_DIM // 2
    inv = jnp.power(ROPE_THETA, -jnp.arange(half, dtype=F32) / half)
    ang = pos * inv[None, :]
    c, s = jnp.cos(ang), jnp.sin(ang)
    cos = jnp.concatenate([c, c], axis=-1)
    sin = jnp.concatenate([-s, s], axis=-1)
    halfi = IDX_DIM // 2
    invi = jnp.power(ROPE_THETA, -jnp.arange(halfi, dtype=F32) / halfi)
    angi = pos * invi[None, :]
    ci, si = jnp.cos(angi), jnp.sin(angi)
    zi = jnp.zeros_like(si)
    cosi = jnp.concatenate([ci, ci, ci, ci], axis=-1)
    sina = jnp.concatenate([-si, zi, -si, zi], axis=-1)
    sinb = jnp.concatenate([zi, si, zi, si], axis=-1)
    return cos, sin, cosi, sina, sinb


def _inproj(x2d, tables, lw, *, att_w, conv_ch, gdn_vw, tm):
    n, d = x2d.shape
    idx_w = N_IDX_HEADS * IDX_DIM
    rt = tables[0].shape[0]
    assert n % tm == 0 and rt % tm == 0 and n % rt == 0
    nt = rt // tm
    n_heads = att_w // HEAD_DIM
    wa_w = 3 * att_w + idx_w
    row = lambda w: pl.BlockSpec((tm, w), lambda i: (i, 0))
    tab = pl.BlockSpec((tm, LANES), lambda i: (i % nt, 0))
    out_shapes = (
        jax.ShapeDtypeStruct((n, att_w), BF16),
        jax.ShapeDtypeStruct((n * n_heads, HEAD_DIM), F32),
        jax.ShapeDtypeStruct((n * n_heads, HEAD_DIM), F32),
        jax.ShapeDtypeStruct((n, att_w), BF16),
        jax.ShapeDtypeStruct((n, att_w), BF16),
        jax.ShapeDtypeStruct((n // rt, att_w, rt), BF16),
        jax.ShapeDtypeStruct((n, idx_w), BF16),
        jax.ShapeDtypeStruct((n, IDX_DIM), F32),
        jax.ShapeDtypeStruct((n, IDX_DIM), BF16),
        jax.ShapeDtypeStruct((n, conv_ch), F32),
        jax.ShapeDtypeStruct((n, gdn_vw), F32),
        jax.ShapeDtypeStruct((n, LANES), F32),
    )
    vt_spec = pl.BlockSpec((1, att_w, tm), lambda i: (i // nt, 0, i % nt))
    head_rows = pl.BlockSpec((tm * n_heads, HEAD_DIM), lambda i: (i, 0))
    out_specs = (row(att_w), head_rows, head_rows, row(att_w), row(att_w), vt_spec, row(idx_w),
                 row(IDX_DIM), row(IDX_DIM), row(conv_ch), row(gdn_vw), row(LANES))
    in_specs = [row(d), _const_spec((1, d)), _const_spec((d, wa_w)), _const_spec((d, conv_ch)),
                _const_spec((d, gdn_vw)), _const_spec((d, LANES)),
                tab, tab, tab, tab, tab,
                _const_spec((1, LANES)), _const_spec((1, LANES)), _const_spec((1, LANES))]
    return pl.pallas_call(
        functools.partial(_inproj_kernel, att_w=att_w, q_scale=HEAD_DIM ** -0.5),
        grid=(n // tm,),
        in_specs=in_specs, out_specs=out_specs, out_shape=out_shapes,
        compiler_params=pltpu.CompilerParams(dimension_semantics=("arbitrary",),
                                             vmem_limit_bytes=VMEM_LIMIT),
        name="inproj",
    )(x2d, lw["pre_mix_g"], lw["wa"], lw["wqkv"], lw["wz"], lw["ws"], *tables,
      lw["lng"], lw["lnb"], lw["smul"])


def _tie_break_by_index_search(count, rewrite, col_limit):
    def break_ties(lo, hi, want, tied):
        def in_tie(x):
            return jnp.logical_and(x >= lo, x < hi)

        def jstep(_, c):
            jlo, jhi = c
            jm = jnp.floor((jlo + jhi) * 0.5)
            cnt = count(lambda x, col: jnp.logical_and(in_tie(x), col <= jm))
            ok = cnt >= want
            return jnp.where(ok, jlo, jm), jnp.where(ok, jm, jhi)

        n_it = max(1, math.ceil(math.log2(col_limit + 1)))
        j0 = (jnp.full_like(lo, -1.0), jnp.full_like(lo, float(col_limit - 1)))
        _, jcut = lax.fori_loop(0, n_it, jstep, j0)
        rewrite(lambda x, col: jnp.where(
            jnp.logical_and(tied, jnp.logical_and(in_tie(x), col > jcut)), -jnp.inf, x))
    return break_ties


def _select_threshold(count, break_ties, n_valid, row_max, row_min, topk):
    kf = float(topk)
    flip = jnp.int32(0x7FFFFFFF)

    def to_key(x):
        b = lax.bitcast_convert_type(jnp.where(x == 0.0, 0.0, x), jnp.int32)
        return jnp.where(b < 0, b ^ flip, b)

    def from_key(k):
        min_normal = jnp.int32(0x00800000)
        k = jnp.where(jnp.logical_and(k > 0, k < min_normal), min_normal, k)
        return lax.bitcast_convert_type(jnp.where(k < 0, k ^ flip, k), F32)

    need = n_valid > kf
    spread = jnp.abs(row_max) * (2.0 ** -20) + 2.0 ** -100
    c_ge0 = count(lambda x, col: x >= 0.0)
    c_gt0 = count(lambda x, col: x > 0.0)
    non_neg = jnp.logical_and(need, c_ge0 >= kf)
    at_zero = jnp.logical_and(non_neg, c_gt0 < kf)
    negative = jnp.logical_and(need, c_ge0 < kf)
    lo0 = jnp.where(non_neg, 0.0, jnp.where(need, row_min, NEG_BIG))
    c_lo0 = jnp.where(non_neg, c_ge0, n_valid)
    hi0 = jnp.where(at_zero, MIN_NORMAL, jnp.where(negative, 0.0, row_max + spread))
    c_hi0 = jnp.where(at_zero, c_gt0, jnp.where(negative, c_ge0, 0.0))
    open0 = jnp.logical_and(need, jnp.logical_not(jnp.logical_or(at_zero, c_lo0 == kf)))
    done0 = jnp.where(open0, 0.0, 1.0)

    def narrow(c, mid, stuck, cnt):
        lo, hi, c_lo, c_hi, done = c
        ge = cnt >= kf
        upd = jnp.logical_and(done < 0.5, jnp.logical_not(stuck))
        up_lo = jnp.logical_and(upd, ge)
        up_hi = jnp.logical_and(upd, jnp.logical_not(ge))
        fin = jnp.logical_or(stuck, jnp.logical_and(upd, cnt == kf))
        return (jnp.where(up_lo, mid, lo), jnp.where(up_hi, mid, hi), jnp.where(up_lo, cnt, c_lo),
                jnp.where(up_hi, cnt, c_hi), jnp.where(fin, 1.0, done))

    def value_step(c):
        it, st = c
        lo, hi = st[0], st[1]
        mid = 0.5 * lo + 0.5 * hi
        stuck = jnp.logical_or(mid <= lo, mid >= hi)
        return it + 1, narrow(st, mid, stuck, count(lambda x, col: x >= mid))

    _, (lo, hi, c_lo, c_hi, done) = lax.while_loop(
        lambda c: jnp.logical_and(c[0] < VALUE_STEPS, jnp.min(c[1][4]) < 0.5), value_step,
        (jnp.int32(0), (lo0, hi0, c_lo0, c_hi0, done0)))

    def key_step(st):
        klo, khi = st[0], st[1]
        mid = (klo & khi) + ((klo ^ khi) >> 1)
        return narrow(st, mid, mid <= klo, count(lambda x, col: to_key(x) >= mid))

    klo, khi, c_lo, c_hi, _ = lax.while_loop(
        lambda st: jnp.min(st[4]) < 0.5, key_step, (to_key(lo), to_key(hi), c_lo, c_hi, done))
    lo, hi = from_key(klo), from_key(khi)

    tied = jnp.logical_and(need, c_lo > kf)

    @pl.when(jnp.max(jnp.where(tied, 1.0, 0.0)) > 0.5)
    def _():
        break_ties(lo, hi, kf - c_hi, tied)

    return lo


def _pattn_kernel(q_ref, iq_ref, sm_ref, k_ref, vt_ref, ik_ref, o_ref, sc_ref, m_ref, l_ref, acc_ref,
                  *, tq, tk, topk, n_heads):
    qi = pl.program_id(1)
    t0 = qi * tq
    nkb = (t0 + tq + tk - 1) // tk
    sub_tiles = tk // SUBLANES
    q_t = t0 + lax.broadcasted_iota(jnp.int32, (1, tq), 1)
    smt = jnp.transpose(sm_ref[0])
    w_rows = [smt[SM_IW + h:SM_IW + h + 1, :] for h in range(N_IDX_HEADS)]
    iq = iq_ref[0]
    iq_heads = [iq[:, h * IDX_DIM:(h + 1) * IDX_DIM] for h in range(N_IDX_HEADS)]
    sub_iota = lax.broadcasted_iota(jnp.int32, (SUBLANES, tq), 0)

    def sub_fold(x, op):
        r = x[0:SUBLANES]
        for j in range(1, x.shape[0] // SUBLANES):
            r = op(r, x[j * SUBLANES:(j + 1) * SUBLANES])
        return r

    def rep(row):
        return jnp.broadcast_to(row, (SUBLANES, tq))

    def score_block(kb, carry):
        mx, mn = carry
        k0 = pl.multiple_of(kb * tk, tk)
        for c in range(tk // SCORE_ROWS):
            r0 = k0 + c * SCORE_ROWS
            ikc = ik_ref[0, pl.ds(r0, SCORE_ROWS), :]
            acc = jnp.zeros((SCORE_ROWS, tq), F32)
            for h in range(N_IDX_HEADS):
                acc = acc + w_rows[h] * jnp.maximum(_dot_nt(ikc, iq_heads[h]), 0.0)
            key = r0 + lax.broadcasted_iota(jnp.int32, (SCORE_ROWS, 1), 0)
            valid = key <= q_t
            sc_ref[pl.ds(r0, SCORE_ROWS), :] = jnp.where(valid, acc, -jnp.inf)
            mx = jnp.maximum(mx, sub_fold(jnp.where(valid, acc, -jnp.inf), jnp.maximum))
            mn = jnp.minimum(mn, sub_fold(jnp.where(valid, acc, jnp.inf), jnp.minimum))
        return mx, mn

    mx, mn = lax.fori_loop(0, nkb, score_block,
                           (jnp.full((SUBLANES, tq), -jnp.inf, F32), jnp.full((SUBLANES, tq), jnp.inf, F32)))
    row_max = rep(jnp.max(mx, axis=0, keepdims=True))
    row_min = rep(jnp.min(mn, axis=0, keepdims=True))
    n_valid = rep((q_t + 1).astype(F32))

    def count(pred):
        def body(kb, acc):
            k0 = pl.multiple_of(kb * tk, tk)
            x = sc_ref[pl.ds(k0, tk), :]
            for j in range(sub_tiles):
                hit = pred(x[j * SUBLANES:(j + 1) * SUBLANES], (k0 + j * SUBLANES + sub_iota).astype(F32))
                acc = acc + jnp.where(hit, 1.0, 0.0)
            return acc
        acc = lax.fori_loop(0, nkb, body, jnp.zeros((SUBLANES, tq), F32))
        return rep(jnp.sum(acc, axis=0, keepdims=True))

    def break_ties(lo, hi, want, tied):
        def body(kb, seen):
            k0 = pl.multiple_of(kb * tk, tk)
            x = sc_ref[pl.ds(k0, tk), :]
            out = []
            for j in range(sub_tiles):
                xs = x[j * SUBLANES:(j + 1) * SUBLANES]
                t = jnp.where(tied, jnp.where(xs >= lo, jnp.where(xs < hi, 1.0, 0.0), 0.0), 0.0)
                upto = t
                for sh in (1, 2, 4):
                    upto = upto + jnp.where(sub_iota >= sh, pltpu.roll(upto, sh, axis=0), 0.0)
                out.append(jnp.where(t > 0.5, jnp.where(seen + upto <= want, xs, -jnp.inf), xs))
                seen = seen + rep(upto[SUBLANES - 1:SUBLANES, :])
            sc_ref[pl.ds(k0, tk), :] = jnp.concatenate(out, axis=0)
            return seen
        lax.fori_loop(0, nkb, body, jnp.zeros((SUBLANES, tq), F32))

    lo = _select_threshold(count, break_ties, n_valid, row_max, row_min, topk)
    tile_rows = lambda v8: jnp.concatenate([v8] * sub_tiles, axis=0)
    lo_t = tile_rows(lo)

    def masked_scores(kb, h):
        k0 = pl.multiple_of(kb * tk, tk)
        sl = slice(h * HEAD_DIM, (h + 1) * HEAD_DIM)
        s = _dot_nt(k_ref[0, pl.ds(k0, tk), sl], q_ref[0, :, sl])
        return jnp.where(sc_ref[pl.ds(k0, tk), :] >= lo_t, s, NEG_BIG)

    m_ref[...] = jnp.full(m_ref.shape, NEG_BIG, F32)
    l_ref[...] = jnp.zeros_like(l_ref)
    acc_ref[...] = jnp.zeros_like(acc_ref)

    def pv_block(kb, carry):
        k0 = pl.multiple_of(kb * tk, tk)
        ss = [masked_scores(kb, h) for h in range(n_heads)]
        hss = [slice(h * SUBLANES, (h + 1) * SUBLANES) for h in range(n_heads)]
        m_old = [m_ref[hs, :] for hs in hss]
        m_new = [jnp.maximum(m_old[h], rep(jnp.max(sub_fold(ss[h], jnp.maximum), axis=0, keepdims=True)))
                 for h in range(n_heads)]
        ps = [jnp.exp(ss[h] - tile_rows(m_new[h])) for h in range(n_heads)]
        for h in range(n_heads):
            sl = slice(h * HEAD_DIM, (h + 1) * HEAD_DIM)
            alpha = jnp.exp(m_old[h] - m_new[h])
            m_ref[hss[h], :] = m_new[h]
            l_ref[hss[h], :] = alpha * l_ref[hss[h], :] + sub_fold(ps[h], jnp.add)
            acc_ref[sl, :] = (alpha[0:1, :] * acc_ref[sl, :]
                              + _dot(vt_ref[0, sl, pl.ds(k0, tk)], ps[h].astype(BF16)))
        return carry
    lax.fori_loop(0, nkb, pv_block, 0)

    for h in range(n_heads):
        sl = slice(h * HEAD_DIM, (h + 1) * HEAD_DIM)
        l_row = jnp.sum(l_ref[h * SUBLANES:(h + 1) * SUBLANES, :], axis=0, keepdims=True)
        o_ref[0, :, sl] = jnp.transpose(acc_ref[sl, :] / l_row).astype(o_ref.dtype)


def _prompt_attention(q, iq, sm, kb, vt, ikb, *, tq, tk, topk):
    b, s, att_w = q.shape
    idx_w = iq.shape[-1]
    n_heads = att_w // HEAD_DIM
    assert s % tq == 0 and s % tk == 0 and tk % tq == 0 and tq % LANES == 0
    blk = lambda w: pl.BlockSpec((1, tq, w), lambda bi, qi: (bi, qi, 0))
    full = lambda r, w: pl.BlockSpec((1, r, w), lambda bi, qi: (bi, 0, 0), pipeline_mode=pl.Buffered(1))
    return pl.pallas_call(
        functools.partial(_pattn_kernel, tq=tq, tk=tk, topk=topk, n_heads=n_heads),
        grid=(b, s // tq),
        in_specs=[blk(att_w), blk(idx_w), blk(LANES), full(s, att_w), full(att_w, s), full(s, IDX_DIM)],
        out_specs=blk(att_w),
        out_shape=jax.ShapeDtypeStruct((b, s, att_w), BF16),
        scratch_shapes=[pltpu.VMEM((s, tq), F32), pltpu.VMEM((n_heads * SUBLANES, tq), F32),
                        pltpu.VMEM((n_heads * SUBLANES, tq), F32), pltpu.VMEM((att_w, tq), F32)],
        compiler_params=pltpu.CompilerParams(dimension_semantics=("arbitrary", "arbitrary"),
                                             vmem_limit_bytes=VMEM_LIMIT),
        name="prompt_attention",
    )(q, iq, sm, kb, vt, ikb)


def _dsel_kernel(pt_ref, iq_ref, w_ref, iknew_ref, *rest, sb, pg, n_tok, past, topk):
    pages = rest[:sb * pg]
    sel_ref = rest[sb * pg]
    sc_ref = rest[sb * pg + 1]
    c = pl.program_id(1)
    n_c = pl.num_programs(1)
    page = pages[0].shape[2]
    rows, l_pad = sc_ref.shape

    def token_rows(s):
        return [jnp.sum(s[t * N_IDX_HEADS:(t + 1) * N_IDX_HEADS, :], axis=0, keepdims=True)
                for t in range(n_tok)]

    @pl.when(c == 0)
    def _():
        sc_ref[...] = jnp.full(sc_ref.shape, -jnp.inf, F32)

    for j in range(sb):
        iq = iq_ref[j]
        w = w_ref[j]
        kt = jnp.concatenate([pages[j * pg + p][0] for p in range(pg)], axis=1).astype(BF16)
        tr = token_rows(jnp.maximum(_dot(iq, kt), 0.0) * w)
        k0 = pl.multiple_of(c * pg * page, pg * page)
        for t in range(n_tok):
            sc_ref[j * n_tok + t:j * n_tok + t + 1, pl.ds(k0, pg * page)] = tr[t]

    @pl.when(c == n_c - 1)
    def _():
        col = lax.broadcasted_iota(jnp.int32, (1, LANES), 1)
        for j in range(sb):
            tr = token_rows(jnp.maximum(_dot_nt(iq_ref[j], iknew_ref[j]), 0.0) * w_ref[j])
            for t in range(n_tok):
                sc_ref[j * n_tok + t:j * n_tok + t + 1, past:past + LANES] = jnp.where(col <= t, tr[t], -jnp.inf)
        x = sc_ref[...]
        rep = lambda v: jnp.broadcast_to(v, (rows, LANES))
        row_max = rep(jnp.max(x, axis=-1, keepdims=True))
        row_min = rep(jnp.min(jnp.where(x > -jnp.inf, x, jnp.inf), axis=-1, keepdims=True))
        r = lax.broadcasted_iota(jnp.int32, (rows, LANES), 0).astype(F32)
        tok = r - n_tok * jnp.floor(r * (1.0 / n_tok))
        n_valid = past + 1.0 + tok
        n_blocks = l_pad // LANES
        assert n_blocks <= BF16_EXACT_INT
        ones_b = jnp.ones((LANES, LANES), BF16)
        lane_iota = lax.broadcasted_iota(jnp.int32, (rows, LANES), 1)

        def count(pred):
            def body(kb, acc):
                c0 = pl.multiple_of(kb * LANES, LANES)
                hit = pred(sc_ref[:, pl.ds(c0, LANES)], (c0 + lane_iota).astype(F32))
                return acc + jnp.where(hit, 1.0, 0.0)
            acc = lax.fori_loop(0, n_blocks, body, jnp.zeros((rows, LANES), F32))
            return _dot(acc.astype(BF16), ones_b)

        def rewrite(fn):
            def body(kb, carry):
                c0 = pl.multiple_of(kb * LANES, LANES)
                sc_ref[:, pl.ds(c0, LANES)] = fn(sc_ref[:, pl.ds(c0, LANES)], (c0 + lane_iota).astype(F32))
                return carry
            lax.fori_loop(0, n_blocks, body, 0)

        lo = _select_threshold(count, _tie_break_by_index_search(count, rewrite, l_pad),
                               n_valid, row_max, row_min, topk)

        def emit(kb, carry):
            c0 = pl.multiple_of(kb * LANES, LANES)
            x = jnp.where(sc_ref[:, pl.ds(c0, LANES)] >= lo, 1.0, 0.0)
            for j in range(sb):
                sel_ref[j, 0:n_tok, pl.ds(c0, LANES)] = x[j * n_tok:(j + 1) * n_tok]
                sel_ref[j, n_tok:SUBLANES, pl.ds(c0, LANES)] = jnp.zeros((SUBLANES - n_tok, LANES), F32)
            return carry
        lax.fori_loop(0, l_pad // LANES, emit, 0)


def _decode_select(page_table, iq, w, iknew, cache_idx_t, *, sb, pg, n_tok, topk):
    bd, n_pages = page_table.shape
    _, di, page = cache_idx_t.shape
    past = n_pages * page
    assert n_pages % pg == 0 and bd % sb == 0 and (sb * n_tok) % SUBLANES == 0
    l_pad = past + LANES
    rows = iq.shape[1]

    def page_spec(j, p):
        return pl.BlockSpec((1, di, page), lambda g, c, pt: (pt[g * sb + j, c * pg + p], 0, 0))

    grid_spec = pltpu.PrefetchScalarGridSpec(
        num_scalar_prefetch=1,
        grid=(bd // sb, n_pages // pg),
        in_specs=[pl.BlockSpec((sb, rows, di), lambda g, c, pt: (g, 0, 0)),
                  pl.BlockSpec((sb, rows, 1), lambda g, c, pt: (g, 0, 0)),
                  pl.BlockSpec((sb, LANES, di), lambda g, c, pt: (g, 0, 0))]
                 + [page_spec(j, p) for j in range(sb) for p in range(pg)],
        out_specs=pl.BlockSpec((sb, SUBLANES, l_pad), lambda g, c, pt: (g, 0, 0)),
        scratch_shapes=[pltpu.VMEM((sb * n_tok, l_pad), F32)],
    )
    return pl.pallas_call(
        functools.partial(_dsel_kernel, sb=sb, pg=pg, n_tok=n_tok, past=past, topk=topk),
        grid_spec=grid_spec,
        out_shape=jax.ShapeDtypeStruct((bd, SUBLANES, l_pad), F32),
        compiler_params=pltpu.CompilerParams(dimension_semantics=("arbitrary", "arbitrary"),
                                             vmem_limit_bytes=VMEM_LIMIT),
        name="decode_select",
    )(page_table, iq, w, iknew, *([cache_idx_t] * (sb * pg)))


def _dattn_kernel(pt_ref, q_ref, sel_ref, knew_ref, vnew_ref, *rest, pg, n_tok, n_heads, past):
    kpages = rest[:pg]
    vpages = rest[pg:2 * pg]
    o_ref = rest[2 * pg]
    m_ref, l_ref, acc_ref = rest[2 * pg + 1:]
    c = pl.program_id(1)
    n_c = pl.num_programs(1)
    page = kpages[0].shape[1] // n_heads
    qh = [q_ref[0, h * SUBLANES:(h + 1) * SUBLANES, :] for h in range(n_heads)]

    @pl.when(c == 0)
    def _():
        m_ref[...] = jnp.full(m_ref.shape, NEG_BIG, F32)
        l_ref[...] = jnp.zeros_like(l_ref)
        acc_ref[...] = jnp.zeros_like(acc_ref)

    def update(k_of, v_of, n_blk, sel):
        s = jnp.concatenate(
            [jnp.concatenate([_dot_nt(qh[h], k_of(p, h)) for p in range(n_blk)], axis=1)
             for h in range(n_heads)], axis=0)
        s = jnp.where(jnp.concatenate([sel] * n_heads, axis=0) > 0.5, s, NEG_BIG)
        m_old = m_ref[...]
        m_new = jnp.maximum(m_old, jnp.max(s, axis=-1, keepdims=True))
        pr = jnp.exp(s - m_new).astype(BF16)
        alpha = jnp.exp(m_old - m_new)
        l_ref[...] = alpha * l_ref[...] + jnp.sum(pr.astype(F32), axis=-1, keepdims=True)
        pv = []
        for h in range(n_heads):
            a = jnp.zeros((SUBLANES, HEAD_DIM), F32)
            for p in range(n_blk):
                a = a + _dot(pr[h * SUBLANES:(h + 1) * SUBLANES, p * page:(p + 1) * page], v_of(p, h))
            pv.append(a)
        acc_ref[...] = alpha * acc_ref[...] + jnp.concatenate(pv, axis=0)
        m_ref[...] = m_new

    head_rows = lambda ref, h: ref[0, pl.ds(h, page, stride=n_heads), :].astype(BF16)
    k0 = pl.multiple_of(c * pg * page, pg * page)
    update(lambda p, h: head_rows(kpages[p], h), lambda p, h: head_rows(vpages[p], h), pg,
           sel_ref[0, :, pl.ds(k0, pg * page)])

    @pl.when(c == n_c - 1)
    def _():
        new = lambda ref, h: ref[0, :, h * HEAD_DIM:(h + 1) * HEAD_DIM]
        update(lambda p, h: new(knew_ref, h), lambda p, h: new(vnew_ref, h), 1,
               sel_ref[0, :, past:past + LANES])
        inv_l = 1.0 / l_ref[...]
        out = acc_ref[...] * inv_l
        for h in range(n_heads):
            o_ref[0, :, h * HEAD_DIM:(h + 1) * HEAD_DIM] = out[h * SUBLANES:h * SUBLANES + n_tok, :].astype(o_ref.dtype)


def _decode_attention(page_table, q8, sel, knew, vnew, cache_k, cache_v, *, pg, n_tok, n_heads):
    bd, n_pages = page_table.shape
    _, page_rows, hd = cache_k.shape
    page = page_rows // n_heads
    past = n_pages * page
    l_pad = sel.shape[-1]
    att_w = n_heads * hd
    rows = n_heads * SUBLANES
    assert n_pages % pg == 0 and page == LANES

    def page_spec(p):
        return pl.BlockSpec((1, page_rows, hd), lambda b, c, pt: (pt[b, c * pg + p], 0, 0))

    per_seq = lambda r, w: pl.BlockSpec((1, r, w), lambda b, c, pt: (b, 0, 0))
    grid_spec = pltpu.PrefetchScalarGridSpec(
        num_scalar_prefetch=1,
        grid=(bd, n_pages // pg),
        in_specs=[per_seq(rows, hd), per_seq(SUBLANES, l_pad), per_seq(LANES, att_w), per_seq(LANES, att_w)]
                 + [page_spec(p) for p in range(pg)] * 2,
        out_specs=per_seq(n_tok, att_w),
        scratch_shapes=[pltpu.VMEM((rows, 1), F32), pltpu.VMEM((rows, 1), F32), pltpu.VMEM((rows, hd), F32)],
    )
    return pl.pallas_call(
        functools.partial(_dattn_kernel, pg=pg, n_tok=n_tok, n_heads=n_heads, past=past),
        grid_spec=grid_spec,
        out_shape=jax.ShapeDtypeStruct((bd, n_tok, att_w), BF16),
        compiler_params=pltpu.CompilerParams(dimension_semantics=("arbitrary", "arbitrary"),
                                             vmem_limit_bytes=VMEM_LIMIT),
        name="decode_attention",
    )(page_table, q8, sel, knew, vnew, *([cache_k] * pg), *([cache_v] * pg))


def _gdn_kernel(x_ref, z_ref, sm_ref, conv0_ref, s0_ref, cw_ref, alog_ref, dtb_ref, ng_ref,
                o_ref, sfin_ref, convout_ref,
                xbuf, st_ref, q_s, k_s, v_s, beta_s, g_s,
                *, tt, chunk, n_heads, t_valid, n_steps):
    step = pl.program_id(1)
    hist = CONV_W - 1
    qk_w = n_heads * GDN_DK

    @pl.when(step == 0)
    def _():
        xbuf[0:SUBLANES, :] = conv0_ref[0]
        st_ref[...] = s0_ref[0]

    @pl.when(step > 0)
    def _():
        xbuf[0:SUBLANES, :] = xbuf[tt:tt + SUBLANES, :]

    xbuf[SUBLANES:SUBLANES + tt, :] = x_ref[0]
    cw = cw_ref[...]
    conv = cw[hist:hist + 1, :] * xbuf[SUBLANES:SUBLANES + tt, :]
    for j in range(hist):
        conv = conv + cw[j:j + 1, :] * xbuf[SUBLANES - hist + j:SUBLANES - hist + j + tt, :]
    act = conv * jax.nn.sigmoid(conv)

    def l2n(xh):
        return xh * lax.rsqrt(jnp.sum(xh * xh, axis=-1, keepdims=True) + NORM_EPS)

    for h in range(n_heads):
        sl = slice(h * GDN_DK, (h + 1) * GDN_DK)
        q_s[:, sl] = l2n(act[:, sl]) * (GDN_DK ** -0.5)
        k_s[:, sl] = l2n(act[:, qk_w + h * GDN_DK: qk_w + (h + 1) * GDN_DK])
    v_s[...] = act[:, 2 * qk_w:]

    smv = sm_ref[0]
    t_glob = step * tt + lax.broadcasted_iota(jnp.int32, (tt, 1), 0)
    live = t_glob < t_valid
    beta_s[...] = jnp.where(live, jax.nn.sigmoid(smv), 0.0)
    g_s[...] = jnp.where(live, -jnp.exp(alog_ref[...]) * jax.nn.softplus(smv + dtb_ref[...]), 0.0)

    ii = lax.broadcasted_iota(jnp.int32, (chunk, chunk), 0)
    jj = lax.broadcasted_iota(jnp.int32, (chunk, chunk), 1)
    tril = ii >= jj
    strict = ii > jj
    tril_f = jnp.where(tril, 1.0, 0.0)
    eye = jnp.where(ii == jj, 1.0, 0.0)
    lane8 = lax.broadcasted_iota(jnp.int32, (SUBLANES, LANES), 1)
    row8 = lax.broadcasted_iota(jnp.int32, (SUBLANES, LANES), 0)
    g_lane0 = SM_GB + n_heads
    pick_g = jnp.where(lane8 == g_lane0 + row8, 1.0, 0.0)
    n_levels = int(math.log2(chunk))
    ng = ng_ref[...]

    n_chunks = tt // chunk
    units = [(ci, h) for ci in range(n_chunks) for h in range(n_heads)]
    rows_of = lambda ci: slice(ci * chunk, (ci + 1) * chunk)
    lanes_of = lambda h: slice(h * GDN_DK, (h + 1) * GDN_DK)
    gcs = [jnp.dot(tril_f, g_s[rows_of(ci), :], precision=HIGHEST, preferred_element_type=F32)
           for ci in range(n_chunks)]
    gcrs = [lax.dot_general(pick_g, gc, (((1,), (1,)), ((), ())), precision=HIGHEST,
                            preferred_element_type=F32) for gc in gcs]
    gcol = {(ci, h): gcs[ci][:, g_lane0 + h:g_lane0 + h + 1] for ci, h in units}
    bcol = {(ci, h): beta_s[rows_of(ci), SM_GB + h:SM_GB + h + 1] for ci, h in units}
    decay = {u: jnp.exp(jnp.where(tril, gcol[u] - gcrs[u[0]][u[1]:u[1] + 1, :], -jnp.inf)) for u in units}
    k_b = {(ci, h): k_s[rows_of(ci), lanes_of(h)].astype(BF16) for ci, h in units}
    kb_f = {(ci, h): k_s[rows_of(ci), lanes_of(h)] * bcol[(ci, h)] for ci, h in units}
    lmat = {u: jnp.where(strict, _dot_nt(kb_f[u].astype(BF16), k_b[u]) * decay[u], 0.0) for u in units}
    qk = {(ci, h): (_dot_nt(q_s[rows_of(ci), lanes_of(h)].astype(BF16), k_b[(ci, h)]) * decay[(ci, h)]).astype(BF16)
          for ci, h in units}
    pw = {u: -lmat[u] for u in units}
    inv = {u: eye + pw[u] for u in units}
    for _ in range(n_levels - 1):
        pw = {u: _dot(pw[u].astype(BF16), pw[u].astype(BF16)) for u in units}
        inv = {u: inv[u] + _dot(inv[u].astype(BF16), pw[u].astype(BF16)) for u in units}
    inv_b = {u: inv[u].astype(BF16) for u in units}
    u_rhs = {(ci, h): _dot(inv_b[(ci, h)], (v_s[rows_of(ci), lanes_of(h)] * bcol[(ci, h)]).astype(BF16))
             for ci, h in units}
    w_rhs = {u: _dot(inv_b[u], (kb_f[u] * jnp.exp(gcol[u])).astype(BF16)).astype(BF16) for u in units}

    state = [st_ref[h] for h in range(n_heads)]
    for ci in range(n_chunks):
        heads = [(ci, h) for h in range(n_heads)]
        s_b = [s.astype(BF16) for s in state]
        ws = [_dot(w_rhs[u], s_b[u[1]]) for u in heads]
        qs = [_dot((q_s[rows_of(ci), lanes_of(h)] * jnp.exp(gcol[(ci, h)])).astype(BF16), s_b[h]) for _, h in heads]
        v_new = [(u_rhs[u] - ws[u[1]]).astype(BF16) for u in heads]
        outs = [qs[h] + _dot(qk[(ci, h)], v_new[h]) for _, h in heads]
        g_last = [gcs[ci][chunk - 1:chunk, g_lane0 + h:g_lane0 + h + 1] for _, h in heads]
        k_dec = [(k_s[rows_of(ci), lanes_of(h)] * jnp.exp(g_last[h] - gcol[(ci, h)])).astype(BF16) for _, h in heads]
        state = [state[h] * jnp.exp(g_last[h]) + _dot_tn(k_dec[h], v_new[h]) for _, h in heads]
        for _, h in heads:
            zh = z_ref[0, rows_of(ci), lanes_of(h)]
            o_ref[0, rows_of(ci), lanes_of(h)] = (_rms(outs[h], ng) * (zh * jax.nn.sigmoid(zh))).astype(o_ref.dtype)
    for h in range(n_heads):
        st_ref[h] = state[h]

    @pl.when(step == n_steps - 1)
    def _():
        sfin_ref[0] = st_ref[...]
        last = t_valid - (n_steps - 1) * tt
        convout_ref[0] = xbuf[last:last + SUBLANES, :]


def _gdn(x, z, sm, conv0, s0, lw, *, tt, chunk, t_valid):
    nb, t_pad, conv_ch = x.shape
    n_heads = s0.shape[1]
    vw = z.shape[-1]
    assert t_pad % tt == 0 and tt % chunk == 0
    n_steps = t_pad // tt
    assert 0 < t_valid - (n_steps - 1) * tt <= tt and t_valid >= CONV_W - 1
    tok = lambda w: pl.BlockSpec((1, tt, w), lambda b, s: (b, s, 0))
    return pl.pallas_call(
        functools.partial(_gdn_kernel, tt=tt, chunk=chunk, n_heads=n_heads, t_valid=t_valid, n_steps=n_steps),
        grid=(nb, n_steps),
        in_specs=[tok(conv_ch), tok(vw), tok(LANES),
                  pl.BlockSpec((1, SUBLANES, conv_ch), lambda b, s: (b, 0, 0)),
                  pl.BlockSpec((1, n_heads, GDN_DK, GDN_DV), lambda b, s: (b, 0, 0, 0)),
                  pl.BlockSpec((SUBLANES, conv_ch), lambda b, s: (0, 0)),
                  pl.BlockSpec((1, LANES), lambda b, s: (0, 0)),
                  pl.BlockSpec((1, LANES), lambda b, s: (0, 0)),
                  pl.BlockSpec((1, GDN_DV), lambda b, s: (0, 0))],
        out_specs=(tok(vw),
                   pl.BlockSpec((1, n_heads, GDN_DK, GDN_DV), lambda b, s: (b, 0, 0, 0)),
                   pl.BlockSpec((1, SUBLANES, conv_ch), lambda b, s: (b, 0, 0))),
        out_shape=(jax.ShapeDtypeStruct((nb, t_pad, vw), BF16),
                   jax.ShapeDtypeStruct((nb, n_heads, GDN_DK, GDN_DV), F32),
                   jax.ShapeDtypeStruct((nb, SUBLANES, conv_ch), F32)),
        scratch_shapes=[pltpu.VMEM((tt + 2 * SUBLANES, conv_ch), F32),
                        pltpu.VMEM((n_heads, GDN_DK, GDN_DV), F32),
                        pltpu.VMEM((tt, n_heads * GDN_DK), F32),
                        pltpu.VMEM((tt, n_heads * GDN_DK), F32),
                        pltpu.VMEM((tt, vw), F32),
                        pltpu.VMEM((tt, LANES), F32),
                        pltpu.VMEM((tt, LANES), F32)],
        compiler_params=pltpu.CompilerParams(dimension_semantics=("arbitrary", "arbitrary"),
                                             vmem_limit_bytes=VMEM_LIMIT),
        name="gated_delta",
    )(x, z, sm, conv0, s0, lw["conv_w"], lw["alog"], lw["dtb"], lw["gdn_norm_g"])


def _outmlp_kernel(x_ref, att_ref, o_ref, woa_ref, wob_ref, g1_ref, g2_ref, g3_ref, wup_ref, wdn_ref,
                   y_ref, *, ff_chunk):
    x = x_ref[...]
    mix = _dot(att_ref[...], woa_ref[...]) + _dot(o_ref[...], wob_ref[...])
    x1 = x + _rms(mix, g1_ref[...])
    h2 = _rms(x1, g2_ref[...]).astype(BF16)
    d_ff = wup_ref.shape[1]
    ff = jnp.zeros(x.shape, F32)
    for c in range(d_ff // ff_chunk):
        u = jnp.maximum(_dot(h2, wup_ref[:, c * ff_chunk:(c + 1) * ff_chunk]), 0.0)
        ff = ff + _dot((u * u).astype(BF16), wdn_ref[c * ff_chunk:(c + 1) * ff_chunk, :])
    y_ref[...] = x1 + _rms(ff, g3_ref[...])


def _outmlp(x2d, att, o, lw, *, tm, ff_chunk):
    n, d = x2d.shape
    att_w = att.shape[1]
    vw = o.shape[1]
    d_ff = lw["wup"].shape[1]
    assert n % tm == 0 and d_ff % ff_chunk == 0
    row = lambda w: pl.BlockSpec((tm, w), lambda i: (i, 0))
    return pl.pallas_call(
        functools.partial(_outmlp_kernel, ff_chunk=ff_chunk),
        grid=(n // tm,),
        in_specs=[row(d), row(att_w), row(vw), _const_spec((att_w, d)), _const_spec((vw, d)),
                  _const_spec((1, d)), _const_spec((1, d)), _const_spec((1, d)),
                  _const_spec((d, d_ff)), _const_spec((d_ff, d))],
        out_specs=row(d),
        out_shape=jax.ShapeDtypeStruct((n, d), F32),
        compiler_params=pltpu.CompilerParams(dimension_semantics=("arbitrary",),
                                             vmem_limit_bytes=VMEM_LIMIT),
        name="outproj_mlp",
    )(x2d, att, o, lw["woa"], lw["wob"], lw["post_mix_g"], lw["pre_mlp_g"], lw["post_mlp_g"],
      lw["wup"], lw["wdn"])


def _layer_weights(l, att_w, conv_ch, gdn_vw, n_gdn_heads, w_in, conv_w, idx_k_norm_g, idx_k_norm_b,
                   gdn_a_log, gdn_dt_bias, gdn_norm_g, w_out, pre_mix_g, post_mix_g, pre_mlp_g,
                   post_mlp_g, w_mlp_up, w_mlp_down):
    idx_w = N_IDX_HEADS * IDX_DIM
    o_ik = 3 * att_w + idx_w
    o_iw = o_ik + IDX_DIM
    o_qkv = o_iw + N_IDX_HEADS
    o_z = o_qkv + conv_ch
    o_gb = o_z + gdn_vw
    o_ga = o_gb + n_gdn_heads
    wi = w_in[l]
    assert wi.shape[1] == o_ga + n_gdn_heads
    d = wi.shape[0]
    n_small = IDX_DIM + N_IDX_HEADS + 2 * n_gdn_heads
    ws = jnp.concatenate([wi[:, o_ik:o_qkv], wi[:, o_gb:], jnp.zeros((d, LANES - n_small), wi.dtype)], axis=1)
    pad_lanes = lambda v, at: jnp.zeros((1, LANES), F32).at[0, at:at + v.shape[0]].set(v.astype(F32))
    smul = jnp.ones((1, LANES), F32).at[0, SM_IW:SM_IW + N_IDX_HEADS].set(N_IDX_HEADS ** -0.5 * IDX_DIM ** -0.5)
    g_lane0 = SM_GB + n_gdn_heads
    vec = lambda v: v[l].astype(F32)[None, :]
    return {
        "wa": wi[:, :o_ik].astype(BF16),
        "wqkv": wi[:, o_qkv:o_z].astype(BF16),
        "wz": wi[:, o_z:o_gb].astype(BF16),
        "ws": ws.astype(BF16),
        "lng": pad_lanes(idx_k_norm_g[l], 0),
        "lnb": pad_lanes(idx_k_norm_b[l], 0),
        "smul": smul,
        "conv_w": jnp.zeros((SUBLANES, conv_ch), F32).at[:CONV_W].set(conv_w[l].astype(F32)),
        "alog": pad_lanes(gdn_a_log[l], g_lane0),
        "dtb": pad_lanes(gdn_dt_bias[l], g_lane0),
        "gdn_norm_g": vec(gdn_norm_g),
        "woa": w_out[l, :att_w].astype(BF16),
        "wob": w_out[l, att_w:].astype(BF16),
        "pre_mix_g": vec(pre_mix_g), "post_mix_g": vec(post_mix_g),
        "pre_mlp_g": vec(pre_mlp_g), "post_mlp_g": vec(post_mlp_g),
        "wup": w_mlp_up[l].astype(BF16),
        "wdn": w_mlp_down[l].astype(BF16),
    }


def _pick_tile(n, prefs):
    for t in prefs:
        if n % t == 0:
            return t
    return n


def kernel(x_prompt, x_sample, cache_k, cache_v, cache_idx_k, page_table, state_gdn, state_conv, w_in, conv_w, idx_k_norm_g, idx_k_norm_b, gdn_a_log, gdn_dt_bias, gdn_norm_g, w_out, pre_mix_g, post_mix_g, pre_mlp_g, post_mlp_g, w_mlp_up, w_mlp_down):
    b, s, d = x_prompt.shape
    bd, t, _ = x_sample.shape
    depth, n_pool, page, n_att_heads, head_dim = cache_k.shape
    n_pages = page_table.shape[1]
    past = n_pages * page
    n_gdn_heads, dk, dv = state_gdn.shape[2:]
    conv_ch = state_conv.shape[-1]
    att_w = n_att_heads * head_dim
    gdn_vw = n_gdn_heads * dv
    assert head_dim == HEAD_DIM and dk == GDN_DK and dv == GDN_DV and cache_idx_k.shape[-1] == IDX_DIM
    assert conv_ch == 2 * n_gdn_heads * dk + gdn_vw and state_conv.shape[2] == CONV_W - 1
    assert t <= SUBLANES and page == LANES

    tab_p = _rope_tables(jnp.arange(s))
    tab_s = tuple(jnp.tile(tb, (bd, 1)) for tb in _rope_tables(past + jnp.arange(t)))
    topk_p = min(TOPK_MAX, s // 4)
    topk_s = min(TOPK_MAX, (past + t) // 4)
    hist = CONV_W - 1

    tm_p = _pick_tile(b * s, (256, 128, 64, 32, 16, 8))
    tm_s = _pick_tile(bd * t, (256, 128, 64, 32, 16, 8))
    tq = _pick_tile(s, (512, 256, 128))
    tk = _pick_tile(s, (512, 256, 128))
    tt_p = _pick_tile(s, (256, 128, 64))
    chunk_p = min(GDN_CHUNK, tt_p)
    chunk_s = 16
    sb_sel = _pick_tile(bd, (8, 4, 2))
    pg_sel = _pick_tile(n_pages, (8, 4, 2, 1))
    pg_att = _pick_tile(n_pages, (16, 8, 4, 2, 1))

    yp = x_prompt.reshape(b * s, d)
    ys = x_sample.reshape(bd * t, d)
    outs_p, outs_s = [], []
    for l in range(depth):
        lw = _layer_weights(l, att_w, conv_ch, gdn_vw, n_gdn_heads, w_in, conv_w, idx_k_norm_g, idx_k_norm_b,
                            gdn_a_log, gdn_dt_bias, gdn_norm_g, w_out, pre_mix_g, post_mix_g, pre_mlp_g,
                            post_mlp_g, w_mlp_up, w_mlp_down)
        (q, kf, vf, kb, vb, vt, iq, ikf, ikb, qkv, z, sm) = _inproj(
            yp, tab_p, lw, att_w=att_w, conv_ch=conv_ch, gdn_vw=gdn_vw, tm=tm_p)
        r3 = lambda a: a.reshape(b, s, a.shape[-1])
        att = _prompt_attention(r3(q), r3(iq), r3(sm), r3(kb), vt, r3(ikb), tq=tq, tk=tk, topk=topk_p)
        o, s_fin, conv_out = _gdn(
            r3(qkv), r3(z), r3(sm), jnp.zeros((b, SUBLANES, conv_ch), F32),
            jnp.zeros((b, n_gdn_heads, dk, dv), F32), lw, tt=tt_p, chunk=chunk_p, t_valid=s)
        yp = _outmlp(yp, att.reshape(b * s, att_w), o.reshape(b * s, gdn_vw), lw, tm=tm_p, ff_chunk=1024)
        outs_p.append((kf.reshape(b, s, n_att_heads, head_dim), vf.reshape(b, s, n_att_heads, head_dim),
                       ikf.reshape(b, s, IDX_DIM), s_fin, conv_out[:, SUBLANES - hist:, :]))

        (q, kf, vf, kb, vb, vt, iq, ikf, ikb, qkv, z, sm) = _inproj(
            ys, tab_s, lw, att_w=att_w, conv_ch=conv_ch, gdn_vw=gdn_vw, tm=tm_s)
        iq_s = iq.reshape(bd, t * N_IDX_HEADS, IDX_DIM)
        w_s = sm[:, SM_IW:SM_IW + N_IDX_HEADS].reshape(bd, t * N_IDX_HEADS, 1)
        pad_new = lambda a: jnp.pad(a.reshape(bd, t, a.shape[-1]), ((0, 0), (0, LANES - t), (0, 0)))
        sel8 = _decode_select(page_table, iq_s, w_s, pad_new(ikb), jnp.swapaxes(cache_idx_k[l], 1, 2),
                              sb=sb_sel, pg=pg_sel, n_tok=t, topk=topk_s)
        q8 = jnp.pad(q.reshape(bd, t, n_att_heads, head_dim).transpose(0, 2, 1, 3),
                     ((0, 0), (0, 0), (0, SUBLANES - t), (0, 0))).reshape(bd, n_att_heads * SUBLANES, head_dim)
        att_s = _decode_attention(page_table, q8, sel8, pad_new(kb), pad_new(vb),
                                  cache_k[l].reshape(n_pool, page * n_att_heads, head_dim),
                                  cache_v[l].reshape(n_pool, page * n_att_heads, head_dim),
                                  pg=pg_att, n_tok=t, n_heads=n_att_heads)
        pad_t = lambda a: jnp.pad(a.reshape(bd, t, a.shape[-1]), ((0, 0), (0, chunk_s - t), (0, 0)))
        conv0 = jnp.pad(state_conv[l].astype(F32), ((0, 0), (SUBLANES - hist, 0), (0, 0)))
        o_s, s_fin_s, conv_out_s = _gdn(pad_t(qkv), pad_t(z), pad_t(sm), conv0, state_gdn[l].astype(F32), lw,
                                        tt=chunk_s, chunk=chunk_s, t_valid=t)
        ys = _outmlp(ys, att_s.reshape(bd * t, att_w), o_s[:, :t].reshape(bd * t, gdn_vw), lw,
                     tm=tm_s, ff_chunk=1024)
        outs_s.append((kf.reshape(bd, t, n_att_heads, head_dim), vf.reshape(bd, t, n_att_heads, head_dim),
                       ikf.reshape(bd, t, IDX_DIM), s_fin_s, conv_out_s[:, SUBLANES - hist:, :]))

    kp, vp, ikp, gp, cp = [jnp.stack([o_[i] for o_ in outs_p]) for i in range(5)]
    ks_, vs_, iks, gs, cs = [jnp.stack([o_[i] for o_ in outs_s]) for i in range(5)]
    return (yp.reshape(b, s, d), ys.reshape(bd, t, d), kp, vp, ikp, gp, cp, ks_, vs_, iks, gs, cs)
```

```python
import functools
import math

import jax
import jax.numpy as jnp
from jax import lax
from jax.experimental import pallas as pl
from jax.experimental.pallas import tpu as pltpu

F32 = jnp.float32
BF16 = jnp.bfloat16

HEAD_DIM = 128
IDX_DIM = 64
N_IDX_HEADS = 8
GDN_DK = 128
GDN_DV = 128
CONV_W = 4
TOPK_MAX = 256
GDN_CHUNK = 64
ROPE_THETA = 10000.0
NORM_EPS = 1e-6

LANES = 128
SUBLANES = 8
VMEM_LIMIT = 56 * 1024 * 1024
NEG_BIG = -1e30
BF16_EXACT_INT = 256
VALUE_STEPS = 28
MIN_NORMAL = 2.0 ** -126
SCORE_ROWS = 256
SM_IW = IDX_DIM
SM_GB = IDX_DIM + N_IDX_HEADS
HIGHEST = lax.Precision.HIGHEST


def _dot(a, b):
    return jnp.dot(a, b, preferred_element_type=F32)


def _dot_nt(a, b):
    return lax.dot_general(a, b, (((1,), (1,)), ((), ())), preferred_element_type=F32)


def _dot_tn(a, b):
    return lax.dot_general(a, b, (((0,), (0,)), ((), ())), preferred_element_type=F32)


def _rms(x, g):
    return x * lax.rsqrt(jnp.mean(x * x, axis=-1, keepdims=True) + NORM_EPS) * g


def _const_spec(shape):
    n = len(shape)
    return pl.BlockSpec(shape, lambda *_: (0,) * n, pipeline_mode=pl.Buffered(1))


def _lane_fold(x, op):
    r = x[:, 0:LANES]
    for j in range(1, x.shape[1] // LANES):
        r = op(r, x[:, j * LANES:(j + 1) * LANES])
    return r


def _inproj_kernel(x_ref, g_ref, wa_ref, wqkv_ref, wz_ref, ws_ref,
                   cos_ref, sin_ref, cosi_ref, sina_ref, sinb_ref, lng_ref, lnb_ref, smul_ref,
                   q_ref, kf_ref, vf_ref, kb_ref, vb_ref, vt_ref, iq_ref, ikf_ref, ikb_ref,
                   qkv_ref, z_ref, sm_ref, *, att_w, q_scale):
    x = x_ref[...]
    h = _rms(x, g_ref[...]).astype(BF16)
    a = _dot(h, wa_ref[...])
    cos = cos_ref[...]
    sin = sin_ref[...]
    n_heads = att_w // HEAD_DIM

    def rope_full(xh):
        return xh * cos + pltpu.roll(xh, HEAD_DIM // 2, axis=1) * sin

    for j in range(n_heads):
        sl = slice(j * HEAD_DIM, (j + 1) * HEAD_DIM)
        q_ref[:, sl] = (rope_full(a[:, sl]) * q_scale).astype(BF16)
        kr = rope_full(a[:, att_w + j * HEAD_DIM: att_w + (j + 1) * HEAD_DIM])
        kf_ref[pl.ds(j, x.shape[0], stride=n_heads), :] = kr
        kb_ref[:, sl] = kr.astype(BF16)
    v = a[:, 2 * att_w:3 * att_w]
    for j in range(n_heads):
        vf_ref[pl.ds(j, x.shape[0], stride=n_heads), :] = v[:, j * HEAD_DIM:(j + 1) * HEAD_DIM]
    vb_ref[...] = v.astype(BF16)
    vt_ref[0] = jnp.transpose(v).astype(BF16)

    cosi = cosi_ref[...]
    sina = sina_ref[...]
    sinb = sinb_ref[...]

    def rope_half(xh):
        return (xh * cosi + pltpu.roll(xh, LANES - IDX_DIM // 2, axis=1) * sina
                + pltpu.roll(xh, IDX_DIM // 2, axis=1) * sinb)

    idx_w = N_IDX_HEADS * IDX_DIM
    for j in range(idx_w // LANES):
        sl = slice(j * LANES, (j + 1) * LANES)
        iq_ref[:, sl] = rope_half(a[:, 3 * att_w + j * LANES: 3 * att_w + (j + 1) * LANES]).astype(BF16)

    qkv_ref[...] = _dot(h, wqkv_ref[...])
    z_ref[...] = _dot(h, wz_ref[...])

    sm = _dot(h, ws_ref[...])
    lane = lax.broadcasted_iota(jnp.int32, sm.shape, 1)
    is_ik = lane < IDX_DIM
    ikraw = jnp.where(is_ik, sm, 0.0)
    mu = jnp.sum(ikraw, axis=-1, keepdims=True) * (1.0 / IDX_DIM)
    xc = jnp.where(is_ik, sm - mu, 0.0)
    var = jnp.sum(xc * xc, axis=-1, keepdims=True) * (1.0 / IDX_DIM)
    ikn = xc * lax.rsqrt(var + NORM_EPS) * lng_ref[...] + lnb_ref[...]
    ikr = rope_half(ikn)
    ikf_ref[...] = ikr[:, :IDX_DIM]
    ikb_ref[...] = ikr[:, :IDX_DIM].astype(BF16)
    sm_ref[...] = sm * smul_ref[...]


def _rope_tables(pos):
    pos = pos.astype(F32)[:, None]
    half = HEAD_DIM // 2
    inv = jnp.power(ROPE_THETA, -jnp.arange(half, dtype=F32) / half)
    ang = pos * inv[None, :]
    c, s = jnp.cos(ang), jnp.sin(ang)
    cos = jnp.concatenate([c, c], axis=-1)
    sin = jnp.concatenate([-s, s], axis=-1)
    halfi = IDX_DIM // 2
    invi = jnp.power(ROPE_THETA, -jnp.arange(halfi, dtype=F32) / halfi)
    angi = pos * invi[None, :]
    ci, si = jnp.cos(angi), jnp.sin(angi)
    zi = jnp.zeros_like(si)
    cosi = jnp.concatenate([ci, ci, ci, ci], axis=-1)
    sina = jnp.concatenate([-si, zi, -si, zi], axis=-1)
    sinb = jnp.concatenate([zi, si, zi, si], axis=-1)
    return cos, sin, cosi, sina, sinb


def _inproj(x2d, tables, lw, *, att_w, conv_ch, gdn_vw, tm):
    n, d = x2d.shape
    idx_w = N_IDX_HEADS * IDX_DIM
    rt = tables[0].shape[0]
    assert n % tm == 0 and rt % tm == 0 and n % rt == 0
    nt = rt // tm
    n_heads = att_w // HEAD_DIM
    wa_w = 3 * att_w + idx_w
    row = lambda w: pl.BlockSpec((tm, w), lambda i: (i, 0))
    tab = pl.BlockSpec((tm, LANES), lambda i: (i % nt, 0))
    out_shapes = (
        jax.ShapeDtypeStruct((n, att_w), BF16),
        jax.ShapeDtypeStruct((n * n_heads, HEAD_DIM), F32),
        jax.ShapeDtypeStruct((n * n_heads, HEAD_DIM), F32),
        jax.ShapeDtypeStruct((n, att_w), BF16),
        jax.ShapeDtypeStruct((n, att_w), BF16),
        jax.ShapeDtypeStruct((n // rt, att_w, rt), BF16),
        jax.ShapeDtypeStruct((n, idx_w), BF16),
        jax.ShapeDtypeStruct((n, IDX_DIM), F32),
        jax.ShapeDtypeStruct((n, IDX_DIM), BF16),
        jax.ShapeDtypeStruct((n, conv_ch), F32),
        jax.ShapeDtypeStruct((n, gdn_vw), F32),
        jax.ShapeDtypeStruct((n, LANES), F32),
    )
    vt_spec = pl.BlockSpec((1, att_w, tm), lambda i: (i // nt, 0, i % nt))
    head_rows = pl.BlockSpec((tm * n_heads, HEAD_DIM), lambda i: (i, 0))
    out_specs = (row(att_w), head_rows, head_rows, row(att_w), row(att_w), vt_spec, row(idx_w),
                 row(IDX_DIM), row(IDX_DIM), row(conv_ch), row(gdn_vw), row(LANES))
    in_specs = [row(d), _const_spec((1, d)), _const_spec((d, wa_w)), _const_spec((d, conv_ch)),
                _const_spec((d, gdn_vw)), _const_spec((d, LANES)),
                tab, tab, tab, tab, tab,
                _const_spec((1, LANES)), _const_spec((1, LANES)), _const_spec((1, LANES))]
    return pl.pallas_call(
        functools.partial(_inproj_kernel, att_w=att_w, q_scale=HEAD_DIM ** -0.5),
        grid=(n // tm,),
        in_specs=in_specs, out_specs=out_specs, out_shape=out_shapes,
        compiler_params=pltpu.CompilerParams(dimension_semantics=("arbitrary",),
                                             vmem_limit_bytes=VMEM_LIMIT),
        name="inproj",
    )(x2d, lw["pre_mix_g"], lw["wa"], lw["wqkv"], lw["wz"], lw["ws"], *tables,
      lw["lng"], lw["lnb"], lw["smul"])


def _tie_break_by_index_search(count, rewrite, col_limit):
    def break_ties(lo, hi, want, tied):
        def in_tie(x):
            return jnp.logical_and(x >= lo, x < hi)

        def jstep(_, c):
            jlo, jhi = c
            jm = jnp.floor((jlo + jhi) * 0.5)
            cnt = count(lambda x, col: jnp.logical_and(in_tie(x), col <= jm))
            ok = cnt >= want
            return jnp.where(ok, jlo, jm), jnp.where(ok, jm, jhi)

        n_it = max(1, math.ceil(math.log2(col_limit + 1)))
        j0 = (jnp.full_like(lo, -1.0), jnp.full_like(lo, float(col_limit - 1)))
        _, jcut = lax.fori_loop(0, n_it, jstep, j0)
        rewrite(lambda x, col: jnp.where(
            jnp.logical_and(tied, jnp.logical_and(in_tie(x), col > jcut)), -jnp.inf, x))
    return break_ties


def _select_threshold(count, break_ties, n_valid, row_max, row_min, topk):
    kf = float(topk)
    flip = jnp.int32(0x7FFFFFFF)

    def to_key(x):
        b = lax.bitcast_convert_type(jnp.where(x == 0.0, 0.0, x), jnp.int32)
        return jnp.where(b < 0, b ^ flip, b)

    def from_key(k):
        min_normal = jnp.int32(0x00800000)
        k = jnp.where(jnp.logical_and(k > 0, k < min_normal), min_normal, k)
        return lax.bitcast_convert_type(jnp.where(k < 0, k ^ flip, k), F32)

    need = n_valid > kf
    spread = jnp.abs(row_max) * (2.0 ** -20) + 2.0 ** -100
    c_ge0 = count(lambda x, col: x >= 0.0)
    c_gt0 = count(lambda x, col: x > 0.0)
    non_neg = jnp.logical_and(need, c_ge0 >= kf)
    at_zero = jnp.logical_and(non_neg, c_gt0 < kf)
    negative = jnp.logical_and(need, c_ge0 < kf)
    lo0 = jnp.where(non_neg, 0.0, jnp.where(need, row_min, NEG_BIG))
    c_lo0 = jnp.where(non_neg, c_ge0, n_valid)
    hi0 = jnp.where(at_zero, MIN_NORMAL, jnp.where(negative, 0.0, row_max + spread))
    c_hi0 = jnp.where(at_zero, c_gt0, jnp.where(negative, c_ge0, 0.0))
    open0 = jnp.logical_and(need, jnp.logical_not(jnp.logical_or(at_zero, c_lo0 == kf)))
    done0 = jnp.where(open0, 0.0, 1.0)

    def narrow(c, mid, stuck, cnt):
        lo, hi, c_lo, c_hi, done = c
        ge = cnt >= kf
        upd = jnp.logical_and(done < 0.5, jnp.logical_not(stuck))
        up_lo = jnp.logical_and(upd, ge)
        up_hi = jnp.logical_and(upd, jnp.logical_not(ge))
        fin = jnp.logical_or(stuck, jnp.logical_and(upd, cnt == kf))
        return (jnp.where(up_lo, mid, lo), jnp.where(up_hi, mid, hi), jnp.where(up_lo, cnt, c_lo),
                jnp.where(up_hi, cnt, c_hi), jnp.where(fin, 1.0, done))

    def value_step(c):
        it, st = c
        lo, hi = st[0], st[1]
        mid = 0.5 * lo + 0.5 * hi
        stuck = jnp.logical_or(mid <= lo, mid >= hi)
        return it + 1, narrow(st, mid, stuck, count(lambda x, col: x >= mid))

    _, (lo, hi, c_lo, c_hi, done) = lax.while_loop(
        lambda c: jnp.logical_and(c[0] < VALUE_STEPS, jnp.min(c[1][4]) < 0.5), value_step,
        (jnp.int32(0), (lo0, hi0, c_lo0, c_hi0, done0)))

    def key_step(st):
        klo, khi = st[0], st[1]
        mid = (klo & khi) + ((klo ^ khi) >> 1)
        return narrow(st, mid, mid <= klo, count(lambda x, col: to_key(x) >= mid))

    klo, khi, c_lo, c_hi, _ = lax.while_loop(
        lambda st: jnp.min(st[4]) < 0.5, key_step, (to_key(lo), to_key(hi), c_lo, c_hi, done))
    lo, hi = from_key(klo), from_key(khi)

    tied = jnp.logical_and(need, c_lo > kf)

    @pl.when(jnp.max(jnp.where(tied, 1.0, 0.0)) > 0.5)
    def _():
        break_ties(lo, hi, kf - c_hi, tied)

    return lo


def _pattn_kernel(q_ref, iq_ref, sm_ref, k_ref, vt_ref, ik_ref, o_ref, sc_ref, m_ref, l_ref, acc_ref,
                  *, tq, tk, topk, n_heads):
    qi = pl.program_id(1)
    t0 = qi * tq
    nkb = (t0 + tq + tk - 1) // tk
    sub_tiles = tk // SUBLANES
    q_t = t0 + lax.broadcasted_iota(jnp.int32, (1, tq), 1)
    smt = jnp.transpose(sm_ref[0])
    w_rows = [smt[SM_IW + h:SM_IW + h + 1, :] for h in range(N_IDX_HEADS)]
    iq = iq_ref[0]
    iq_heads = [iq[:, h * IDX_DIM:(h + 1) * IDX_DIM] for h in range(N_IDX_HEADS)]
    sub_iota = lax.broadcasted_iota(jnp.int32, (SUBLANES, tq), 0)

    def sub_fold(x, op):
        r = x[0:SUBLANES]
        for j in range(1, x.shape[0] // SUBLANES):
            r = op(r, x[j * SUBLANES:(j + 1) * SUBLANES])
        return r

    def rep(row):
        return jnp.broadcast_to(row, (SUBLANES, tq))

    def score_block(kb, carry):
        mx, mn = carry
        k0 = pl.multiple_of(kb * tk, tk)
        for c in range(tk // SCORE_ROWS):
            r0 = k0 + c * SCORE_ROWS
            ikc = ik_ref[0, pl.ds(r0, SCORE_ROWS), :]
            acc = jnp.zeros((SCORE_ROWS, tq), F32)
            for h in range(N_IDX_HEADS):
                acc = acc + w_rows[h] * jnp.maximum(_dot_nt(ikc, iq_heads[h]), 0.0)
            key = r0 + lax.broadcasted_iota(jnp.int32, (SCORE_ROWS, 1), 0)
            valid = key <= q_t
            sc_ref[pl.ds(r0, SCORE_ROWS), :] = jnp.where(valid, acc, -jnp.inf)
            mx = jnp.maximum(mx, sub_fold(jnp.where(valid, acc, -jnp.inf), jnp.maximum))
            mn = jnp.minimum(mn, sub_fold(jnp.where(valid, acc, jnp.inf), jnp.minimum))
        return mx, mn

    mx, mn = lax.fori_loop(0, nkb, score_block,
                           (jnp.full((SUBLANES, tq), -jnp.inf, F32), jnp.full((SUBLANES, tq), jnp.inf, F32)))
    row_max = rep(jnp.max(mx, axis=0, keepdims=True))
    row_min = rep(jnp.min(mn, axis=0, keepdims=True))
    n_valid = rep((q_t + 1).astype(F32))

    def count(pred):
        def body(kb, acc):
            k0 = pl.multiple_of(kb * tk, tk)
            x = sc_ref[pl.ds(k0, tk), :]
            for j in range(sub_tiles):
                hit = pred(x[j * SUBLANES:(j + 1) * SUBLANES], (k0 + j * SUBLANES + sub_iota).astype(F32))
                acc = acc + jnp.where(hit, 1.0, 0.0)
            return acc
        acc = lax.fori_loop(0, nkb, body, jnp.zeros((SUBLANES, tq), F32))
        return rep(jnp.sum(acc, axis=0, keepdims=True))

    def break_ties(lo, hi, want, tied):
        def body(kb, seen):
            k0 = pl.multiple_of(kb * tk, tk)
            x = sc_ref[pl.ds(k0, tk), :]
            out = []
            for j in range(sub_tiles):
                xs = x[j * SUBLANES:(j + 1) * SUBLANES]
                t = jnp.where(tied, jnp.where(xs >= lo, jnp.where(xs < hi, 1.0, 0.0), 0.0), 0.0)
                upto = t
                for sh in (1, 2, 4):
                    upto = upto + jnp.where(sub_iota >= sh, pltpu.roll(upto, sh, axis=0), 0.0)
                out.append(jnp.where(t > 0.5, jnp.where(seen + upto <= want, xs, -jnp.inf), xs))
                seen = seen + rep(upto[SUBLANES - 1:SUBLANES, :])
            sc_ref[pl.ds(k0, tk), :] = jnp.concatenate(out, axis=0)
            return seen
        lax.fori_loop(0, nkb, body, jnp.zeros((SUBLANES, tq), F32))

    lo = _select_threshold(count, break_ties, n_valid, row_max, row_min, topk)
    tile_rows = lambda v8: jnp.concatenate([v8] * sub_tiles, axis=0)
    lo_t = tile_rows(lo)

    def masked_scores(kb, h, bias):
        k0 = pl.multiple_of(kb * tk, tk)
        sl = slice(h * HEAD_DIM, (h + 1) * HEAD_DIM)
        return _dot_nt(k_ref[0, pl.ds(k0, tk), sl], q_ref[0, :, sl]) + bias

    m_ref[...] = jnp.full(m_ref.shape, NEG_BIG, F32)
    l_ref[...] = jnp.zeros_like(l_ref)
    acc_ref[...] = jnp.zeros_like(acc_ref)

    def pv_block(kb, carry):
        k0 = pl.multiple_of(kb * tk, tk)
        bias = jnp.where(sc_ref[pl.ds(k0, tk), :] >= lo_t, 0.0, NEG_BIG)
        ss = [masked_scores(kb, h, bias) for h in range(n_heads)]
        hss = [slice(h * SUBLANES, (h + 1) * SUBLANES) for h in range(n_heads)]
        m_old = [m_ref[hs, :] for hs in hss]
        m_new = [jnp.maximum(m_old[h], rep(jnp.max(sub_fold(ss[h], jnp.maximum), axis=0, keepdims=True)))
                 for h in range(n_heads)]
        ps = [jnp.exp(ss[h] - tile_rows(m_new[h])) for h in range(n_heads)]
        for h in range(n_heads):
            sl = slice(h * HEAD_DIM, (h + 1) * HEAD_DIM)
            alpha = jnp.exp(m_old[h] - m_new[h])
            m_ref[hss[h], :] = m_new[h]
            l_ref[hss[h], :] = alpha * l_ref[hss[h], :] + sub_fold(ps[h], jnp.add)
            acc_ref[sl, :] = (alpha[0:1, :] * acc_ref[sl, :]
                              + _dot(vt_ref[0, sl, pl.ds(k0, tk)], ps[h].astype(BF16)))
        return carry
    lax.fori_loop(0, nkb, pv_block, 0)

    for h in range(n_heads):
        sl = slice(h * HEAD_DIM, (h + 1) * HEAD_DIM)
        l_row = jnp.sum(l_ref[h * SUBLANES:(h + 1) * SUBLANES, :], axis=0, keepdims=True)
        o_ref[0, :, sl] = jnp.transpose(acc_ref[sl, :] / l_row).astype(o_ref.dtype)


def _prompt_attention(q, iq, sm, kb, vt, ikb, *, tq, tk, topk):
    b, s, att_w = q.shape
    idx_w = iq.shape[-1]
    n_heads = att_w // HEAD_DIM
    assert s % tq == 0 and s % tk == 0 and tk % tq == 0 and tq % LANES == 0
    blk = lambda w: pl.BlockSpec((1, tq, w), lambda bi, qi: (bi, qi, 0))
    full = lambda r, w: pl.BlockSpec((1, r, w), lambda bi, qi: (bi, 0, 0), pipeline_mode=pl.Buffered(1))
    return pl.pallas_call(
        functools.partial(_pattn_kernel, tq=tq, tk=tk, topk=topk, n_heads=n_heads),
        grid=(b, s // tq),
        in_specs=[blk(att_w), blk(idx_w), blk(LANES), full(s, att_w), full(att_w, s), full(s, IDX_DIM)],
        out_specs=blk(att_w),
        out_shape=jax.ShapeDtypeStruct((b, s, att_w), BF16),
        scratch_shapes=[pltpu.VMEM((s, tq), F32), pltpu.VMEM((n_heads * SUBLANES, tq), F32),
                        pltpu.VMEM((n_heads * SUBLANES, tq), F32), pltpu.VMEM((att_w, tq), F32)],
        compiler_params=pltpu.CompilerParams(dimension_semantics=("arbitrary", "arbitrary"),
                                             vmem_limit_bytes=VMEM_LIMIT),
        name="prompt_attention",
    )(q, iq, sm, kb, vt, ikb)


def _dsel_kernel(pt_ref, iq_ref, w_ref, iknew_ref, *rest, sb, pg, n_tok, past, topk):
    pages = rest[:sb * pg]
    sel_ref = rest[sb * pg]
    sc_ref = rest[sb * pg + 1]
    c = pl.program_id(1)
    n_c = pl.num_programs(1)
    page = pages[0].shape[2]
    rows, l_pad = sc_ref.shape

    def token_rows(s):
        return [jnp.sum(s[t * N_IDX_HEADS:(t + 1) * N_IDX_HEADS, :], axis=0, keepdims=True)
                for t in range(n_tok)]

    @pl.when(c == 0)
    def _():
        sc_ref[...] = jnp.full(sc_ref.shape, -jnp.inf, F32)

    for j in range(sb):
        iq = iq_ref[j]
        w = w_ref[j]
        kt = jnp.concatenate([pages[j * pg + p][0] for p in range(pg)], axis=1).astype(BF16)
        tr = token_rows(jnp.maximum(_dot(iq, kt), 0.0) * w)
        k0 = pl.multiple_of(c * pg * page, pg * page)
        for t in range(n_tok):
            sc_ref[j * n_tok + t:j * n_tok + t + 1, pl.ds(k0, pg * page)] = tr[t]

    @pl.when(c == n_c - 1)
    def _():
        col = lax.broadcasted_iota(jnp.int32, (1, LANES), 1)
        for j in range(sb):
            tr = token_rows(jnp.maximum(_dot_nt(iq_ref[j], iknew_ref[j]), 0.0) * w_ref[j])
            for t in range(n_tok):
                sc_ref[j * n_tok + t:j * n_tok + t + 1, past:past + LANES] = jnp.where(col <= t, tr[t], -jnp.inf)
        x = sc_ref[...]
        rep = lambda v: jnp.broadcast_to(v, (rows, LANES))
        row_max = rep(jnp.max(x, axis=-1, keepdims=True))
        row_min = rep(jnp.min(jnp.where(x > -jnp.inf, x, jnp.inf), axis=-1, keepdims=True))
        r = lax.broadcasted_iota(jnp.int32, (rows, LANES), 0).astype(F32)
        tok = r - n_tok * jnp.floor(r * (1.0 / n_tok))
        n_valid = past + 1.0 + tok
        n_blocks = l_pad // LANES
        assert n_blocks <= BF16_EXACT_INT
        ones_b = jnp.ones((LANES, LANES), BF16)
        lane_iota = lax.broadcasted_iota(jnp.int32, (rows, LANES), 1)

        def count(pred):
            def body(kb, acc):
                c0 = pl.multiple_of(kb * LANES, LANES)
                hit = pred(sc_ref[:, pl.ds(c0, LANES)], (c0 + lane_iota).astype(F32))
                return acc + jnp.where(hit, 1.0, 0.0)
            acc = lax.fori_loop(0, n_blocks, body, jnp.zeros((rows, LANES), F32))
            return _dot(acc.astype(BF16), ones_b)

        def rewrite(fn):
            def body(kb, carry):
                c0 = pl.multiple_of(kb * LANES, LANES)
                sc_ref[:, pl.ds(c0, LANES)] = fn(sc_ref[:, pl.ds(c0, LANES)], (c0 + lane_iota).astype(F32))
                return carry
            lax.fori_loop(0, n_blocks, body, 0)

        lo = _select_threshold(count, _tie_break_by_index_search(count, rewrite, l_pad),
                               n_valid, row_max, row_min, topk)

        def emit(kb, carry):
            c0 = pl.multiple_of(kb * LANES, LANES)
            x = jnp.where(sc_ref[:, pl.ds(c0, LANES)] >= lo, 1.0, 0.0)
            for j in range(sb):
                sel_ref[j, 0:n_tok, pl.ds(c0, LANES)] = x[j * n_tok:(j + 1) * n_tok]
                sel_ref[j, n_tok:SUBLANES, pl.ds(c0, LANES)] = jnp.zeros((SUBLANES - n_tok, LANES), F32)
            return carry
        lax.fori_loop(0, l_pad // LANES, emit, 0)


def _decode_select(page_table, iq, w, iknew, cache_idx_t, *, sb, pg, n_tok, topk):
    bd, n_pages = page_table.shape
    _, di, page = cache_idx_t.shape
    past = n_pages * page
    assert n_pages % pg == 0 and bd % sb == 0 and (sb * n_tok) % SUBLANES == 0
    l_pad = past + LANES
    rows = iq.shape[1]

    def page_spec(j, p):
        return pl.BlockSpec((1, di, page), lambda g, c, pt: (pt[g * sb + j, c * pg + p], 0, 0))

    grid_spec = pltpu.PrefetchScalarGridSpec(
        num_scalar_prefetch=1,
        grid=(bd // sb, n_pages // pg),
        in_specs=[pl.BlockSpec((sb, rows, di), lambda g, c, pt: (g, 0, 0)),
                  pl.BlockSpec((sb, rows, 1), lambda g, c, pt: (g, 0, 0)),
                  pl.BlockSpec((sb, LANES, di), lambda g, c, pt: (g, 0, 0))]
                 + [page_spec(j, p) for j in range(sb) for p in range(pg)],
        out_specs=pl.BlockSpec((sb, SUBLANES, l_pad), lambda g, c, pt: (g, 0, 0)),
        scratch_shapes=[pltpu.VMEM((sb * n_tok, l_pad), F32)],
    )
    return pl.pallas_call(
        functools.partial(_dsel_kernel, sb=sb, pg=pg, n_tok=n_tok, past=past, topk=topk),
        grid_spec=grid_spec,
        out_shape=jax.ShapeDtypeStruct((bd, SUBLANES, l_pad), F32),
        compiler_params=pltpu.CompilerParams(dimension_semantics=("arbitrary", "arbitrary"),
                                             vmem_limit_bytes=VMEM_LIMIT),
        name="decode_select",
    )(page_table, iq, w, iknew, *([cache_idx_t] * (sb * pg)))


def _dattn_kernel(pt_ref, q_ref, sel_ref, knew_ref, vnew_ref, *rest, pg, n_tok, n_heads, past):
    kpages = rest[:pg]
    vpages = rest[pg:2 * pg]
    o_ref = rest[2 * pg]
    m_ref, l_ref, acc_ref = rest[2 * pg + 1:]
    c = pl.program_id(1)
    n_c = pl.num_programs(1)
    page = kpages[0].shape[1] // n_heads
    qh = [q_ref[0, h * SUBLANES:(h + 1) * SUBLANES, :] for h in range(n_heads)]

    @pl.when(c == 0)
    def _():
        m_ref[...] = jnp.full(m_ref.shape, NEG_BIG, F32)
        l_ref[...] = jnp.zeros_like(l_ref)
        acc_ref[...] = jnp.zeros_like(acc_ref)

    def update(k_of, v_of, n_blk, sel):
        s = jnp.concatenate(
            [jnp.concatenate([_dot_nt(qh[h], k_of(p, h)) for p in range(n_blk)], axis=1)
             for h in range(n_heads)], axis=0)
        s = jnp.where(jnp.concatenate([sel] * n_heads, axis=0) > 0.5, s, NEG_BIG)
        m_old = m_ref[...]
        m_new = jnp.maximum(m_old, jnp.max(s, axis=-1, keepdims=True))
        pr = jnp.exp(s - m_new).astype(BF16)
        alpha = jnp.exp(m_old - m_new)
        l_ref[...] = alpha * l_ref[...] + jnp.sum(pr.astype(F32), axis=-1, keepdims=True)
        pv = []
        for h in range(n_heads):
            a = jnp.zeros((SUBLANES, HEAD_DIM), F32)
            for p in range(n_blk):
                a = a + _dot(pr[h * SUBLANES:(h + 1) * SUBLANES, p * page:(p + 1) * page], v_of(p, h))
            pv.append(a)
        acc_ref[...] = alpha * acc_ref[...] + jnp.concatenate(pv, axis=0)
        m_ref[...] = m_new

    head_rows = lambda ref, h: ref[0, pl.ds(h, page, stride=n_heads), :].astype(BF16)
    k0 = pl.multiple_of(c * pg * page, pg * page)
    update(lambda p, h: head_rows(kpages[p], h), lambda p, h: head_rows(vpages[p], h), pg,
           sel_ref[0, :, pl.ds(k0, pg * page)])

    @pl.when(c == n_c - 1)
    def _():
        new = lambda ref, h: ref[0, :, h * HEAD_DIM:(h + 1) * HEAD_DIM]
        update(lambda p, h: new(knew_ref, h), lambda p, h: new(vnew_ref, h), 1,
               sel_ref[0, :, past:past + LANES])
        inv_l = 1.0 / l_ref[...]
        out = acc_ref[...] * inv_l
        for h in range(n_heads):
            o_ref[0, :, h * HEAD_DIM:(h + 1) * HEAD_DIM] = out[h * SUBLANES:h * SUBLANES + n_tok, :].astype(o_ref.dtype)


def _decode_attention(page_table, q8, sel, knew, vnew, cache_k, cache_v, *, pg, n_tok, n_heads):
    bd, n_pages = page_table.shape
    _, page_rows, hd = cache_k.shape
    page = page_rows // n_heads
    past = n_pages * page
    l_pad = sel.shape[-1]
    att_w = n_heads * hd
    rows = n_heads * SUBLANES
    assert n_pages % pg == 0 and page == LANES

    def page_spec(p):
        return pl.BlockSpec((1, page_rows, hd), lambda b, c, pt: (pt[b, c * pg + p], 0, 0))

    per_seq = lambda r, w: pl.BlockSpec((1, r, w), lambda b, c, pt: (b, 0, 0))
    grid_spec = pltpu.PrefetchScalarGridSpec(
        num_scalar_prefetch=1,
        grid=(bd, n_pages // pg),
        in_specs=[per_seq(rows, hd), per_seq(SUBLANES, l_pad), per_seq(LANES, att_w), per_seq(LANES, att_w)]
                 + [page_spec(p) for p in range(pg)] * 2,
        out_specs=per_seq(n_tok, att_w),
        scratch_shapes=[pltpu.VMEM((rows, 1), F32), pltpu.VMEM((rows, 1), F32), pltpu.VMEM((rows, hd), F32)],
    )
    return pl.pallas_call(
        functools.partial(_dattn_kernel, pg=pg, n_tok=n_tok, n_heads=n_heads, past=past),
        grid_spec=grid_spec,
        out_shape=jax.ShapeDtypeStruct((bd, n_tok, att_w), BF16),
        compiler_params=pltpu.CompilerParams(dimension_semantics=("arbitrary", "arbitrary"),
                                             vmem_limit_bytes=VMEM_LIMIT),
        name="decode_attention",
    )(page_table, q8, sel, knew, vnew, *([cache_k] * pg), *([cache_v] * pg))


def _gdn_kernel(x_ref, z_ref, sm_ref, conv0_ref, s0_ref, cw_ref, alog_ref, dtb_ref, ng_ref,
                o_ref, sfin_ref, convout_ref,
                xbuf, st_ref, q_s, k_s, v_s, beta_s, g_s,
                *, tt, chunk, n_heads, t_valid, n_steps):
    step = pl.program_id(1)
    hist = CONV_W - 1
    qk_w = n_heads * GDN_DK

    @pl.when(step == 0)
    def _():
        xbuf[0:SUBLANES, :] = conv0_ref[0]
        st_ref[...] = s0_ref[0]

    @pl.when(step > 0)
    def _():
        xbuf[0:SUBLANES, :] = xbuf[tt:tt + SUBLANES, :]

    xbuf[SUBLANES:SUBLANES + tt, :] = x_ref[0]
    cw = cw_ref[...]
    conv = cw[hist:hist + 1, :] * xbuf[SUBLANES:SUBLANES + tt, :]
    for j in range(hist):
        conv = conv + cw[j:j + 1, :] * xbuf[SUBLANES - hist + j:SUBLANES - hist + j + tt, :]
    act = conv * jax.nn.sigmoid(conv)

    def l2n(xh):
        return xh * lax.rsqrt(jnp.sum(xh * xh, axis=-1, keepdims=True) + NORM_EPS)

    for h in range(n_heads):
        sl = slice(h * GDN_DK, (h + 1) * GDN_DK)
        q_s[:, sl] = l2n(act[:, sl]) * (GDN_DK ** -0.5)
        k_s[:, sl] = l2n(act[:, qk_w + h * GDN_DK: qk_w + (h + 1) * GDN_DK])
    v_s[...] = act[:, 2 * qk_w:]

    smv = sm_ref[0]
    t_glob = step * tt + lax.broadcasted_iota(jnp.int32, (tt, 1), 0)
    live = t_glob < t_valid
    beta_s[...] = jnp.where(live, jax.nn.sigmoid(smv), 0.0)
    g_s[...] = jnp.where(live, -jnp.exp(alog_ref[...]) * jax.nn.softplus(smv + dtb_ref[...]), 0.0)

    ii = lax.broadcasted_iota(jnp.int32, (chunk, chunk), 0)
    jj = lax.broadcasted_iota(jnp.int32, (chunk, chunk), 1)
    tril = ii >= jj
    strict = ii > jj
    tril_f = jnp.where(tril, 1.0, 0.0)
    eye = jnp.where(ii == jj, 1.0, 0.0)
    lane8 = lax.broadcasted_iota(jnp.int32, (SUBLANES, LANES), 1)
    row8 = lax.broadcasted_iota(jnp.int32, (SUBLANES, LANES), 0)
    g_lane0 = SM_GB + n_heads
    pick_g = jnp.where(lane8 == g_lane0 + row8, 1.0, 0.0)
    n_levels = int(math.log2(chunk))
    ng = ng_ref[...]

    n_chunks = tt // chunk
    units = [(ci, h) for ci in range(n_chunks) for h in range(n_heads)]
    rows_of = lambda ci: slice(ci * chunk, (ci + 1) * chunk)
    lanes_of = lambda h: slice(h * GDN_DK, (h + 1) * GDN_DK)
    gcs = [jnp.dot(tril_f, g_s[rows_of(ci), :], precision=HIGHEST, preferred_element_type=F32)
           for ci in range(n_chunks)]
    gcrs = [lax.dot_general(pick_g, gc, (((1,), (1,)), ((), ())), precision=HIGHEST,
                            preferred_element_type=F32) for gc in gcs]
    gcol = {(ci, h): gcs[ci][:, g_lane0 + h:g_lane0 + h + 1] for ci, h in units}
    bcol = {(ci, h): beta_s[rows_of(ci), SM_GB + h:SM_GB + h + 1] for ci, h in units}
    decay = {u: jnp.exp(jnp.where(tril, gcol[u] - gcrs[u[0]][u[1]:u[1] + 1, :], -jnp.inf)) for u in units}
    k_b = {(ci, h): k_s[rows_of(ci), lanes_of(h)].astype(BF16) for ci, h in units}
    kb_f = {(ci, h): k_s[rows_of(ci), lanes_of(h)] * bcol[(ci, h)] for ci, h in units}
    lmat = {u: jnp.where(strict, _dot_nt(kb_f[u].astype(BF16), k_b[u]) * decay[u], 0.0) for u in units}
    qk = {(ci, h): (_dot_nt(q_s[rows_of(ci), lanes_of(h)].astype(BF16), k_b[(ci, h)]) * decay[(ci, h)]).astype(BF16)
          for ci, h in units}
    pw = {u: -lmat[u] for u in units}
    inv = {u: eye + pw[u] for u in units}
    for _ in range(n_levels - 1):
        pw = {u: _dot(pw[u].astype(BF16), pw[u].astype(BF16)) for u in units}
        inv = {u: inv[u] + _dot(inv[u].astype(BF16), pw[u].astype(BF16)) for u in units}
    inv_b = {u: inv[u].astype(BF16) for u in units}
    u_rhs = {(ci, h): _dot(inv_b[(ci, h)], (v_s[rows_of(ci), lanes_of(h)] * bcol[(ci, h)]).astype(BF16))
             for ci, h in units}
    w_rhs = {u: _dot(inv_b[u], (kb_f[u] * jnp.exp(gcol[u])).astype(BF16)).astype(BF16) for u in units}

    state = [st_ref[h] for h in range(n_heads)]
    for ci in range(n_chunks):
        heads = [(ci, h) for h in range(n_heads)]
        s_b = [s.astype(BF16) for s in state]
        ws = [_dot(w_rhs[u], s_b[u[1]]) for u in heads]
        qs = [_dot((q_s[rows_of(ci), lanes_of(h)] * jnp.exp(gcol[(ci, h)])).astype(BF16), s_b[h]) for _, h in heads]
        v_new = [(u_rhs[u] - ws[u[1]]).astype(BF16) for u in heads]
        outs = [qs[h] + _dot(qk[(ci, h)], v_new[h]) for _, h in heads]
        g_last = [gcs[ci][chunk - 1:chunk, g_lane0 + h:g_lane0 + h + 1] for _, h in heads]
        k_dec = [(k_s[rows_of(ci), lanes_of(h)] * jnp.exp(g_last[h] - gcol[(ci, h)])).astype(BF16) for _, h in heads]
        state = [state[h] * jnp.exp(g_last[h]) + _dot_tn(k_dec[h], v_new[h]) for _, h in heads]
        for _, h in heads:
            zh = z_ref[0, rows_of(ci), lanes_of(h)]
            o_ref[0, rows_of(ci), lanes_of(h)] = (_rms(outs[h], ng) * (zh * jax.nn.sigmoid(zh))).astype(o_ref.dtype)
    for h in range(n_heads):
        st_ref[h] = state[h]

    @pl.when(step == n_steps - 1)
    def _():
        sfin_ref[0] = st_ref[...]
        last = t_valid - (n_steps - 1) * tt
        convout_ref[0] = xbuf[last:last + SUBLANES, :]


def _gdn(x, z, sm, conv0, s0, lw, *, tt, chunk, t_valid):
    nb, t_pad, conv_ch = x.shape
    n_heads = s0.shape[1]
    vw = z.shape[-1]
    assert t_pad % tt == 0 and tt % chunk == 0
    n_steps = t_pad // tt
    assert 0 < t_valid - (n_steps - 1) * tt <= tt and t_valid >= CONV_W - 1
    tok = lambda w: pl.BlockSpec((1, tt, w), lambda b, s: (b, s, 0))
    return pl.pallas_call(
        functools.partial(_gdn_kernel, tt=tt, chunk=chunk, n_heads=n_heads, t_valid=t_valid, n_steps=n_steps),
        grid=(nb, n_steps),
        in_specs=[tok(conv_ch), tok(vw), tok(LANES),
                  pl.BlockSpec((1, SUBLANES, conv_ch), lambda b, s: (b, 0, 0)),
                  pl.BlockSpec((1, n_heads, GDN_DK, GDN_DV), lambda b, s: (b, 0, 0, 0)),
                  pl.BlockSpec((SUBLANES, conv_ch), lambda b, s: (0, 0)),
                  pl.BlockSpec((1, LANES), lambda b, s: (0, 0)),
                  pl.BlockSpec((1, LANES), lambda b, s: (0, 0)),
                  pl.BlockSpec((1, GDN_DV), lambda b, s: (0, 0))],
        out_specs=(tok(vw),
                   pl.BlockSpec((1, n_heads, GDN_DK, GDN_DV), lambda b, s: (b, 0, 0, 0)),
                   pl.BlockSpec((1, SUBLANES, conv_ch), lambda b, s: (b, 0, 0))),
        out_shape=(jax.ShapeDtypeStruct((nb, t_pad, vw), BF16),
                   jax.ShapeDtypeStruct((nb, n_heads, GDN_DK, GDN_DV), F32),
                   jax.ShapeDtypeStruct((nb, SUBLANES, conv_ch), F32)),
        scratch_shapes=[pltpu.VMEM((tt + 2 * SUBLANES, conv_ch), F32),
                        pltpu.VMEM((n_heads, GDN_DK, GDN_DV), F32),
                        pltpu.VMEM((tt, n_heads * GDN_DK), F32),
                        pltpu.VMEM((tt, n_heads * GDN_DK), F32),
                        pltpu.VMEM((tt, vw), F32),
                        pltpu.VMEM((tt, LANES), F32),
                        pltpu.VMEM((tt, LANES), F32)],
        compiler_params=pltpu.CompilerParams(dimension_semantics=("arbitrary", "arbitrary"),
                                             vmem_limit_bytes=VMEM_LIMIT),
        name="gated_delta",
    )(x, z, sm, conv0, s0, lw["conv_w"], lw["alog"], lw["dtb"], lw["gdn_norm_g"])


def _outmlp_kernel(x_ref, att_ref, o_ref, woa_ref, wob_ref, g1_ref, g2_ref, g3_ref, wup_ref, wdn_ref,
                   y_ref, *, ff_chunk):
    x = x_ref[...]
    mix = _dot(att_ref[...], woa_ref[...]) + _dot(o_ref[...], wob_ref[...])
    x1 = x + _rms(mix, g1_ref[...])
    h2 = _rms(x1, g2_ref[...]).astype(BF16)
    d_ff = wup_ref.shape[1]
    ff = jnp.zeros(x.shape, F32)
    for c in range(d_ff // ff_chunk):
        u = jnp.maximum(_dot(h2, wup_ref[:, c * ff_chunk:(c + 1) * ff_chunk]), 0.0)
        ff = ff + _dot((u * u).astype(BF16), wdn_ref[c * ff_chunk:(c + 1) * ff_chunk, :])
    y_ref[...] = x1 + _rms(ff, g3_ref[...])


def _outmlp(x2d, att, o, lw, *, tm, ff_chunk):
    n, d = x2d.shape
    att_w = att.shape[1]
    vw = o.shape[1]
    d_ff = lw["wup"].shape[1]
    assert n % tm == 0 and d_ff % ff_chunk == 0
    row = lambda w: pl.BlockSpec((tm, w), lambda i: (i, 0))
    return pl.pallas_call(
        functools.partial(_outmlp_kernel, ff_chunk=ff_chunk),
        grid=(n // tm,),
        in_specs=[row(d), row(att_w), row(vw), _const_spec((att_w, d)), _const_spec((vw, d)),
                  _const_spec((1, d)), _const_spec((1, d)), _const_spec((1, d)),
                  _const_spec((d, d_ff)), _const_spec((d_ff, d))],
        out_specs=row(d),
        out_shape=jax.ShapeDtypeStruct((n, d), F32),
        compiler_params=pltpu.CompilerParams(dimension_semantics=("arbitrary",),
                                             vmem_limit_bytes=VMEM_LIMIT),
        name="outproj_mlp",
    )(x2d, att, o, lw["woa"], lw["wob"], lw["post_mix_g"], lw["pre_mlp_g"], lw["post_mlp_g"],
      lw["wup"], lw["wdn"])


def _layer_weights(l, att_w, conv_ch, gdn_vw, n_gdn_heads, w_in, conv_w, idx_k_norm_g, idx_k_norm_b,
                   gdn_a_log, gdn_dt_bias, gdn_norm_g, w_out, pre_mix_g, post_mix_g, pre_mlp_g,
                   post_mlp_g, w_mlp_up, w_mlp_down):
    idx_w = N_IDX_HEADS * IDX_DIM
    o_ik = 3 * att_w + idx_w
    o_iw = o_ik + IDX_DIM
    o_qkv = o_iw + N_IDX_HEADS
    o_z = o_qkv + conv_ch
    o_gb = o_z + gdn_vw
    o_ga = o_gb + n_gdn_heads
    wi = w_in[l]
    assert wi.shape[1] == o_ga + n_gdn_heads
    d = wi.shape[0]
    n_small = IDX_DIM + N_IDX_HEADS + 2 * n_gdn_heads
    ws = jnp.concatenate([wi[:, o_ik:o_qkv], wi[:, o_gb:], jnp.zeros((d, LANES - n_small), wi.dtype)], axis=1)
    pad_lanes = lambda v, at: jnp.zeros((1, LANES), F32).at[0, at:at + v.shape[0]].set(v.astype(F32))
    smul = jnp.ones((1, LANES), F32).at[0, SM_IW:SM_IW + N_IDX_HEADS].set(N_IDX_HEADS ** -0.5 * IDX_DIM ** -0.5)
    g_lane0 = SM_GB + n_gdn_heads
    vec = lambda v: v[l].astype(F32)[None, :]
    return {
        "wa": wi[:, :o_ik].astype(BF16),
        "wqkv": wi[:, o_qkv:o_z].astype(BF16),
        "wz": wi[:, o_z:o_gb].astype(BF16),
        "ws": ws.astype(BF16),
        "lng": pad_lanes(idx_k_norm_g[l], 0),
        "lnb": pad_lanes(idx_k_norm_b[l], 0),
        "smul": smul,
        "conv_w": jnp.zeros((SUBLANES, conv_ch), F32).at[:CONV_W].set(conv_w[l].astype(F32)),
        "alog": pad_lanes(gdn_a_log[l], g_lane0),
        "dtb": pad_lanes(gdn_dt_bias[l], g_lane0),
        "gdn_norm_g": vec(gdn_norm_g),
        "woa": w_out[l, :att_w].astype(BF16),
        "wob": w_out[l, att_w:].astype(BF16),
        "pre_mix_g": vec(pre_mix_g), "post_mix_g": vec(post_mix_g),
        "pre_mlp_g": vec(pre_mlp_g), "post_mlp_g": vec(post_mlp_g),
        "wup": w_mlp_up[l].astype(BF16),
        "wdn": w_mlp_down[l].astype(BF16),
    }


def _pick_tile(n, prefs):
    for t in prefs:
        if n % t == 0:
            return t
    return n


def kernel(x_prompt, x_sample, cache_k, cache_v, cache_idx_k, page_table, state_gdn, state_conv, w_in, conv_w, idx_k_norm_g, idx_k_norm_b, gdn_a_log, gdn_dt_bias, gdn_norm_g, w_out, pre_mix_g, post_mix_g, pre_mlp_g, post_mlp_g, w_mlp_up, w_mlp_down):
    b, s, d = x_prompt.shape
    bd, t, _ = x_sample.shape
    depth, n_pool, page, n_att_heads, head_dim = cache_k.shape
    n_pages = page_table.shape[1]
    past = n_pages * page
    n_gdn_heads, dk, dv = state_gdn.shape[2:]
    conv_ch = state_conv.shape[-1]
    att_w = n_att_heads * head_dim
    gdn_vw = n_gdn_heads * dv
    assert head_dim == HEAD_DIM and dk == GDN_DK and dv == GDN_DV and cache_idx_k.shape[-1] == IDX_DIM
    assert conv_ch == 2 * n_gdn_heads * dk + gdn_vw and state_conv.shape[2] == CONV_W - 1
    assert t <= SUBLANES and page == LANES

    tab_p = _rope_tables(jnp.arange(s))
    tab_s = tuple(jnp.tile(tb, (bd, 1)) for tb in _rope_tables(past + jnp.arange(t)))
    topk_p = min(TOPK_MAX, s // 4)
    topk_s = min(TOPK_MAX, (past + t) // 4)
    hist = CONV_W - 1

    tm_p = _pick_tile(b * s, (256, 128, 64, 32, 16, 8))
    tm_s = _pick_tile(bd * t, (256, 128, 64, 32, 16, 8))
    tq = _pick_tile(s, (512, 256, 128))
    tk = _pick_tile(s, (512, 256, 128))
    tt_p = _pick_tile(s, (256, 128, 64))
    chunk_p = min(GDN_CHUNK, tt_p)
    chunk_s = 16
    sb_sel = _pick_tile(bd, (8, 4, 2))
    pg_sel = _pick_tile(n_pages, (8, 4, 2, 1))
    pg_att = _pick_tile(n_pages, (32, 16, 8, 4, 2, 1))

    yp = x_prompt.reshape(b * s, d)
    ys = x_sample.reshape(bd * t, d)
    outs_p, outs_s = [], []
    for l in range(depth):
        lw = _layer_weights(l, att_w, conv_ch, gdn_vw, n_gdn_heads, w_in, conv_w, idx_k_norm_g, idx_k_norm_b,
                            gdn_a_log, gdn_dt_bias, gdn_norm_g, w_out, pre_mix_g, post_mix_g, pre_mlp_g,
                            post_mlp_g, w_mlp_up, w_mlp_down)
        (q, kf, vf, kb, vb, vt, iq, ikf, ikb, qkv, z, sm) = _inproj(
            yp, tab_p, lw, att_w=att_w, conv_ch=conv_ch, gdn_vw=gdn_vw, tm=tm_p)
        r3 = lambda a: a.reshape(b, s, a.shape[-1])
        att = _prompt_attention(r3(q), r3(iq), r3(sm), r3(kb), vt, r3(ikb), tq=tq, tk=tk, topk=topk_p)
        o, s_fin, conv_out = _gdn(
            r3(qkv), r3(z), r3(sm), jnp.zeros((b, SUBLANES, conv_ch), F32),
            jnp.zeros((b, n_gdn_heads, dk, dv), F32), lw, tt=tt_p, chunk=chunk_p, t_valid=s)
        yp = _outmlp(yp, att.reshape(b * s, att_w), o.reshape(b * s, gdn_vw), lw, tm=tm_p, ff_chunk=1024)
        outs_p.append((kf.reshape(b, s, n_att_heads, head_dim), vf.reshape(b, s, n_att_heads, head_dim),
                       ikf.reshape(b, s, IDX_DIM), s_fin, conv_out[:, SUBLANES - hist:, :]))

        (q, kf, vf, kb, vb, vt, iq, ikf, ikb, qkv, z, sm) = _inproj(
            ys, tab_s, lw, att_w=att_w, conv_ch=conv_ch, gdn_vw=gdn_vw, tm=tm_s)
        iq_s = iq.reshape(bd, t * N_IDX_HEADS, IDX_DIM)
        w_s = sm[:, SM_IW:SM_IW + N_IDX_HEADS].reshape(bd, t * N_IDX_HEADS, 1)
        pad_new = lambda a: jnp.pad(a.reshape(bd, t, a.shape[-1]), ((0, 0), (0, LANES - t), (0, 0)))
        sel8 = _decode_select(page_table, iq_s, w_s, pad_new(ikb), jnp.swapaxes(cache_idx_k[l], 1, 2),
                              sb=sb_sel, pg=pg_sel, n_tok=t, topk=topk_s)
        q8 = jnp.pad(q.reshape(bd, t, n_att_heads, head_dim).transpose(0, 2, 1, 3),
                     ((0, 0), (0, 0), (0, SUBLANES - t), (0, 0))).reshape(bd, n_att_heads * SUBLANES, head_dim)
        att_s = _decode_attention(page_table, q8, sel8, pad_new(kb), pad_new(vb),
                                  cache_k[l].reshape(n_pool, page * n_att_heads, head_dim),
                                  cache_v[l].reshape(n_pool, page * n_att_heads, head_dim),
                                  pg=pg_att, n_tok=t, n_heads=n_att_heads)
        pad_t = lambda a: jnp.pad(a.reshape(bd, t, a.shape[-1]), ((0, 0), (0, chunk_s - t), (0, 0)))
        conv0 = jnp.pad(state_conv[l].astype(F32), ((0, 0), (SUBLANES - hist, 0), (0, 0)))
        o_s, s_fin_s, conv_out_s = _gdn(pad_t(qkv), pad_t(z), pad_t(sm), conv0, state_gdn[l].astype(F32), lw,
                                        tt=chunk_s, chunk=chunk_s, t_valid=t)
        ys = _outmlp(ys, att_s.reshape(bd * t, att_w), o_s[:, :t].reshape(bd * t, gdn_vw), lw,
                     tm=tm_s, ff_chunk=1024)
        outs_s.append((kf.reshape(bd, t, n_att_heads, head_dim), vf.reshape(bd, t, n_att_heads, head_dim),
                       ikf.reshape(bd, t, IDX_DIM), s_fin_s, conv_out_s[:, SUBLANES - hist:, :]))

    kp, vp, ikp, gp, cp = [jnp.stack([o_[i] for o_ in outs_p]) for i in range(5)]
    ks_, vs_, iks, gs, cs = [jnp.stack([o_[i] for o_ in outs_s]) for i in range(5)]
    return (yp.reshape(b, s, d), ys.reshape(bd, t, d), kp, vp, ikp, gp, cp, ks_, vs_, iks, gs, cs)
```

```python
import functools
import math

import jax
import jax.numpy as jnp
from jax import lax
from jax.experimental import pallas as pl
from jax.experimental.pallas import tpu as pltpu

F32 = jnp.float32
BF16 = jnp.bfloat16

HEAD_DIM = 128
IDX_DIM = 64
N_IDX_HEADS = 8
GDN_DK = 128
GDN_DV = 128
CONV_W = 4
TOPK_MAX = 256
GDN_CHUNK = 64
ROPE_THETA = 10000.0
NORM_EPS = 1e-6

LANES = 128
SUBLANES = 8
VMEM_LIMIT = 56 * 1024 * 1024
NEG_BIG = -1e30
BF16_EXACT_INT = 256
VALUE_STEPS = 28
MIN_NORMAL = 2.0 ** -126
SCORE_ROWS = 256
SM_IW = IDX_DIM
SM_GB = IDX_DIM + N_IDX_HEADS
HIGHEST = lax.Precision.HIGHEST


def _dot(a, b):
    return jnp.dot(a, b, preferred_element_type=F32)


def _dot_nt(a, b):
    return lax.dot_general(a, b, (((1,), (1,)), ((), ())), preferred_element_type=F32)


def _dot_tn(a, b):
    return lax.dot_general(a, b, (((0,), (0,)), ((), ())), preferred_element_type=F32)


def _rms(x, g):
    return x * lax.rsqrt(jnp.mean(x * x, axis=-1, keepdims=True) + NORM_EPS) * g


def _const_spec(shape):
    n = len(shape)
    return pl.BlockSpec(shape, lambda *_: (0,) * n, pipeline_mode=pl.Buffered(1))


def _lane_fold(x, op):
    r = x[:, 0:LANES]
    for j in range(1, x.shape[1] // LANES):
        r = op(r, x[:, j * LANES:(j + 1) * LANES])
    return r


def _inproj_kernel(x_ref, g_ref, wa_ref, wqkv_ref, wz_ref, ws_ref,
                   cos_ref, sin_ref, cosi_ref, sina_ref, sinb_ref, lng_ref, lnb_ref, smul_ref,
                   q_ref, kf_ref, vf_ref, kb_ref, vb_ref, vt_ref, iq_ref, ikf_ref, ikb_ref,
                   qkv_ref, z_ref, sm_ref, *, att_w, q_scale):
    x = x_ref[...]
    h = _rms(x, g_ref[...]).astype(BF16)
    a = _dot(h, wa_ref[...])
    cos = cos_ref[...]
    sin = sin_ref[...]
    n_heads = att_w // HEAD_DIM

    def rope_full(xh):
        return xh * cos + pltpu.roll(xh, HEAD_DIM // 2, axis=1) * sin

    for j in range(n_heads):
        sl = slice(j * HEAD_DIM, (j + 1) * HEAD_DIM)
        q_ref[:, sl] = (rope_full(a[:, sl]) * q_scale).astype(BF16)
        kr = rope_full(a[:, att_w + j * HEAD_DIM: att_w + (j + 1) * HEAD_DIM])
        kf_ref[pl.ds(j, x.shape[0], stride=n_heads), :] = kr
        kb_ref[:, sl] = kr.astype(BF16)
    v = a[:, 2 * att_w:3 * att_w]
    for j in range(n_heads):
        vf_ref[pl.ds(j, x.shape[0], stride=n_heads), :] = v[:, j * HEAD_DIM:(j + 1) * HEAD_DIM]
    vb_ref[...] = v.astype(BF16)
    vt_ref[0] = jnp.transpose(v).astype(BF16)

    cosi = cosi_ref[...]
    sina = sina_ref[...]
    sinb = sinb_ref[...]

    def rope_half(xh):
        return (xh * cosi + pltpu.roll(xh, LANES - IDX_DIM // 2, axis=1) * sina
                + pltpu.roll(xh, IDX_DIM // 2, axis=1) * sinb)

    idx_w = N_IDX_HEADS * IDX_DIM
    for j in range(idx_w // LANES):
        sl = slice(j * LANES, (j + 1) * LANES)
        iq_ref[:, sl] = rope_half(a[:, 3 * att_w + j * LANES: 3 * att_w + (j + 1) * LANES]).astype(BF16)

    qkv_ref[...] = _dot(h, wqkv_ref[...])
    z_ref[...] = _dot(h, wz_ref[...])

    sm = _dot(h, ws_ref[...])
    lane = lax.broadcasted_iota(jnp.int32, sm.shape, 1)
    is_ik = lane < IDX_DIM
    ikraw = jnp.where(is_ik, sm, 0.0)
    mu = jnp.sum(ikraw, axis=-1, keepdims=True) * (1.0 / IDX_DIM)
    xc = jnp.where(is_ik, sm - mu, 0.0)
    var = jnp.sum(xc * xc, axis=-1, keepdims=True) * (1.0 / IDX_DIM)
    ikn = xc * lax.rsqrt(var + NORM_EPS) * lng_ref[...] + lnb_ref[...]
    ikr = rope_half(ikn)
    ikf_ref[...] = ikr[:, :IDX_DIM]
    ikb_ref[...] = ikr[:, :IDX_DIM].astype(BF16)
    sm_ref[...] = sm * smul_ref[...]


def _rope_tables(pos):
    pos = pos.astype(F32)[:, None]
    half = HEAD_DIM // 2
    inv = jnp.power(ROPE_THETA, -jnp.arange(half, dtype=F32) / half)
    ang = pos * inv[None, :]
    c, s = jnp.cos(ang), jnp.sin(ang)
    cos = jnp.concatenate([c, c], axis=-1)
    sin = jnp.concatenate([-s, s], axis=-1)
    halfi = IDX_DIM // 2
    invi = jnp.power(ROPE_THETA, -jnp.arange(halfi, dtype=F32) / halfi)
    angi = pos * invi[None, :]
    ci, si = jnp.cos(angi), jnp.sin(angi)
    zi = jnp.zeros_like(si)
    cosi = jnp.concatenate([ci, ci, ci, ci], axis=-1)
    sina = jnp.concatenate([-si, zi, -si, zi], axis=-1)
    sinb = jnp.concatenate([zi, si, zi, si], axis=-1)
    return cos, sin, cosi, sina, sinb


def _inproj(x2d, tables, lw, *, att_w, conv_ch, gdn_vw, tm):
    n, d = x2d.shape
    idx_w = N_IDX_HEADS * IDX_DIM
    rt = tables[0].shape[0]
    assert n % tm == 0 and rt % tm == 0 and n % rt == 0
    nt = rt // tm
    n_heads = att_w // HEAD_DIM
    wa_w = 3 * att_w + idx_w
    row = lambda w: pl.BlockSpec((tm, w), lambda i: (i, 0))
    tab = pl.BlockSpec((tm, LANES), lambda i: (i % nt, 0))
    out_shapes = (
        jax.ShapeDtypeStruct((n, att_w), BF16),
        jax.ShapeDtypeStruct((n * n_heads, HEAD_DIM), F32),
        jax.ShapeDtypeStruct((n * n_heads, HEAD_DIM), F32),
        jax.ShapeDtypeStruct((n, att_w), BF16),
        jax.ShapeDtypeStruct((n, att_w), BF16),
        jax.ShapeDtypeStruct((n // rt, att_w, rt), BF16),
        jax.ShapeDtypeStruct((n, idx_w), BF16),
        jax.ShapeDtypeStruct((n, IDX_DIM), F32),
        jax.ShapeDtypeStruct((n, IDX_DIM), BF16),
        jax.ShapeDtypeStruct((n, conv_ch), F32),
        jax.ShapeDtypeStruct((n, gdn_vw), F32),
        jax.ShapeDtypeStruct((n, LANES), F32),
    )
    vt_spec = pl.BlockSpec((1, att_w, tm), lambda i: (i // nt, 0, i % nt))
    head_rows = pl.BlockSpec((tm * n_heads, HEAD_DIM), lambda i: (i, 0))
    out_specs = (row(att_w), head_rows, head_rows, row(att_w), row(att_w), vt_spec, row(idx_w),
                 row(IDX_DIM), row(IDX_DIM), row(conv_ch), row(gdn_vw), row(LANES))
    in_specs = [row(d), _const_spec((1, d)), _const_spec((d, wa_w)), _const_spec((d, conv_ch)),
                _const_spec((d, gdn_vw)), _const_spec((d, LANES)),
                tab, tab, tab, tab, tab,
                _const_spec((1, LANES)), _const_spec((1, LANES)), _const_spec((1, LANES))]
    return pl.pallas_call(
        functools.partial(_inproj_kernel, att_w=att_w, q_scale=HEAD_DIM ** -0.5),
        grid=(n // tm,),
        in_specs=in_specs, out_specs=out_specs, out_shape=out_shapes,
        compiler_params=pltpu.CompilerParams(dimension_semantics=("arbitrary",),
                                             vmem_limit_bytes=VMEM_LIMIT),
        name="inproj",
    )(x2d, lw["pre_mix_g"], lw["wa"], lw["wqkv"], lw["wz"], lw["ws"], *tables,
      lw["lng"], lw["lnb"], lw["smul"])


def _tie_break_by_index_search(count, rewrite, col_limit):
    def break_ties(lo, hi, want, tied):
        def in_tie(x):
            return jnp.logical_and(x >= lo, x < hi)

        def jstep(_, c):
            jlo, jhi = c
            jm = jnp.floor((jlo + jhi) * 0.5)
            cnt = count(lambda x, col: jnp.logical_and(in_tie(x), col <= jm))
            ok = cnt >= want
            return jnp.where(ok, jlo, jm), jnp.where(ok, jm, jhi)

        n_it = max(1, math.ceil(math.log2(col_limit + 1)))
        j0 = (jnp.full_like(lo, -1.0), jnp.full_like(lo, float(col_limit - 1)))
        _, jcut = lax.fori_loop(0, n_it, jstep, j0)
        rewrite(lambda x, col: jnp.where(
            jnp.logical_and(tied, jnp.logical_and(in_tie(x), col > jcut)), -jnp.inf, x))
    return break_ties


def _select_threshold(count, break_ties, n_valid, row_max, row_min, topk):
    kf = float(topk)
    flip = jnp.int32(0x7FFFFFFF)

    def to_key(x):
        b = lax.bitcast_convert_type(jnp.where(x == 0.0, 0.0, x), jnp.int32)
        return jnp.where(b < 0, b ^ flip, b)

    def from_key(k):
        min_normal = jnp.int32(0x00800000)
        k = jnp.where(jnp.logical_and(k > 0, k < min_normal), min_normal, k)
        return lax.bitcast_convert_type(jnp.where(k < 0, k ^ flip, k), F32)

    need = n_valid > kf
    spread = jnp.abs(row_max) * (2.0 ** -20) + 2.0 ** -100
    c_ge0 = count(lambda x, col: x >= 0.0)
    c_gt0 = count(lambda x, col: x > 0.0)
    non_neg = jnp.logical_and(need, c_ge0 >= kf)
    at_zero = jnp.logical_and(non_neg, c_gt0 < kf)
    negative = jnp.logical_and(need, c_ge0 < kf)
    lo0 = jnp.where(non_neg, 0.0, jnp.where(need, row_min, NEG_BIG))
    c_lo0 = jnp.where(non_neg, c_ge0, n_valid)
    hi0 = jnp.where(at_zero, MIN_NORMAL, jnp.where(negative, 0.0, row_max + spread))
    c_hi0 = jnp.where(at_zero, c_gt0, jnp.where(negative, c_ge0, 0.0))
    open0 = jnp.logical_and(need, jnp.logical_not(jnp.logical_or(at_zero, c_lo0 == kf)))
    done0 = jnp.where(open0, 0.0, 1.0)

    def narrow(c, mid, stuck, cnt):
        lo, hi, c_lo, c_hi, done = c
        ge = cnt >= kf
        upd = jnp.logical_and(done < 0.5, jnp.logical_not(stuck))
        up_lo = jnp.logical_and(upd, ge)
        up_hi = jnp.logical_and(upd, jnp.logical_not(ge))
        fin = jnp.logical_or(stuck, jnp.logical_and(upd, cnt == kf))
        return (jnp.where(up_lo, mid, lo), jnp.where(up_hi, mid, hi), jnp.where(up_lo, cnt, c_lo),
                jnp.where(up_hi, cnt, c_hi), jnp.where(fin, 1.0, done))

    def value_step(c):
        it, st = c
        lo, hi = st[0], st[1]
        mid = 0.5 * lo + 0.5 * hi
        stuck = jnp.logical_or(mid <= lo, mid >= hi)
        return it + 1, narrow(st, mid, stuck, count(lambda x, col: x >= mid))

    _, (lo, hi, c_lo, c_hi, done) = lax.while_loop(
        lambda c: jnp.logical_and(c[0] < VALUE_STEPS, jnp.min(c[1][4]) < 0.5), value_step,
        (jnp.int32(0), (lo0, hi0, c_lo0, c_hi0, done0)))

    def key_step(st):
        klo, khi = st[0], st[1]
        mid = (klo & khi) + ((klo ^ khi) >> 1)
        return narrow(st, mid, mid <= klo, count(lambda x, col: to_key(x) >= mid))

    klo, khi, c_lo, c_hi, _ = lax.while_loop(
        lambda st: jnp.min(st[4]) < 0.5, key_step, (to_key(lo), to_key(hi), c_lo, c_hi, done))
    lo, hi = from_key(klo), from_key(khi)

    tied = jnp.logical_and(need, c_lo > kf)

    @pl.when(jnp.max(jnp.where(tied, 1.0, 0.0)) > 0.5)
    def _():
        break_ties(lo, hi, kf - c_hi, tied)

    return lo


def _pattn_kernel(q_ref, iq_ref, sm_ref, k_ref, vt_ref, ik_ref, o_ref, sc_ref, m_ref, l_ref, acc_ref,
                  *, tq, tk, topk, n_heads):
    qi = pl.program_id(1)
    t0 = qi * tq
    nkb = (t0 + tq + tk - 1) // tk
    sub_tiles = tk // SUBLANES
    q_t = t0 + lax.broadcasted_iota(jnp.int32, (1, tq), 1)
    smt = jnp.transpose(sm_ref[0])
    w_rows = [smt[SM_IW + h:SM_IW + h + 1, :] for h in range(N_IDX_HEADS)]
    iq = iq_ref[0]
    iq_heads = [iq[:, h * IDX_DIM:(h + 1) * IDX_DIM] for h in range(N_IDX_HEADS)]
    sub_iota = lax.broadcasted_iota(jnp.int32, (SUBLANES, tq), 0)

    def sub_fold(x, op):
        r = x[0:SUBLANES]
        for j in range(1, x.shape[0] // SUBLANES):
            r = op(r, x[j * SUBLANES:(j + 1) * SUBLANES])
        return r

    def rep(row):
        return jnp.broadcast_to(row, (SUBLANES, tq))

    def score_block(kb, carry):
        mx, mn = carry
        k0 = pl.multiple_of(kb * tk, tk)
        for c in range(tk // SCORE_ROWS):
            r0 = k0 + c * SCORE_ROWS
            ikc = ik_ref[0, pl.ds(r0, SCORE_ROWS), :]
            acc = jnp.zeros((SCORE_ROWS, tq), F32)
            for h in range(N_IDX_HEADS):
                acc = acc + w_rows[h] * jnp.maximum(_dot_nt(ikc, iq_heads[h]), 0.0)
            key = r0 + lax.broadcasted_iota(jnp.int32, (SCORE_ROWS, 1), 0)
            valid = key <= q_t
            sc_ref[pl.ds(r0, SCORE_ROWS), :] = jnp.where(valid, acc, -jnp.inf)
            mx = jnp.maximum(mx, sub_fold(jnp.where(valid, acc, -jnp.inf), jnp.maximum))
            mn = jnp.minimum(mn, sub_fold(jnp.where(valid, acc, jnp.inf), jnp.minimum))
        return mx, mn

    mx, mn = lax.fori_loop(0, nkb, score_block,
                           (jnp.full((SUBLANES, tq), -jnp.inf, F32), jnp.full((SUBLANES, tq), jnp.inf, F32)))
    row_max = rep(jnp.max(mx, axis=0, keepdims=True))
    row_min = rep(jnp.min(mn, axis=0, keepdims=True))
    n_valid = rep((q_t + 1).astype(F32))

    def count(pred):
        def body(kb, acc):
            k0 = pl.multiple_of(kb * tk, tk)
            x = sc_ref[pl.ds(k0, tk), :]
            for j in range(sub_tiles):
                hit = pred(x[j * SUBLANES:(j + 1) * SUBLANES], (k0 + j * SUBLANES + sub_iota).astype(F32))
                acc = acc + jnp.where(hit, 1.0, 0.0)
            return acc
        acc = lax.fori_loop(0, nkb, body, jnp.zeros((SUBLANES, tq), F32))
        return rep(jnp.sum(acc, axis=0, keepdims=True))

    def break_ties(lo, hi, want, tied):
        def body(kb, seen):
            k0 = pl.multiple_of(kb * tk, tk)
            x = sc_ref[pl.ds(k0, tk), :]
            out = []
            for j in range(sub_tiles):
                xs = x[j * SUBLANES:(j + 1) * SUBLANES]
                t = jnp.where(tied, jnp.where(xs >= lo, jnp.where(xs < hi, 1.0, 0.0), 0.0), 0.0)
                upto = t
                for sh in (1, 2, 4):
                    upto = upto + jnp.where(sub_iota >= sh, pltpu.roll(upto, sh, axis=0), 0.0)
                out.append(jnp.where(t > 0.5, jnp.where(seen + upto <= want, xs, -jnp.inf), xs))
                seen = seen + rep(upto[SUBLANES - 1:SUBLANES, :])
            sc_ref[pl.ds(k0, tk), :] = jnp.concatenate(out, axis=0)
            return seen
        lax.fori_loop(0, nkb, body, jnp.zeros((SUBLANES, tq), F32))

    lo = _select_threshold(count, break_ties, n_valid, row_max, row_min, topk)
    tile_rows = lambda v8: jnp.concatenate([v8] * sub_tiles, axis=0)
    lo_t = tile_rows(lo)

    def masked_scores(kb, h, bias):
        k0 = pl.multiple_of(kb * tk, tk)
        sl = slice(h * HEAD_DIM, (h + 1) * HEAD_DIM)
        return _dot_nt(k_ref[0, pl.ds(k0, tk), sl], q_ref[0, :, sl]) + bias

    m_ref[...] = jnp.full(m_ref.shape, NEG_BIG, F32)
    l_ref[...] = jnp.zeros_like(l_ref)
    acc_ref[...] = jnp.zeros_like(acc_ref)

    def pv_block(kb, carry):
        k0 = pl.multiple_of(kb * tk, tk)
        bias = jnp.where(sc_ref[pl.ds(k0, tk), :] >= lo_t, 0.0, NEG_BIG)
        ss = [masked_scores(kb, h, bias) for h in range(n_heads)]
        hss = [slice(h * SUBLANES, (h + 1) * SUBLANES) for h in range(n_heads)]
        m_old = [m_ref[hs, :] for hs in hss]
        m_new = [jnp.maximum(m_old[h], rep(jnp.max(sub_fold(ss[h], jnp.maximum), axis=0, keepdims=True)))
                 for h in range(n_heads)]
        ps = [jnp.exp(ss[h] - tile_rows(m_new[h])) for h in range(n_heads)]
        for h in range(n_heads):
            sl = slice(h * HEAD_DIM, (h + 1) * HEAD_DIM)
            alpha = jnp.exp(m_old[h] - m_new[h])
            m_ref[hss[h], :] = m_new[h]
            l_ref[hss[h], :] = alpha * l_ref[hss[h], :] + sub_fold(ps[h], jnp.add)
            acc_ref[sl, :] = (alpha[0:1, :] * acc_ref[sl, :]
                              + _dot(vt_ref[0, sl, pl.ds(k0, tk)], ps[h].astype(BF16)))
        return carry
    lax.fori_loop(0, nkb, pv_block, 0)

    for h in range(n_heads):
        sl = slice(h * HEAD_DIM, (h + 1) * HEAD_DIM)
        l_row = jnp.sum(l_ref[h * SUBLANES:(h + 1) * SUBLANES, :], axis=0, keepdims=True)
        o_ref[0, :, sl] = jnp.transpose(acc_ref[sl, :] / l_row).astype(o_ref.dtype)


def _prompt_attention(q, iq, sm, kb, vt, ikb, *, tq, tk, topk):
    b, s, att_w = q.shape
    idx_w = iq.shape[-1]
    n_heads = att_w // HEAD_DIM
    assert s % tq == 0 and s % tk == 0 and tk % tq == 0 and tq % LANES == 0
    blk = lambda w: pl.BlockSpec((1, tq, w), lambda bi, qi: (bi, qi, 0))
    full = lambda r, w: pl.BlockSpec((1, r, w), lambda bi, qi: (bi, 0, 0), pipeline_mode=pl.Buffered(1))
    return pl.pallas_call(
        functools.partial(_pattn_kernel, tq=tq, tk=tk, topk=topk, n_heads=n_heads),
        grid=(b, s // tq),
        in_specs=[blk(att_w), blk(idx_w), blk(LANES), full(s, att_w), full(att_w, s), full(s, IDX_DIM)],
        out_specs=blk(att_w),
        out_shape=jax.ShapeDtypeStruct((b, s, att_w), BF16),
        scratch_shapes=[pltpu.VMEM((s, tq), F32), pltpu.VMEM((n_heads * SUBLANES, tq), F32),
                        pltpu.VMEM((n_heads * SUBLANES, tq), F32), pltpu.VMEM((att_w, tq), F32)],
        compiler_params=pltpu.CompilerParams(dimension_semantics=("arbitrary", "arbitrary"),
                                             vmem_limit_bytes=VMEM_LIMIT),
        name="prompt_attention",
    )(q, iq, sm, kb, vt, ikb)


def _dsel_kernel(pt_ref, iq_ref, w_ref, iknew_ref, *rest, sb, pg, n_tok, past, topk):
    pages = rest[:sb * pg]
    sel_ref = rest[sb * pg]
    sc_ref = rest[sb * pg + 1]
    c = pl.program_id(1)
    n_c = pl.num_programs(1)
    page = pages[0].shape[2]
    rows, l_pad = sc_ref.shape

    def token_rows(s):
        return [jnp.sum(s[t * N_IDX_HEADS:(t + 1) * N_IDX_HEADS, :], axis=0, keepdims=True)
                for t in range(n_tok)]

    @pl.when(c == 0)
    def _():
        sc_ref[...] = jnp.full(sc_ref.shape, -jnp.inf, F32)

    for j in range(sb):
        iq = iq_ref[j]
        w = w_ref[j]
        kt = jnp.concatenate([pages[j * pg + p][0] for p in range(pg)], axis=1).astype(BF16)
        tr = token_rows(jnp.maximum(_dot(iq, kt), 0.0) * w)
        k0 = pl.multiple_of(c * pg * page, pg * page)
        for t in range(n_tok):
            sc_ref[j * n_tok + t:j * n_tok + t + 1, pl.ds(k0, pg * page)] = tr[t]

    @pl.when(c == n_c - 1)
    def _():
        col = lax.broadcasted_iota(jnp.int32, (1, LANES), 1)
        for j in range(sb):
            tr = token_rows(jnp.maximum(_dot_nt(iq_ref[j], iknew_ref[j]), 0.0) * w_ref[j])
            for t in range(n_tok):
                sc_ref[j * n_tok + t:j * n_tok + t + 1, past:past + LANES] = jnp.where(col <= t, tr[t], -jnp.inf)
        x = sc_ref[...]
        rep = lambda v: jnp.broadcast_to(v, (rows, LANES))
        row_max = rep(jnp.max(x, axis=-1, keepdims=True))
        row_min = rep(jnp.min(jnp.where(x > -jnp.inf, x, jnp.inf), axis=-1, keepdims=True))
        r = lax.broadcasted_iota(jnp.int32, (rows, LANES), 0).astype(F32)
        tok = r - n_tok * jnp.floor(r * (1.0 / n_tok))
        n_valid = past + 1.0 + tok
        n_blocks = l_pad // LANES
        assert n_blocks <= BF16_EXACT_INT
        ones_b = jnp.ones((LANES, LANES), BF16)
        lane_iota = lax.broadcasted_iota(jnp.int32, (rows, LANES), 1)

        def count(pred):
            def body(kb, acc):
                c0 = pl.multiple_of(kb * LANES, LANES)
                hit = pred(sc_ref[:, pl.ds(c0, LANES)], (c0 + lane_iota).astype(F32))
                return acc + jnp.where(hit, 1.0, 0.0)
            acc = lax.fori_loop(0, n_blocks, body, jnp.zeros((rows, LANES), F32))
            return _dot(acc.astype(BF16), ones_b)

        def rewrite(fn):
            def body(kb, carry):
                c0 = pl.multiple_of(kb * LANES, LANES)
                sc_ref[:, pl.ds(c0, LANES)] = fn(sc_ref[:, pl.ds(c0, LANES)], (c0 + lane_iota).astype(F32))
                return carry
            lax.fori_loop(0, n_blocks, body, 0)

        lo = _select_threshold(count, _tie_break_by_index_search(count, rewrite, l_pad),
                               n_valid, row_max, row_min, topk)

        def emit(kb, carry):
            c0 = pl.multiple_of(kb * LANES, LANES)
            x = jnp.where(sc_ref[:, pl.ds(c0, LANES)] >= lo, 1.0, 0.0)
            for j in range(sb):
                sel_ref[j, 0:n_tok, pl.ds(c0, LANES)] = x[j * n_tok:(j + 1) * n_tok]
                sel_ref[j, n_tok:SUBLANES, pl.ds(c0, LANES)] = jnp.zeros((SUBLANES - n_tok, LANES), F32)
            return carry
        lax.fori_loop(0, l_pad // LANES, emit, 0)


def _decode_select(page_table, iq, w, iknew, cache_idx_t, *, sb, pg, n_tok, topk):
    bd, n_pages = page_table.shape
    _, di, page = cache_idx_t.shape
    past = n_pages * page
    assert n_pages % pg == 0 and bd % sb == 0 and (sb * n_tok) % SUBLANES == 0
    l_pad = past + LANES
    rows = iq.shape[1]

    def page_spec(j, p):
        return pl.BlockSpec((1, di, page), lambda g, c, pt: (pt[g * sb + j, c * pg + p], 0, 0))

    grid_spec = pltpu.PrefetchScalarGridSpec(
        num_scalar_prefetch=1,
        grid=(bd // sb, n_pages // pg),
        in_specs=[pl.BlockSpec((sb, rows, di), lambda g, c, pt: (g, 0, 0)),
                  pl.BlockSpec((sb, rows, 1), lambda g, c, pt: (g, 0, 0)),
                  pl.BlockSpec((sb, LANES, di), lambda g, c, pt: (g, 0, 0))]
                 + [page_spec(j, p) for j in range(sb) for p in range(pg)],
        out_specs=pl.BlockSpec((sb, SUBLANES, l_pad), lambda g, c, pt: (g, 0, 0)),
        scratch_shapes=[pltpu.VMEM((sb * n_tok, l_pad), F32)],
    )
    return pl.pallas_call(
        functools.partial(_dsel_kernel, sb=sb, pg=pg, n_tok=n_tok, past=past, topk=topk),
        grid_spec=grid_spec,
        out_shape=jax.ShapeDtypeStruct((bd, SUBLANES, l_pad), F32),
        compiler_params=pltpu.CompilerParams(dimension_semantics=("arbitrary", "arbitrary"),
                                             vmem_limit_bytes=VMEM_LIMIT),
        name="decode_select",
    )(page_table, iq, w, iknew, *([cache_idx_t] * (sb * pg)))


def _dattn_kernel(pt_ref, q_ref, sel_ref, knew_ref, vnew_ref, *rest, pg, n_tok, n_heads, past):
    kpages = rest[:pg]
    vpages = rest[pg:2 * pg]
    o_ref = rest[2 * pg]
    m_ref, l_ref, acc_ref = rest[2 * pg + 1:]
    c = pl.program_id(1)
    n_c = pl.num_programs(1)
    page = kpages[0].shape[1] // n_heads
    qh = [q_ref[0, h * SUBLANES:(h + 1) * SUBLANES, :] for h in range(n_heads)]

    @pl.when(c == 0)
    def _():
        m_ref[...] = jnp.full(m_ref.shape, NEG_BIG, F32)
        l_ref[...] = jnp.zeros_like(l_ref)
        acc_ref[...] = jnp.zeros_like(acc_ref)

    def update(k_of, v_of, n_blk, sel):
        s = jnp.concatenate(
            [jnp.concatenate([_dot_nt(qh[h], k_of(p, h)) for p in range(n_blk)], axis=1)
             for h in range(n_heads)], axis=0)
        s = jnp.where(jnp.concatenate([sel] * n_heads, axis=0) > 0.5, s, NEG_BIG)
        m_old = m_ref[...]
        m_new = jnp.maximum(m_old, jnp.max(s, axis=-1, keepdims=True))
        pr = jnp.exp(s - m_new).astype(BF16)
        alpha = jnp.exp(m_old - m_new)
        l_ref[...] = alpha * l_ref[...] + jnp.sum(pr.astype(F32), axis=-1, keepdims=True)
        pv = []
        for h in range(n_heads):
            a = jnp.zeros((SUBLANES, HEAD_DIM), F32)
            for p in range(n_blk):
                a = a + _dot(pr[h * SUBLANES:(h + 1) * SUBLANES, p * page:(p + 1) * page], v_of(p, h))
            pv.append(a)
        acc_ref[...] = alpha * acc_ref[...] + jnp.concatenate(pv, axis=0)
        m_ref[...] = m_new

    head_rows = lambda ref, h: ref[0, pl.ds(h, page, stride=n_heads), :].astype(BF16)
    k0 = pl.multiple_of(c * pg * page, pg * page)
    update(lambda p, h: head_rows(kpages[p], h), lambda p, h: head_rows(vpages[p], h), pg,
           sel_ref[0, :, pl.ds(k0, pg * page)])

    @pl.when(c == n_c - 1)
    def _():
        new = lambda ref, h: ref[0, :, h * HEAD_DIM:(h + 1) * HEAD_DIM]
        update(lambda p, h: new(knew_ref, h), lambda p, h: new(vnew_ref, h), 1,
               sel_ref[0, :, past:past + LANES])
        inv_l = 1.0 / l_ref[...]
        out = acc_ref[...] * inv_l
        for h in range(n_heads):
            o_ref[0, :, h * HEAD_DIM:(h + 1) * HEAD_DIM] = out[h * SUBLANES:h * SUBLANES + n_tok, :].astype(o_ref.dtype)


def _decode_attention(page_table, q8, sel, knew, vnew, cache_k, cache_v, *, pg, n_tok, n_heads):
    bd, n_pages = page_table.shape
    _, page_rows, hd = cache_k.shape
    page = page_rows // n_heads
    past = n_pages * page
    l_pad = sel.shape[-1]
    att_w = n_heads * hd
    rows = n_heads * SUBLANES
    assert n_pages % pg == 0 and page == LANES

    def page_spec(p):
        return pl.BlockSpec((1, page_rows, hd), lambda b, c, pt: (pt[b, c * pg + p], 0, 0))

    per_seq = lambda r, w: pl.BlockSpec((1, r, w), lambda b, c, pt: (b, 0, 0))
    grid_spec = pltpu.PrefetchScalarGridSpec(
        num_scalar_prefetch=1,
        grid=(bd, n_pages // pg),
        in_specs=[per_seq(rows, hd), per_seq(SUBLANES, l_pad), per_seq(LANES, att_w), per_seq(LANES, att_w)]
                 + [page_spec(p) for p in range(pg)] * 2,
        out_specs=per_seq(n_tok, att_w),
        scratch_shapes=[pltpu.VMEM((rows, 1), F32), pltpu.VMEM((rows, 1), F32), pltpu.VMEM((rows, hd), F32)],
    )
    return pl.pallas_call(
        functools.partial(_dattn_kernel, pg=pg, n_tok=n_tok, n_heads=n_heads, past=past),
        grid_spec=grid_spec,
        out_shape=jax.ShapeDtypeStruct((bd, n_tok, att_w), BF16),
        compiler_params=pltpu.CompilerParams(dimension_semantics=("arbitrary", "arbitrary"),
                                             vmem_limit_bytes=VMEM_LIMIT),
        name="decode_attention",
    )(page_table, q8, sel, knew, vnew, *([cache_k] * pg), *([cache_v] * pg))


def _gdn_kernel(x_ref, z_ref, sm_ref, conv0_ref, s0_ref, cw_ref, alog_ref, dtb_ref, ng_ref,
                o_ref, sfin_ref, convout_ref,
                xbuf, st_ref, q_s, k_s, v_s, beta_s, g_s,
                *, sb, tt, chunk, n_heads, t_valid, n_steps):
    step = pl.program_id(1)
    hist = CONV_W - 1
    qk_w = n_heads * GDN_DK

    @pl.when(step == 0)
    def _():
        xbuf[:, 0:SUBLANES, :] = conv0_ref[...]
        st_ref[...] = s0_ref[...]

    @pl.when(step > 0)
    def _():
        xbuf[:, 0:SUBLANES, :] = xbuf[:, tt:tt + SUBLANES, :]

    cw = cw_ref[...]
    t_glob = step * tt + lax.broadcasted_iota(jnp.int32, (tt, 1), 0)
    live = t_glob < t_valid

    def l2n(xh):
        return xh * lax.rsqrt(jnp.sum(xh * xh, axis=-1, keepdims=True) + NORM_EPS)

    for j in range(sb):
        xbuf[j, SUBLANES:SUBLANES + tt, :] = x_ref[j]
        conv = cw[hist:hist + 1, :] * xbuf[j, SUBLANES:SUBLANES + tt, :]
        for i in range(hist):
            conv = conv + cw[i:i + 1, :] * xbuf[j, SUBLANES - hist + i:SUBLANES - hist + i + tt, :]
        act = conv * jax.nn.sigmoid(conv)
        for h in range(n_heads):
            sl = slice(h * GDN_DK, (h + 1) * GDN_DK)
            q_s[j, :, sl] = l2n(act[:, sl]) * (GDN_DK ** -0.5)
            k_s[j, :, sl] = l2n(act[:, qk_w + h * GDN_DK: qk_w + (h + 1) * GDN_DK])
        v_s[j] = act[:, 2 * qk_w:]
        smv = sm_ref[j]
        beta_s[j] = jnp.where(live, jax.nn.sigmoid(smv), 0.0)
        g_s[j] = jnp.where(live, -jnp.exp(alog_ref[...]) * jax.nn.softplus(smv + dtb_ref[...]), 0.0)

    ii = lax.broadcasted_iota(jnp.int32, (chunk, chunk), 0)
    jj = lax.broadcasted_iota(jnp.int32, (chunk, chunk), 1)
    tril = ii >= jj
    strict = ii > jj
    tril_f = jnp.where(tril, 1.0, 0.0)
    eye = jnp.where(ii == jj, 1.0, 0.0)
    lane8 = lax.broadcasted_iota(jnp.int32, (SUBLANES, LANES), 1)
    row8 = lax.broadcasted_iota(jnp.int32, (SUBLANES, LANES), 0)
    g_lane0 = SM_GB + n_heads
    pick_g = jnp.where(lane8 == g_lane0 + row8, 1.0, 0.0)
    n_levels = int(math.log2(chunk))
    ng = ng_ref[...]

    n_chunks = tt // chunk
    tiles = [(j, ci) for j in range(sb) for ci in range(n_chunks)]
    units = [(j, ci, h) for j, ci in tiles for h in range(n_heads)]
    rows_of = lambda ci: slice(ci * chunk, (ci + 1) * chunk)
    lanes_of = lambda h: slice(h * GDN_DK, (h + 1) * GDN_DK)
    gcs = {(j, ci): jnp.dot(tril_f, g_s[j, rows_of(ci), :], precision=HIGHEST, preferred_element_type=F32)
           for j, ci in tiles}
    gcrs = {t: lax.dot_general(pick_g, gcs[t], (((1,), (1,)), ((), ())), precision=HIGHEST,
                               preferred_element_type=F32) for t in tiles}
    gcol = {(j, ci, h): gcs[(j, ci)][:, g_lane0 + h:g_lane0 + h + 1] for j, ci, h in units}
    bcol = {(j, ci, h): beta_s[j, rows_of(ci), SM_GB + h:SM_GB + h + 1] for j, ci, h in units}
    decay = {u: jnp.exp(jnp.where(tril, gcol[u] - gcrs[u[:2]][u[2]:u[2] + 1, :], -jnp.inf)) for u in units}
    k_f = {(j, ci, h): k_s[j, rows_of(ci), lanes_of(h)] for j, ci, h in units}
    q_f = {(j, ci, h): q_s[j, rows_of(ci), lanes_of(h)] for j, ci, h in units}
    k_b = {u: k_f[u].astype(BF16) for u in units}
    kb_f = {u: k_f[u] * bcol[u] for u in units}
    lmat = {u: jnp.where(strict, _dot_nt(kb_f[u].astype(BF16), k_b[u]) * decay[u], 0.0) for u in units}
    qk = {u: (_dot_nt(q_f[u].astype(BF16), k_b[u]) * decay[u]).astype(BF16) for u in units}
    pw = {u: -lmat[u] for u in units}
    inv = {u: eye + pw[u] for u in units}
    for _ in range(n_levels - 1):
        pw = {u: _dot(pw[u].astype(BF16), pw[u].astype(BF16)) for u in units}
        inv = {u: inv[u] + _dot(inv[u].astype(BF16), pw[u].astype(BF16)) for u in units}
    inv_b = {u: inv[u].astype(BF16) for u in units}
    u_rhs = {(j, ci, h): _dot(inv_b[(j, ci, h)], (v_s[j, rows_of(ci), lanes_of(h)] * bcol[(j, ci, h)]).astype(BF16))
             for j, ci, h in units}
    w_rhs = {u: _dot(inv_b[u], (kb_f[u] * jnp.exp(gcol[u])).astype(BF16)).astype(BF16) for u in units}

    streams = [(j, h) for j in range(sb) for h in range(n_heads)]
    state = {(j, h): st_ref[j, h] for j, h in streams}
    for ci in range(n_chunks):
        cur = {(j, h): (j, ci, h) for j, h in streams}
        s_b = {sh: state[sh].astype(BF16) for sh in streams}
        ws = {sh: _dot(w_rhs[cur[sh]], s_b[sh]) for sh in streams}
        qs = {sh: _dot((q_f[cur[sh]] * jnp.exp(gcol[cur[sh]])).astype(BF16), s_b[sh]) for sh in streams}
        v_new = {sh: (u_rhs[cur[sh]] - ws[sh]).astype(BF16) for sh in streams}
        outs = {sh: qs[sh] + _dot(qk[cur[sh]], v_new[sh]) for sh in streams}
        g_last = {(j, h): gcs[(j, ci)][chunk - 1:chunk, g_lane0 + h:g_lane0 + h + 1] for j, h in streams}
        k_dec = {sh: (k_f[cur[sh]] * jnp.exp(g_last[sh] - gcol[cur[sh]])).astype(BF16) for sh in streams}
        state = {sh: state[sh] * jnp.exp(g_last[sh]) + _dot_tn(k_dec[sh], v_new[sh]) for sh in streams}
        for j, h in streams:
            zh = z_ref[j, rows_of(ci), lanes_of(h)]
            o_ref[j, rows_of(ci), lanes_of(h)] = (
                _rms(outs[(j, h)], ng) * (zh * jax.nn.sigmoid(zh))).astype(o_ref.dtype)
    for j, h in streams:
        st_ref[j, h] = state[(j, h)]

    @pl.when(step == n_steps - 1)
    def _():
        sfin_ref[...] = st_ref[...]
        last = t_valid - (n_steps - 1) * tt
        convout_ref[...] = xbuf[:, last:last + SUBLANES, :]


def _gdn(x, z, sm, conv0, s0, lw, *, sb, tt, chunk, t_valid):
    nb, t_pad, conv_ch = x.shape
    n_heads = s0.shape[1]
    vw = z.shape[-1]
    assert t_pad % tt == 0 and tt % chunk == 0 and nb % sb == 0
    n_steps = t_pad // tt
    assert 0 < t_valid - (n_steps - 1) * tt <= tt and t_valid >= CONV_W - 1
    tok = lambda w: pl.BlockSpec((sb, tt, w), lambda b, s: (b, s, 0))
    return pl.pallas_call(
        functools.partial(_gdn_kernel, sb=sb, tt=tt, chunk=chunk, n_heads=n_heads, t_valid=t_valid,
                          n_steps=n_steps),
        grid=(nb // sb, n_steps),
        in_specs=[tok(conv_ch), tok(vw), tok(LANES),
                  pl.BlockSpec((sb, SUBLANES, conv_ch), lambda b, s: (b, 0, 0)),
                  pl.BlockSpec((sb, n_heads, GDN_DK, GDN_DV), lambda b, s: (b, 0, 0, 0)),
                  pl.BlockSpec((SUBLANES, conv_ch), lambda b, s: (0, 0)),
                  pl.BlockSpec((1, LANES), lambda b, s: (0, 0)),
                  pl.BlockSpec((1, LANES), lambda b, s: (0, 0)),
                  pl.BlockSpec((1, GDN_DV), lambda b, s: (0, 0))],
        out_specs=(tok(vw),
                   pl.BlockSpec((sb, n_heads, GDN_DK, GDN_DV), lambda b, s: (b, 0, 0, 0)),
                   pl.BlockSpec((sb, SUBLANES, conv_ch), lambda b, s: (b, 0, 0))),
        out_shape=(jax.ShapeDtypeStruct((nb, t_pad, vw), BF16),
                   jax.ShapeDtypeStruct((nb, n_heads, GDN_DK, GDN_DV), F32),
                   jax.ShapeDtypeStruct((nb, SUBLANES, conv_ch), F32)),
        scratch_shapes=[pltpu.VMEM((sb, tt + 2 * SUBLANES, conv_ch), F32),
                        pltpu.VMEM((sb, n_heads, GDN_DK, GDN_DV), F32),
                        pltpu.VMEM((sb, tt, n_heads * GDN_DK), F32),
                        pltpu.VMEM((sb, tt, n_heads * GDN_DK), F32),
                        pltpu.VMEM((sb, tt, vw), F32),
                        pltpu.VMEM((sb, tt, LANES), F32),
                        pltpu.VMEM((sb, tt, LANES), F32)],
        compiler_params=pltpu.CompilerParams(dimension_semantics=("arbitrary", "arbitrary"),
                                             vmem_limit_bytes=VMEM_LIMIT),
        name="gated_delta",
    )(x, z, sm, conv0, s0, lw["conv_w"], lw["alog"], lw["dtb"], lw["gdn_norm_g"])


def _outmlp_kernel(x_ref, att_ref, o_ref, woa_ref, wob_ref, g1_ref, g2_ref, g3_ref, wup_ref, wdn_ref,
                   y_ref, *, ff_chunk):
    x = x_ref[...]
    mix = _dot(att_ref[...], woa_ref[...]) + _dot(o_ref[...], wob_ref[...])
    x1 = x + _rms(mix, g1_ref[...])
    h2 = _rms(x1, g2_ref[...]).astype(BF16)
    d_ff = wup_ref.shape[1]
    ff = jnp.zeros(x.shape, F32)
    for c in range(d_ff // ff_chunk):
        u = jnp.maximum(_dot(h2, wup_ref[:, c * ff_chunk:(c + 1) * ff_chunk]), 0.0)
        ff = ff + _dot((u * u).astype(BF16), wdn_ref[c * ff_chunk:(c + 1) * ff_chunk, :])
    y_ref[...] = x1 + _rms(ff, g3_ref[...])


def _outmlp(x2d, att, o, lw, *, tm, ff_chunk):
    n, d = x2d.shape
    att_w = att.shape[1]
    vw = o.shape[1]
    d_ff = lw["wup"].shape[1]
    assert n % tm == 0 and d_ff % ff_chunk == 0
    row = lambda w: pl.BlockSpec((tm, w), lambda i: (i, 0))
    return pl.pallas_call(
        functools.partial(_outmlp_kernel, ff_chunk=ff_chunk),
        grid=(n // tm,),
        in_specs=[row(d), row(att_w), row(vw), _const_spec((att_w, d)), _const_spec((vw, d)),
                  _const_spec((1, d)), _const_spec((1, d)), _const_spec((1, d)),
                  _const_spec((d, d_ff)), _const_spec((d_ff, d))],
        out_specs=row(d),
        out_shape=jax.ShapeDtypeStruct((n, d), F32),
        compiler_params=pltpu.CompilerParams(dimension_semantics=("arbitrary",),
                                             vmem_limit_bytes=VMEM_LIMIT),
        name="outproj_mlp",
    )(x2d, att, o, lw["woa"], lw["wob"], lw["post_mix_g"], lw["pre_mlp_g"], lw["post_mlp_g"],
      lw["wup"], lw["wdn"])


def _layer_weights(l, att_w, conv_ch, gdn_vw, n_gdn_heads, w_in, conv_w, idx_k_norm_g, idx_k_norm_b,
                   gdn_a_log, gdn_dt_bias, gdn_norm_g, w_out, pre_mix_g, post_mix_g, pre_mlp_g,
                   post_mlp_g, w_mlp_up, w_mlp_down):
    idx_w = N_IDX_HEADS * IDX_DIM
    o_ik = 3 * att_w + idx_w
    o_iw = o_ik + IDX_DIM
    o_qkv = o_iw + N_IDX_HEADS
    o_z = o_qkv + conv_ch
    o_gb = o_z + gdn_vw
    o_ga = o_gb + n_gdn_heads
    wi = w_in[l]
    assert wi.shape[1] == o_ga + n_gdn_heads
    d = wi.shape[0]
    n_small = IDX_DIM + N_IDX_HEADS + 2 * n_gdn_heads
    ws = jnp.concatenate([wi[:, o_ik:o_qkv], wi[:, o_gb:], jnp.zeros((d, LANES - n_small), wi.dtype)], axis=1)
    pad_lanes = lambda v, at: jnp.zeros((1, LANES), F32).at[0, at:at + v.shape[0]].set(v.astype(F32))
    smul = jnp.ones((1, LANES), F32).at[0, SM_IW:SM_IW + N_IDX_HEADS].set(N_IDX_HEADS ** -0.5 * IDX_DIM ** -0.5)
    g_lane0 = SM_GB + n_gdn_heads
    vec = lambda v: v[l].astype(F32)[None, :]
    return {
        "wa": wi[:, :o_ik].astype(BF16),
        "wqkv": wi[:, o_qkv:o_z].astype(BF16),
        "wz": wi[:, o_z:o_gb].astype(BF16),
        "ws": ws.astype(BF16),
        "lng": pad_lanes(idx_k_norm_g[l], 0),
        "lnb": pad_lanes(idx_k_norm_b[l], 0),
        "smul": smul,
        "conv_w": jnp.zeros((SUBLANES, conv_ch), F32).at[:CONV_W].set(conv_w[l].astype(F32)),
        "alog": pad_lanes(gdn_a_log[l], g_lane0),
        "dtb": pad_lanes(gdn_dt_bias[l], g_lane0),
        "gdn_norm_g": vec(gdn_norm_g),
        "woa": w_out[l, :att_w].astype(BF16),
        "wob": w_out[l, att_w:].astype(BF16),
        "pre_mix_g": vec(pre_mix_g), "post_mix_g": vec(post_mix_g),
        "pre_mlp_g": vec(pre_mlp_g), "post_mlp_g": vec(post_mlp_g),
        "wup": w_mlp_up[l].astype(BF16),
        "wdn": w_mlp_down[l].astype(BF16),
    }


def _pick_tile(n, prefs):
    for t in prefs:
        if n % t == 0:
            return t
    return n


def kernel(x_prompt, x_sample, cache_k, cache_v, cache_idx_k, page_table, state_gdn, state_conv, w_in, conv_w, idx_k_norm_g, idx_k_norm_b, gdn_a_log, gdn_dt_bias, gdn_norm_g, w_out, pre_mix_g, post_mix_g, pre_mlp_g, post_mlp_g, w_mlp_up, w_mlp_down):
    b, s, d = x_prompt.shape
    bd, t, _ = x_sample.shape
    depth, n_pool, page, n_att_heads, head_dim = cache_k.shape
    n_pages = page_table.shape[1]
    past = n_pages * page
    n_gdn_heads, dk, dv = state_gdn.shape[2:]
    conv_ch = state_conv.shape[-1]
    att_w = n_att_heads * head_dim
    gdn_vw = n_gdn_heads * dv
    assert head_dim == HEAD_DIM and dk == GDN_DK and dv == GDN_DV and cache_idx_k.shape[-1] == IDX_DIM
    assert conv_ch == 2 * n_gdn_heads * dk + gdn_vw and state_conv.shape[2] == CONV_W - 1
    assert t <= SUBLANES and page == LANES

    tab_p = _rope_tables(jnp.arange(s))
    tab_s = tuple(jnp.tile(tb, (bd, 1)) for tb in _rope_tables(past + jnp.arange(t)))
    topk_p = min(TOPK_MAX, s // 4)
    topk_s = min(TOPK_MAX, (past + t) // 4)
    hist = CONV_W - 1

    tm_p = _pick_tile(b * s, (256, 128, 64, 32, 16, 8))
    tm_s = _pick_tile(bd * t, (256, 128, 64, 32, 16, 8))
    tq = _pick_tile(s, (512, 256, 128))
    tk = _pick_tile(s, (512, 256, 128))
    tt_p = _pick_tile(s, (256, 128, 64))
    chunk_p = min(GDN_CHUNK, tt_p)
    chunk_s = 16
    sb_gdn = _pick_tile(bd, (4, 2, 1))
    sb_sel = _pick_tile(bd, (8, 4, 2))
    pg_sel = _pick_tile(n_pages, (8, 4, 2, 1))
    pg_att = _pick_tile(n_pages, (32, 16, 8, 4, 2, 1))

    yp = x_prompt.reshape(b * s, d)
    ys = x_sample.reshape(bd * t, d)
    outs_p, outs_s = [], []
    for l in range(depth):
        lw = _layer_weights(l, att_w, conv_ch, gdn_vw, n_gdn_heads, w_in, conv_w, idx_k_norm_g, idx_k_norm_b,
                            gdn_a_log, gdn_dt_bias, gdn_norm_g, w_out, pre_mix_g, post_mix_g, pre_mlp_g,
                            post_mlp_g, w_mlp_up, w_mlp_down)
        (q, kf, vf, kb, vb, vt, iq, ikf, ikb, qkv, z, sm) = _inproj(
            yp, tab_p, lw, att_w=att_w, conv_ch=conv_ch, gdn_vw=gdn_vw, tm=tm_p)
        r3 = lambda a: a.reshape(b, s, a.shape[-1])
        att = _prompt_attention(r3(q), r3(iq), r3(sm), r3(kb), vt, r3(ikb), tq=tq, tk=tk, topk=topk_p)
        o, s_fin, conv_out = _gdn(
            r3(qkv), r3(z), r3(sm), jnp.zeros((b, SUBLANES, conv_ch), F32),
            jnp.zeros((b, n_gdn_heads, dk, dv), F32), lw, sb=1, tt=tt_p, chunk=chunk_p, t_valid=s)
        yp = _outmlp(yp, att.reshape(b * s, att_w), o.reshape(b * s, gdn_vw), lw, tm=tm_p, ff_chunk=1024)
        outs_p.append((kf.reshape(b, s, n_att_heads, head_dim), vf.reshape(b, s, n_att_heads, head_dim),
                       ikf.reshape(b, s, IDX_DIM), s_fin, conv_out[:, SUBLANES - hist:, :]))

        (q, kf, vf, kb, vb, vt, iq, ikf, ikb, qkv, z, sm) = _inproj(
            ys, tab_s, lw, att_w=att_w, conv_ch=conv_ch, gdn_vw=gdn_vw, tm=tm_s)
        iq_s = iq.reshape(bd, t * N_IDX_HEADS, IDX_DIM)
        w_s = sm[:, SM_IW:SM_IW + N_IDX_HEADS].reshape(bd, t * N_IDX_HEADS, 1)
        pad_new = lambda a: jnp.pad(a.reshape(bd, t, a.shape[-1]), ((0, 0), (0, LANES - t), (0, 0)))
        sel8 = _decode_select(page_table, iq_s, w_s, pad_new(ikb), jnp.swapaxes(cache_idx_k[l], 1, 2),
                              sb=sb_sel, pg=pg_sel, n_tok=t, topk=topk_s)
        q8 = jnp.pad(q.reshape(bd, t, n_att_heads, head_dim).transpose(0, 2, 1, 3),
                     ((0, 0), (0, 0), (0, SUBLANES - t), (0, 0))).reshape(bd, n_att_heads * SUBLANES, head_dim)
        att_s = _decode_attention(page_table, q8, sel8, pad_new(kb), pad_new(vb),
                                  cache_k[l].reshape(n_pool, page * n_att_heads, head_dim),
                                  cache_v[l].reshape(n_pool, page * n_att_heads, head_dim),
                                  pg=pg_att, n_tok=t, n_heads=n_att_heads)
        pad_t = lambda a: jnp.pad(a.reshape(bd, t, a.shape[-1]), ((0, 0), (0, chunk_s - t), (0, 0)))
        conv0 = jnp.pad(state_conv[l].astype(F32), ((0, 0), (SUBLANES - hist, 0), (0, 0)))
        o_s, s_fin_s, conv_out_s = _gdn(pad_t(qkv), pad_t(z), pad_t(sm), conv0, state_gdn[l].astype(F32), lw,
                                        sb=sb_gdn, tt=chunk_s, chunk=chunk_s, t_valid=t)
        ys = _outmlp(ys, att_s.reshape(bd * t, att_w), o_s[:, :t].reshape(bd * t, gdn_vw), lw,
                     tm=tm_s, ff_chunk=1024)
        outs_s.append((kf.reshape(bd, t, n_att_heads, head_dim), vf.reshape(bd, t, n_att_heads, head_dim),
                       ikf.reshape(bd, t, IDX_DIM), s_fin_s, conv_out_s[:, SUBLANES - hist:, :]))

    kp, vp, ikp, gp, cp = [jnp.stack([o_[i] for o_ in outs_p]) for i in range(5)]
    ks_, vs_, iks, gs, cs = [jnp.stack([o_[i] for o_ in outs_s]) for i in range(5)]
    return (yp.reshape(b, s, d), ys.reshape(bd, t, d), kp, vp, ikp, gp, cp, ks_, vs_, iks, gs, cs)
```

```python
import functools
import math

import jax
import jax.numpy as jnp
from jax import lax
from jax.experimental import pallas as pl
from jax.experimental.pallas import tpu as pltpu

F32 = jnp.float32
BF16 = jnp.bfloat16

HEAD_DIM = 128
IDX_DIM = 64
N_IDX_HEADS = 8
GDN_DK = 128
GDN_DV = 128
CONV_W = 4
TOPK_MAX = 256
GDN_CHUNK = 64
ROPE_THETA = 10000.0
NORM_EPS = 1e-6

LANES = 128
SUBLANES = 8
VMEM_LIMIT = 56 * 1024 * 1024
NEG_BIG = -1e30
BF16_EXACT_INT = 256
VALUE_STEPS = 28
MIN_NORMAL = 2.0 ** -126
SCORE_ROWS = 256
SM_IW = IDX_DIM
SM_GB = IDX_DIM + N_IDX_HEADS
HIGHEST = lax.Precision.HIGHEST


def _dot(a, b):
    return jnp.dot(a, b, preferred_element_type=F32)


def _dot_nt(a, b):
    return lax.dot_general(a, b, (((1,), (1,)), ((), ())), preferred_element_type=F32)


def _dot_tn(a, b):
    return lax.dot_general(a, b, (((0,), (0,)), ((), ())), preferred_element_type=F32)


def _rms(x, g):
    return x * lax.rsqrt(jnp.mean(x * x, axis=-1, keepdims=True) + NORM_EPS) * g


def _const_spec(shape):
    n = len(shape)
    return pl.BlockSpec(shape, lambda *_: (0,) * n, pipeline_mode=pl.Buffered(1))


def _lane_fold(x, op):
    r = x[:, 0:LANES]
    for j in range(1, x.shape[1] // LANES):
        r = op(r, x[:, j * LANES:(j + 1) * LANES])
    return r


def _inproj_kernel(x_ref, g_ref, wa_ref, wqkv_ref, wz_ref, ws_ref,
                   cos_ref, sin_ref, cosi_ref, sina_ref, sinb_ref, lng_ref, lnb_ref, smul_ref,
                   q_ref, kf_ref, vf_ref, kb_ref, vb_ref, vt_ref, iq_ref, ikf_ref, ikb_ref,
                   qkv_ref, z_ref, sm_ref, *, att_w, q_scale):
    x = x_ref[...]
    h = _rms(x, g_ref[...]).astype(BF16)
    a = _dot(h, wa_ref[...])
    cos = cos_ref[...]
    sin = sin_ref[...]
    n_heads = att_w // HEAD_DIM

    def rope_full(xh):
        return xh * cos + pltpu.roll(xh, HEAD_DIM // 2, axis=1) * sin

    for j in range(n_heads):
        sl = slice(j * HEAD_DIM, (j + 1) * HEAD_DIM)
        q_ref[:, sl] = (rope_full(a[:, sl]) * q_scale).astype(BF16)
        kr = rope_full(a[:, att_w + j * HEAD_DIM: att_w + (j + 1) * HEAD_DIM])
        kf_ref[pl.ds(j, x.shape[0], stride=n_heads), :] = kr
        kb_ref[:, sl] = kr.astype(BF16)
    v = a[:, 2 * att_w:3 * att_w]
    for j in range(n_heads):
        vf_ref[pl.ds(j, x.shape[0], stride=n_heads), :] = v[:, j * HEAD_DIM:(j + 1) * HEAD_DIM]
    vb_ref[...] = v.astype(BF16)
    vt_ref[0] = jnp.transpose(v).astype(BF16)

    cosi = cosi_ref[...]
    sina = sina_ref[...]
    sinb = sinb_ref[...]

    def rope_half(xh):
        return (xh * cosi + pltpu.roll(xh, LANES - IDX_DIM // 2, axis=1) * sina
                + pltpu.roll(xh, IDX_DIM // 2, axis=1) * sinb)

    idx_w = N_IDX_HEADS * IDX_DIM
    for j in range(idx_w // LANES):
        sl = slice(j * LANES, (j + 1) * LANES)
        iq_ref[:, sl] = rope_half(a[:, 3 * att_w + j * LANES: 3 * att_w + (j + 1) * LANES]).astype(BF16)

    qkv_ref[...] = _dot(h, wqkv_ref[...])
    z_ref[...] = _dot(h, wz_ref[...])

    sm = _dot(h, ws_ref[...])
    lane = lax.broadcasted_iota(jnp.int32, sm.shape, 1)
    is_ik = lane < IDX_DIM
    ikraw = jnp.where(is_ik, sm, 0.0)
    mu = jnp.sum(ikraw, axis=-1, keepdims=True) * (1.0 / IDX_DIM)
    xc = jnp.where(is_ik, sm - mu, 0.0)
    var = jnp.sum(xc * xc, axis=-1, keepdims=True) * (1.0 / IDX_DIM)
    ikn = xc * lax.rsqrt(var + NORM_EPS) * lng_ref[...] + lnb_ref[...]
    ikr = rope_half(ikn)
    ikf_ref[...] = ikr[:, :IDX_DIM]
    ikb_ref[...] = ikr[:, :IDX_DIM].astype(BF16)
    sm_ref[...] = sm * smul_ref[...]


def _rope_tables(pos):
    pos = pos.astype(F32)[:, None]
    half = HEAD_DIM // 2
    inv = jnp.power(ROPE_THETA, -jnp.arange(half, dtype=F32) / half)
    ang = pos * inv[None, :]
    c, s = jnp.cos(ang), jnp.sin(ang)
    cos = jnp.concatenate([c, c], axis=-1)
    sin = jnp.concatenate([-s, s], axis=-1)
    halfi = IDX_DIM // 2
    invi = jnp.power(ROPE_THETA, -jnp.arange(halfi, dtype=F32) / halfi)
    angi = pos * invi[None, :]
    ci, si = jnp.cos(angi), jnp.sin(angi)
    zi = jnp.zeros_like(si)
    cosi = jnp.concatenate([ci, ci, ci, ci], axis=-1)
    sina = jnp.concatenate([-si, zi, -si, zi], axis=-1)
    sinb = jnp.concatenate([zi, si, zi, si], axis=-1)
    return cos, sin, cosi, sina, sinb


def _inproj(x2d, tables, lw, *, att_w, conv_ch, gdn_vw, tm):
    n, d = x2d.shape
    idx_w = N_IDX_HEADS * IDX_DIM
    rt = tables[0].shape[0]
    assert n % tm == 0 and rt % tm == 0 and n % rt == 0
    nt = rt // tm
    n_heads = att_w // HEAD_DIM
    wa_w = 3 * att_w + idx_w
    row = lambda w: pl.BlockSpec((tm, w), lambda i: (i, 0))
    tab = pl.BlockSpec((tm, LANES), lambda i: (i % nt, 0))
    out_shapes = (
        jax.ShapeDtypeStruct((n, att_w), BF16),
        jax.ShapeDtypeStruct((n * n_heads, HEAD_DIM), F32),
        jax.ShapeDtypeStruct((n * n_heads, HEAD_DIM), F32),
        jax.ShapeDtypeStruct((n, att_w), BF16),
        jax.ShapeDtypeStruct((n, att_w), BF16),
        jax.ShapeDtypeStruct((n // rt, att_w, rt), BF16),
        jax.ShapeDtypeStruct((n, idx_w), BF16),
        jax.ShapeDtypeStruct((n, IDX_DIM), F32),
        jax.ShapeDtypeStruct((n, IDX_DIM), BF16),
        jax.ShapeDtypeStruct((n, conv_ch), F32),
        jax.ShapeDtypeStruct((n, gdn_vw), F32),
        jax.ShapeDtypeStruct((n, LANES), F32),
    )
    vt_spec = pl.BlockSpec((1, att_w, tm), lambda i: (i // nt, 0, i % nt))
    head_rows = pl.BlockSpec((tm * n_heads, HEAD_DIM), lambda i: (i, 0))
    out_specs = (row(att_w), head_rows, head_rows, row(att_w), row(att_w), vt_spec, row(idx_w),
                 row(IDX_DIM), row(IDX_DIM), row(conv_ch), row(gdn_vw), row(LANES))
    in_specs = [row(d), _const_spec((1, d)), _const_spec((d, wa_w)), _const_spec((d, conv_ch)),
                _const_spec((d, gdn_vw)), _const_spec((d, LANES)),
                tab, tab, tab, tab, tab,
                _const_spec((1, LANES)), _const_spec((1, LANES)), _const_spec((1, LANES))]
    return pl.pallas_call(
        functools.partial(_inproj_kernel, att_w=att_w, q_scale=HEAD_DIM ** -0.5),
        grid=(n // tm,),
        in_specs=in_specs, out_specs=out_specs, out_shape=out_shapes,
        compiler_params=pltpu.CompilerParams(dimension_semantics=("arbitrary",),
                                             vmem_limit_bytes=VMEM_LIMIT),
        name="inproj",
    )(x2d, lw["pre_mix_g"], lw["wa"], lw["wqkv"], lw["wz"], lw["ws"], *tables,
      lw["lng"], lw["lnb"], lw["smul"])


def _tie_break_by_index_search(count, rewrite, col_limit):
    def break_ties(lo, hi, want, tied):
        def in_tie(x):
            return jnp.logical_and(x >= lo, x < hi)

        def jstep(_, c):
            jlo, jhi = c
            jm = jnp.floor((jlo + jhi) * 0.5)
            cnt = count(lambda x, col: jnp.logical_and(in_tie(x), col <= jm))
            ok = cnt >= want
            return jnp.where(ok, jlo, jm), jnp.where(ok, jm, jhi)

        n_it = max(1, math.ceil(math.log2(col_limit + 1)))
        j0 = (jnp.full_like(lo, -1.0), jnp.full_like(lo, float(col_limit - 1)))
        _, jcut = lax.fori_loop(0, n_it, jstep, j0)
        rewrite(lambda x, col: jnp.where(
            jnp.logical_and(tied, jnp.logical_and(in_tie(x), col > jcut)), -jnp.inf, x))
    return break_ties


def _select_threshold(count, break_ties, n_valid, row_max, row_min, topk):
    kf = float(topk)
    flip = jnp.int32(0x7FFFFFFF)

    def to_key(x):
        b = lax.bitcast_convert_type(jnp.where(x == 0.0, 0.0, x), jnp.int32)
        return jnp.where(b < 0, b ^ flip, b)

    def from_key(k):
        min_normal = jnp.int32(0x00800000)
        k = jnp.where(jnp.logical_and(k > 0, k < min_normal), min_normal, k)
        return lax.bitcast_convert_type(jnp.where(k < 0, k ^ flip, k), F32)

    need = n_valid > kf
    spread = jnp.abs(row_max) * (2.0 ** -20) + 2.0 ** -100
    c_ge0 = count(lambda x, col: x >= 0.0)
    c_gt0 = count(lambda x, col: x > 0.0)
    non_neg = jnp.logical_and(need, c_ge0 >= kf)
    at_zero = jnp.logical_and(non_neg, c_gt0 < kf)
    negative = jnp.logical_and(need, c_ge0 < kf)
    lo0 = jnp.where(non_neg, 0.0, jnp.where(need, row_min, NEG_BIG))
    c_lo0 = jnp.where(non_neg, c_ge0, n_valid)
    hi0 = jnp.where(at_zero, MIN_NORMAL, jnp.where(negative, 0.0, row_max + spread))
    c_hi0 = jnp.where(at_zero, c_gt0, jnp.where(negative, c_ge0, 0.0))
    open0 = jnp.logical_and(need, jnp.logical_not(jnp.logical_or(at_zero, c_lo0 == kf)))
    done0 = jnp.where(open0, 0.0, 1.0)

    def narrow(c, mid, stuck, cnt):
        lo, hi, c_lo, c_hi, done = c
        ge = cnt >= kf
        upd = jnp.logical_and(done < 0.5, jnp.logical_not(stuck))
        up_lo = jnp.logical_and(upd, ge)
        up_hi = jnp.logical_and(upd, jnp.logical_not(ge))
        fin = jnp.logical_or(stuck, jnp.logical_and(upd, cnt == kf))
        return (jnp.where(up_lo, mid, lo), jnp.where(up_hi, mid, hi), jnp.where(up_lo, cnt, c_lo),
                jnp.where(up_hi, cnt, c_hi), jnp.where(fin, 1.0, done))

    def value_step(c):
        it, st = c
        lo, hi = st[0], st[1]
        mid = 0.5 * lo + 0.5 * hi
        stuck = jnp.logical_or(mid <= lo, mid >= hi)
        return it + 1, narrow(st, mid, stuck, count(lambda x, col: x >= mid))

    _, (lo, hi, c_lo, c_hi, done) = lax.while_loop(
        lambda c: jnp.logical_and(c[0] < VALUE_STEPS, jnp.min(c[1][4]) < 0.5), value_step,
        (jnp.int32(0), (lo0, hi0, c_lo0, c_hi0, done0)))

    def key_step(st):
        klo, khi = st[0], st[1]
        mid = (klo & khi) + ((klo ^ khi) >> 1)
        return narrow(st, mid, mid <= klo, count(lambda x, col: to_key(x) >= mid))

    klo, khi, c_lo, c_hi, _ = lax.while_loop(
        lambda st: jnp.min(st[4]) < 0.5, key_step, (to_key(lo), to_key(hi), c_lo, c_hi, done))
    lo, hi = from_key(klo), from_key(khi)

    tied = jnp.logical_and(need, c_lo > kf)

    @pl.when(jnp.max(jnp.where(tied, 1.0, 0.0)) > 0.5)
    def _():
        break_ties(lo, hi, kf - c_hi, tied)

    return lo


def _pattn_kernel(q_ref, iq_ref, sm_ref, k_ref, vt_ref, ik_ref, o_ref, sc_ref, m_ref, l_ref, acc_ref,
                  *, tq, tk, topk, n_heads):
    qi = pl.program_id(1)
    t0 = qi * tq
    nkb = (t0 + tq + tk - 1) // tk
    sub_tiles = tk // SUBLANES
    q_t = t0 + lax.broadcasted_iota(jnp.int32, (1, tq), 1)
    smt = jnp.transpose(sm_ref[0])
    w_rows = [smt[SM_IW + h:SM_IW + h + 1, :] for h in range(N_IDX_HEADS)]
    iq = iq_ref[0]
    iq_heads = [iq[:, h * IDX_DIM:(h + 1) * IDX_DIM] for h in range(N_IDX_HEADS)]
    sub_iota = lax.broadcasted_iota(jnp.int32, (SUBLANES, tq), 0)

    def sub_fold(x, op):
        r = x[0:SUBLANES]
        for j in range(1, x.shape[0] // SUBLANES):
            r = op(r, x[j * SUBLANES:(j + 1) * SUBLANES])
        return r

    def rep(row):
        return jnp.broadcast_to(row, (SUBLANES, tq))

    def score_block(kb, carry):
        mx, mn = carry
        k0 = pl.multiple_of(kb * tk, tk)
        for c in range(tk // SCORE_ROWS):
            r0 = k0 + c * SCORE_ROWS
            ikc = ik_ref[0, pl.ds(r0, SCORE_ROWS), :]
            acc = jnp.zeros((SCORE_ROWS, tq), F32)
            for h in range(N_IDX_HEADS):
                acc = acc + w_rows[h] * jnp.maximum(_dot_nt(ikc, iq_heads[h]), 0.0)
            key = r0 + lax.broadcasted_iota(jnp.int32, (SCORE_ROWS, 1), 0)
            valid = key <= q_t
            sc_ref[pl.ds(r0, SCORE_ROWS), :] = jnp.where(valid, acc, -jnp.inf)
            mx = jnp.maximum(mx, sub_fold(jnp.where(valid, acc, -jnp.inf), jnp.maximum))
            mn = jnp.minimum(mn, sub_fold(jnp.where(valid, acc, jnp.inf), jnp.minimum))
        return mx, mn

    mx, mn = lax.fori_loop(0, nkb, score_block,
                           (jnp.full((SUBLANES, tq), -jnp.inf, F32), jnp.full((SUBLANES, tq), jnp.inf, F32)))
    row_max = rep(jnp.max(mx, axis=0, keepdims=True))
    row_min = rep(jnp.min(mn, axis=0, keepdims=True))
    n_valid = rep((q_t + 1).astype(F32))

    def count(pred):
        def body(kb, acc):
            k0 = pl.multiple_of(kb * tk, tk)
            x = sc_ref[pl.ds(k0, tk), :]
            for j in range(sub_tiles):
                hit = pred(x[j * SUBLANES:(j + 1) * SUBLANES], (k0 + j * SUBLANES + sub_iota).astype(F32))
                acc = acc + jnp.where(hit, 1.0, 0.0)
            return acc
        acc = lax.fori_loop(0, nkb, body, jnp.zeros((SUBLANES, tq), F32))
        return rep(jnp.sum(acc, axis=0, keepdims=True))

    def break_ties(lo, hi, want, tied):
        def body(kb, seen):
            k0 = pl.multiple_of(kb * tk, tk)
            x = sc_ref[pl.ds(k0, tk), :]
            out = []
            for j in range(sub_tiles):
                xs = x[j * SUBLANES:(j + 1) * SUBLANES]
                t = jnp.where(tied, jnp.where(xs >= lo, jnp.where(xs < hi, 1.0, 0.0), 0.0), 0.0)
                upto = t
                for sh in (1, 2, 4):
                    upto = upto + jnp.where(sub_iota >= sh, pltpu.roll(upto, sh, axis=0), 0.0)
                out.append(jnp.where(t > 0.5, jnp.where(seen + upto <= want, xs, -jnp.inf), xs))
                seen = seen + rep(upto[SUBLANES - 1:SUBLANES, :])
            sc_ref[pl.ds(k0, tk), :] = jnp.concatenate(out, axis=0)
            return seen
        lax.fori_loop(0, nkb, body, jnp.zeros((SUBLANES, tq), F32))

    lo = _select_threshold(count, break_ties, n_valid, row_max, row_min, topk)
    tile_rows = lambda v8: jnp.concatenate([v8] * sub_tiles, axis=0)
    lo_t = tile_rows(lo)

    def masked_scores(kb, h, bias):
        k0 = pl.multiple_of(kb * tk, tk)
        sl = slice(h * HEAD_DIM, (h + 1) * HEAD_DIM)
        return _dot_nt(k_ref[0, pl.ds(k0, tk), sl], q_ref[0, :, sl]) + bias

    m_ref[...] = jnp.full(m_ref.shape, NEG_BIG, F32)
    l_ref[...] = jnp.zeros_like(l_ref)
    acc_ref[...] = jnp.zeros_like(acc_ref)

    def pv_block(kb, carry):
        k0 = pl.multiple_of(kb * tk, tk)
        bias = jnp.where(sc_ref[pl.ds(k0, tk), :] >= lo_t, 0.0, NEG_BIG)
        ss = [masked_scores(kb, h, bias) for h in range(n_heads)]
        hss = [slice(h * SUBLANES, (h + 1) * SUBLANES) for h in range(n_heads)]
        m_old = [m_ref[hs, :] for hs in hss]
        m_new = [jnp.maximum(m_old[h], rep(jnp.max(sub_fold(ss[h], jnp.maximum), axis=0, keepdims=True)))
                 for h in range(n_heads)]
        ps = [jnp.exp(ss[h] - tile_rows(m_new[h])) for h in range(n_heads)]
        for h in range(n_heads):
            sl = slice(h * HEAD_DIM, (h + 1) * HEAD_DIM)
            alpha = jnp.exp(m_old[h] - m_new[h])
            m_ref[hss[h], :] = m_new[h]
            l_ref[hss[h], :] = alpha * l_ref[hss[h], :] + sub_fold(ps[h], jnp.add)
            acc_ref[sl, :] = (alpha[0:1, :] * acc_ref[sl, :]
                              + _dot(vt_ref[0, sl, pl.ds(k0, tk)], ps[h].astype(BF16)))
        return carry
    lax.fori_loop(0, nkb, pv_block, 0)

    for h in range(n_heads):
        sl = slice(h * HEAD_DIM, (h + 1) * HEAD_DIM)
        l_row = jnp.sum(l_ref[h * SUBLANES:(h + 1) * SUBLANES, :], axis=0, keepdims=True)
        o_ref[0, :, sl] = jnp.transpose(acc_ref[sl, :] / l_row).astype(o_ref.dtype)


def _prompt_attention(q, iq, sm, kb, vt, ikb, *, tq, tk, topk):
    b, s, att_w = q.shape
    idx_w = iq.shape[-1]
    n_heads = att_w // HEAD_DIM
    assert s % tq == 0 and s % tk == 0 and tk % tq == 0 and tq % LANES == 0
    blk = lambda w: pl.BlockSpec((1, tq, w), lambda bi, qi: (bi, qi, 0))
    full = lambda r, w: pl.BlockSpec((1, r, w), lambda bi, qi: (bi, 0, 0), pipeline_mode=pl.Buffered(1))
    return pl.pallas_call(
        functools.partial(_pattn_kernel, tq=tq, tk=tk, topk=topk, n_heads=n_heads),
        grid=(b, s // tq),
        in_specs=[blk(att_w), blk(idx_w), blk(LANES), full(s, att_w), full(att_w, s), full(s, IDX_DIM)],
        out_specs=blk(att_w),
        out_shape=jax.ShapeDtypeStruct((b, s, att_w), BF16),
        scratch_shapes=[pltpu.VMEM((s, tq), F32), pltpu.VMEM((n_heads * SUBLANES, tq), F32),
                        pltpu.VMEM((n_heads * SUBLANES, tq), F32), pltpu.VMEM((att_w, tq), F32)],
        compiler_params=pltpu.CompilerParams(dimension_semantics=("arbitrary", "arbitrary"),
                                             vmem_limit_bytes=VMEM_LIMIT),
        name="prompt_attention",
    )(q, iq, sm, kb, vt, ikb)


def _dsel_kernel(pt_ref, iq_ref, w_ref, iknew_ref, *rest, sb, pg, n_tok, past, topk):
    pages = rest[:sb * pg]
    sel_ref = rest[sb * pg]
    sc_ref = rest[sb * pg + 1]
    c = pl.program_id(1)
    n_c = pl.num_programs(1)
    page = pages[0].shape[2]
    rows, l_pad = sc_ref.shape

    def token_rows(s):
        return [jnp.sum(s[t * N_IDX_HEADS:(t + 1) * N_IDX_HEADS, :], axis=0, keepdims=True)
                for t in range(n_tok)]

    @pl.when(c == 0)
    def _():
        sc_ref[...] = jnp.full(sc_ref.shape, -jnp.inf, F32)

    for j in range(sb):
        iq = iq_ref[j]
        w = w_ref[j]
        kt = jnp.concatenate([pages[j * pg + p][0] for p in range(pg)], axis=1).astype(BF16)
        tr = token_rows(jnp.maximum(_dot(iq, kt), 0.0) * w)
        k0 = pl.multiple_of(c * pg * page, pg * page)
        for t in range(n_tok):
            sc_ref[j * n_tok + t:j * n_tok + t + 1, pl.ds(k0, pg * page)] = tr[t]

    @pl.when(c == n_c - 1)
    def _():
        col = lax.broadcasted_iota(jnp.int32, (1, LANES), 1)
        for j in range(sb):
            tr = token_rows(jnp.maximum(_dot_nt(iq_ref[j], iknew_ref[j]), 0.0) * w_ref[j])
            for t in range(n_tok):
                sc_ref[j * n_tok + t:j * n_tok + t + 1, past:past + LANES] = jnp.where(col <= t, tr[t], -jnp.inf)
        x = sc_ref[...]
        rep = lambda v: jnp.broadcast_to(v, (rows, LANES))
        row_max = rep(jnp.max(x, axis=-1, keepdims=True))
        row_min = rep(jnp.min(jnp.where(x > -jnp.inf, x, jnp.inf), axis=-1, keepdims=True))
        r = lax.broadcasted_iota(jnp.int32, (rows, LANES), 0).astype(F32)
        tok = r - n_tok * jnp.floor(r * (1.0 / n_tok))
        n_valid = past + 1.0 + tok
        n_blocks = l_pad // LANES
        assert n_blocks <= BF16_EXACT_INT
        ones_b = jnp.ones((LANES, LANES), BF16)
        lane_iota = lax.broadcasted_iota(jnp.int32, (rows, LANES), 1)

        def count(pred):
            def body(kb, acc):
                c0 = pl.multiple_of(kb * LANES, LANES)
                hit = pred(sc_ref[:, pl.ds(c0, LANES)], (c0 + lane_iota).astype(F32))
                return acc + jnp.where(hit, 1.0, 0.0)
            acc = lax.fori_loop(0, n_blocks, body, jnp.zeros((rows, LANES), F32))
            return _dot(acc.astype(BF16), ones_b)

        def rewrite(fn):
            def body(kb, carry):
                c0 = pl.multiple_of(kb * LANES, LANES)
                sc_ref[:, pl.ds(c0, LANES)] = fn(sc_ref[:, pl.ds(c0, LANES)], (c0 + lane_iota).astype(F32))
                return carry
            lax.fori_loop(0, n_blocks, body, 0)

        lo = _select_threshold(count, _tie_break_by_index_search(count, rewrite, l_pad),
                               n_valid, row_max, row_min, topk)

        def emit(kb, carry):
            c0 = pl.multiple_of(kb * LANES, LANES)
            x = jnp.where(sc_ref[:, pl.ds(c0, LANES)] >= lo, 1.0, 0.0)
            for j in range(sb):
                sel_ref[j, 0:n_tok, pl.ds(c0, LANES)] = x[j * n_tok:(j + 1) * n_tok]
                sel_ref[j, n_tok:SUBLANES, pl.ds(c0, LANES)] = jnp.zeros((SUBLANES - n_tok, LANES), F32)
            return carry
        lax.fori_loop(0, l_pad // LANES, emit, 0)


def _decode_select(page_table, iq, w, iknew, cache_idx_t, *, sb, pg, n_tok, topk):
    bd, n_pages = page_table.shape
    _, di, page = cache_idx_t.shape
    past = n_pages * page
    assert n_pages % pg == 0 and bd % sb == 0 and (sb * n_tok) % SUBLANES == 0
    l_pad = past + LANES
    rows = iq.shape[1]

    def page_spec(j, p):
        return pl.BlockSpec((1, di, page), lambda g, c, pt: (pt[g * sb + j, c * pg + p], 0, 0))

    grid_spec = pltpu.PrefetchScalarGridSpec(
        num_scalar_prefetch=1,
        grid=(bd // sb, n_pages // pg),
        in_specs=[pl.BlockSpec((sb, rows, di), lambda g, c, pt: (g, 0, 0)),
                  pl.BlockSpec((sb, rows, 1), lambda g, c, pt: (g, 0, 0)),
                  pl.BlockSpec((sb, LANES, di), lambda g, c, pt: (g, 0, 0))]
                 + [page_spec(j, p) for j in range(sb) for p in range(pg)],
        out_specs=pl.BlockSpec((sb, SUBLANES, l_pad), lambda g, c, pt: (g, 0, 0)),
        scratch_shapes=[pltpu.VMEM((sb * n_tok, l_pad), F32)],
    )
    return pl.pallas_call(
        functools.partial(_dsel_kernel, sb=sb, pg=pg, n_tok=n_tok, past=past, topk=topk),
        grid_spec=grid_spec,
        out_shape=jax.ShapeDtypeStruct((bd, SUBLANES, l_pad), F32),
        compiler_params=pltpu.CompilerParams(dimension_semantics=("arbitrary", "arbitrary"),
                                             vmem_limit_bytes=VMEM_LIMIT),
        name="decode_select",
    )(page_table, iq, w, iknew, *([cache_idx_t] * (sb * pg)))


def _dattn_kernel(pt_ref, q_ref, sel_ref, knew_ref, vnew_ref, *rest, pg, n_tok, n_heads, past):
    kpages = rest[:pg]
    vpages = rest[pg:2 * pg]
    o_ref = rest[2 * pg]
    m_ref, l_ref, acc_ref = rest[2 * pg + 1:]
    c = pl.program_id(1)
    n_c = pl.num_programs(1)
    page = kpages[0].shape[1] // n_heads
    qh = [q_ref[0, h * SUBLANES:(h + 1) * SUBLANES, :] for h in range(n_heads)]

    @pl.when(c == 0)
    def _():
        m_ref[...] = jnp.full(m_ref.shape, NEG_BIG, F32)
        l_ref[...] = jnp.zeros_like(l_ref)
        acc_ref[...] = jnp.zeros_like(acc_ref)

    def update(k_of, v_of, n_blk, sel):
        s = jnp.concatenate(
            [jnp.concatenate([_dot_nt(qh[h], k_of(p, h)) for p in range(n_blk)], axis=1)
             for h in range(n_heads)], axis=0)
        s = jnp.where(jnp.concatenate([sel] * n_heads, axis=0) > 0.5, s, NEG_BIG)
        m_old = m_ref[...]
        m_new = jnp.maximum(m_old, jnp.max(s, axis=-1, keepdims=True))
        pr = jnp.exp(s - m_new).astype(BF16)
        alpha = jnp.exp(m_old - m_new)
        l_ref[...] = alpha * l_ref[...] + jnp.sum(pr.astype(F32), axis=-1, keepdims=True)
        pv = []
        for h in range(n_heads):
            a = jnp.zeros((SUBLANES, HEAD_DIM), F32)
            for p in range(n_blk):
                a = a + _dot(pr[h * SUBLANES:(h + 1) * SUBLANES, p * page:(p + 1) * page], v_of(p, h))
            pv.append(a)
        acc_ref[...] = alpha * acc_ref[...] + jnp.concatenate(pv, axis=0)
        m_ref[...] = m_new

    head_rows = lambda ref, h: ref[0, pl.ds(h, page, stride=n_heads), :].astype(BF16)
    k0 = pl.multiple_of(c * pg * page, pg * page)
    update(lambda p, h: head_rows(kpages[p], h), lambda p, h: head_rows(vpages[p], h), pg,
           sel_ref[0, :, pl.ds(k0, pg * page)])

    @pl.when(c == n_c - 1)
    def _():
        new = lambda ref, h: ref[0, :, h * HEAD_DIM:(h + 1) * HEAD_DIM]
        update(lambda p, h: new(knew_ref, h), lambda p, h: new(vnew_ref, h), 1,
               sel_ref[0, :, past:past + LANES])
        inv_l = 1.0 / l_ref[...]
        out = acc_ref[...] * inv_l
        for h in range(n_heads):
            o_ref[0, :, h * HEAD_DIM:(h + 1) * HEAD_DIM] = out[h * SUBLANES:h * SUBLANES + n_tok, :].astype(o_ref.dtype)


def _decode_attention(page_table, q8, sel, knew, vnew, cache_k, cache_v, *, pg, n_tok, n_heads):
    bd, n_pages = page_table.shape
    _, page_rows, hd = cache_k.shape
    page = page_rows // n_heads
    past = n_pages * page
    l_pad = sel.shape[-1]
    att_w = n_heads * hd
    rows = n_heads * SUBLANES
    assert n_pages % pg == 0 and page == LANES

    def page_spec(p):
        return pl.BlockSpec((1, page_rows, hd), lambda b, c, pt: (pt[b, c * pg + p], 0, 0))

    per_seq = lambda r, w: pl.BlockSpec((1, r, w), lambda b, c, pt: (b, 0, 0))
    grid_spec = pltpu.PrefetchScalarGridSpec(
        num_scalar_prefetch=1,
        grid=(bd, n_pages // pg),
        in_specs=[per_seq(rows, hd), per_seq(SUBLANES, l_pad), per_seq(LANES, att_w), per_seq(LANES, att_w)]
                 + [page_spec(p) for p in range(pg)] * 2,
        out_specs=per_seq(n_tok, att_w),
        scratch_shapes=[pltpu.VMEM((rows, 1), F32), pltpu.VMEM((rows, 1), F32), pltpu.VMEM((rows, hd), F32)],
    )
    return pl.pallas_call(
        functools.partial(_dattn_kernel, pg=pg, n_tok=n_tok, n_heads=n_heads, past=past),
        grid_spec=grid_spec,
        out_shape=jax.ShapeDtypeStruct((bd, n_tok, att_w), BF16),
        compiler_params=pltpu.CompilerParams(dimension_semantics=("arbitrary", "arbitrary"),
                                             vmem_limit_bytes=VMEM_LIMIT),
        name="decode_attention",
    )(page_table, q8, sel, knew, vnew, *([cache_k] * pg), *([cache_v] * pg))


def _gdn_kernel(x_ref, z_ref, sm_ref, conv0_ref, s0_ref, cw_ref, alog_ref, dtb_ref, ng_ref,
                o_ref, sfin_ref, convout_ref,
                xbuf, st_ref, q_s, k_s, v_s, beta_s, g_s,
                *, sb, tt, chunk, n_heads, t_valid, n_steps):
    step = pl.program_id(1)
    hist = CONV_W - 1
    qk_w = n_heads * GDN_DK

    @pl.when(step == 0)
    def _():
        xbuf[:, 0:SUBLANES, :] = conv0_ref[...]
        st_ref[...] = s0_ref[...]

    @pl.when(step > 0)
    def _():
        xbuf[:, 0:SUBLANES, :] = xbuf[:, tt:tt + SUBLANES, :]

    cw = cw_ref[...]
    t_glob = step * tt + lax.broadcasted_iota(jnp.int32, (tt, 1), 0)
    live = t_glob < t_valid

    def l2n(xh):
        return xh * lax.rsqrt(jnp.sum(xh * xh, axis=-1, keepdims=True) + NORM_EPS)

    for j in range(sb):
        xbuf[j, SUBLANES:SUBLANES + tt, :] = x_ref[j]
        conv = cw[hist:hist + 1, :] * xbuf[j, SUBLANES:SUBLANES + tt, :]
        for i in range(hist):
            conv = conv + cw[i:i + 1, :] * xbuf[j, SUBLANES - hist + i:SUBLANES - hist + i + tt, :]
        act = conv * jax.nn.sigmoid(conv)
        for h in range(n_heads):
            sl = slice(h * GDN_DK, (h + 1) * GDN_DK)
            q_s[j, :, sl] = l2n(act[:, sl]) * (GDN_DK ** -0.5)
            k_s[j, :, sl] = l2n(act[:, qk_w + h * GDN_DK: qk_w + (h + 1) * GDN_DK])
        v_s[j] = act[:, 2 * qk_w:]
        smv = sm_ref[j]
        beta_s[j] = jnp.where(live, jax.nn.sigmoid(smv), 0.0)
        g_s[j] = jnp.where(live, -jnp.exp(alog_ref[...]) * jax.nn.softplus(smv + dtb_ref[...]), 0.0)

    ii = lax.broadcasted_iota(jnp.int32, (chunk, chunk), 0)
    jj = lax.broadcasted_iota(jnp.int32, (chunk, chunk), 1)
    tril = ii >= jj
    strict = ii > jj
    tril_f = jnp.where(tril, 1.0, 0.0)
    eye = jnp.where(ii == jj, 1.0, 0.0)
    lane8 = lax.broadcasted_iota(jnp.int32, (SUBLANES, LANES), 1)
    row8 = lax.broadcasted_iota(jnp.int32, (SUBLANES, LANES), 0)
    g_lane0 = SM_GB + n_heads
    pick_g = jnp.where(lane8 == g_lane0 + row8, 1.0, 0.0)
    n_levels = int(math.log2(chunk))
    ng = ng_ref[...]

    n_chunks = tt // chunk
    tiles = [(j, ci) for j in range(sb) for ci in range(n_chunks)]
    units = [(j, ci, h) for j, ci in tiles for h in range(n_heads)]
    rows_of = lambda ci: slice(ci * chunk, (ci + 1) * chunk)
    lanes_of = lambda h: slice(h * GDN_DK, (h + 1) * GDN_DK)
    gcs = {(j, ci): jnp.dot(tril_f, g_s[j, rows_of(ci), :], precision=HIGHEST, preferred_element_type=F32)
           for j, ci in tiles}
    gcrs = {t: lax.dot_general(pick_g, gcs[t], (((1,), (1,)), ((), ())), precision=HIGHEST,
                               preferred_element_type=F32) for t in tiles}
    gcol = {(j, ci, h): gcs[(j, ci)][:, g_lane0 + h:g_lane0 + h + 1] for j, ci, h in units}
    bcol = {(j, ci, h): beta_s[j, rows_of(ci), SM_GB + h:SM_GB + h + 1] for j, ci, h in units}
    decay = {u: jnp.exp(jnp.where(tril, gcol[u] - gcrs[u[:2]][u[2]:u[2] + 1, :], -jnp.inf)) for u in units}
    k_f = {(j, ci, h): k_s[j, rows_of(ci), lanes_of(h)] for j, ci, h in units}
    q_f = {(j, ci, h): q_s[j, rows_of(ci), lanes_of(h)] for j, ci, h in units}
    k_b = {u: k_f[u].astype(BF16) for u in units}
    kb_f = {u: k_f[u] * bcol[u] for u in units}
    lmat = {u: jnp.where(strict, _dot_nt(kb_f[u].astype(BF16), k_b[u]) * decay[u], 0.0) for u in units}
    qk = {u: (_dot_nt(q_f[u].astype(BF16), k_b[u]) * decay[u]).astype(BF16) for u in units}
    pw = {u: -lmat[u] for u in units}
    inv = {u: eye + pw[u] for u in units}
    for _ in range(n_levels - 1):
        pw = {u: _dot(pw[u].astype(BF16), pw[u].astype(BF16)) for u in units}
        inv = {u: inv[u] + _dot(inv[u].astype(BF16), pw[u].astype(BF16)) for u in units}
    inv_b = {u: inv[u].astype(BF16) for u in units}
    u_rhs = {(j, ci, h): _dot(inv_b[(j, ci, h)], (v_s[j, rows_of(ci), lanes_of(h)] * bcol[(j, ci, h)]).astype(BF16))
             for j, ci, h in units}
    w_rhs = {u: _dot(inv_b[u], (kb_f[u] * jnp.exp(gcol[u])).astype(BF16)).astype(BF16) for u in units}

    streams = [(j, h) for j in range(sb) for h in range(n_heads)]
    state = {(j, h): st_ref[j, h] for j, h in streams}
    for ci in range(n_chunks):
        cur = {(j, h): (j, ci, h) for j, h in streams}
        s_b = {sh: state[sh].astype(BF16) for sh in streams}
        ws = {sh: _dot(w_rhs[cur[sh]], s_b[sh]) for sh in streams}
        qs = {sh: _dot((q_f[cur[sh]] * jnp.exp(gcol[cur[sh]])).astype(BF16), s_b[sh]) for sh in streams}
        v_new = {sh: (u_rhs[cur[sh]] - ws[sh]).astype(BF16) for sh in streams}
        outs = {sh: qs[sh] + _dot(qk[cur[sh]], v_new[sh]) for sh in streams}
        g_last = {(j, h): gcs[(j, ci)][chunk - 1:chunk, g_lane0 + h:g_lane0 + h + 1] for j, h in streams}
        k_dec = {sh: (k_f[cur[sh]] * jnp.exp(g_last[sh] - gcol[cur[sh]])).astype(BF16) for sh in streams}
        state = {sh: state[sh] * jnp.exp(g_last[sh]) + _dot_tn(k_dec[sh], v_new[sh]) for sh in streams}
        for j, h in streams:
            zh = z_ref[j, rows_of(ci), lanes_of(h)]
            o_ref[j, rows_of(ci), lanes_of(h)] = (
                _rms(outs[(j, h)], ng) * (zh * jax.nn.sigmoid(zh))).astype(o_ref.dtype)
    for j, h in streams:
        st_ref[j, h] = state[(j, h)]

    @pl.when(step == n_steps - 1)
    def _():
        sfin_ref[...] = st_ref[...]
        last = t_valid - (n_steps - 1) * tt
        convout_ref[...] = xbuf[:, last:last + SUBLANES, :]


def _gdn(x, z, sm, conv0, s0, lw, *, sb, tt, chunk, t_valid):
    nb, t_pad, conv_ch = x.shape
    n_heads = s0.shape[1]
    vw = z.shape[-1]
    assert t_pad % tt == 0 and tt % chunk == 0 and nb % sb == 0
    n_steps = t_pad // tt
    assert 0 < t_valid - (n_steps - 1) * tt <= tt and t_valid >= CONV_W - 1
    tok = lambda w: pl.BlockSpec((sb, tt, w), lambda b, s: (b, s, 0))
    return pl.pallas_call(
        functools.partial(_gdn_kernel, sb=sb, tt=tt, chunk=chunk, n_heads=n_heads, t_valid=t_valid,
                          n_steps=n_steps),
        grid=(nb // sb, n_steps),
        in_specs=[tok(conv_ch), tok(vw), tok(LANES),
                  pl.BlockSpec((sb, SUBLANES, conv_ch), lambda b, s: (b, 0, 0)),
                  pl.BlockSpec((sb, n_heads, GDN_DK, GDN_DV), lambda b, s: (b, 0, 0, 0)),
                  pl.BlockSpec((SUBLANES, conv_ch), lambda b, s: (0, 0)),
                  pl.BlockSpec((1, LANES), lambda b, s: (0, 0)),
                  pl.BlockSpec((1, LANES), lambda b, s: (0, 0)),
                  pl.BlockSpec((1, GDN_DV), lambda b, s: (0, 0))],
        out_specs=(tok(vw),
                   pl.BlockSpec((sb, n_heads, GDN_DK, GDN_DV), lambda b, s: (b, 0, 0, 0)),
                   pl.BlockSpec((sb, SUBLANES, conv_ch), lambda b, s: (b, 0, 0))),
        out_shape=(jax.ShapeDtypeStruct((nb, t_pad, vw), BF16),
                   jax.ShapeDtypeStruct((nb, n_heads, GDN_DK, GDN_DV), F32),
                   jax.ShapeDtypeStruct((nb, SUBLANES, conv_ch), F32)),
        scratch_shapes=[pltpu.VMEM((sb, tt + 2 * SUBLANES, conv_ch), F32),
                        pltpu.VMEM((sb, n_heads, GDN_DK, GDN_DV), F32),
                        pltpu.VMEM((sb, tt, n_heads * GDN_DK), F32),
                        pltpu.VMEM((sb, tt, n_heads * GDN_DK), F32),
                        pltpu.VMEM((sb, tt, vw), F32),
                        pltpu.VMEM((sb, tt, LANES), F32),
                        pltpu.VMEM((sb, tt, LANES), F32)],
        compiler_params=pltpu.CompilerParams(dimension_semantics=("arbitrary", "arbitrary"),
                                             vmem_limit_bytes=VMEM_LIMIT),
        name="gated_delta",
    )(x, z, sm, conv0, s0, lw["conv_w"], lw["alog"], lw["dtb"], lw["gdn_norm_g"])


def _outmlp_kernel(x_ref, att_ref, o_ref, woa_ref, wob_ref, g1_ref, g2_ref, g3_ref, wup_ref, wdn_ref,
                   y_ref, *, ff_chunk):
    x = x_ref[...]
    mix = _dot(att_ref[...], woa_ref[...]) + _dot(o_ref[...], wob_ref[...])
    x1 = x + _rms(mix, g1_ref[...])
    h2 = _rms(x1, g2_ref[...]).astype(BF16)
    d_ff = wup_ref.shape[1]
    ff = jnp.zeros(x.shape, F32)
    for c in range(d_ff // ff_chunk):
        u = jnp.maximum(_dot(h2, wup_ref[:, c * ff_chunk:(c + 1) * ff_chunk]), 0.0)
        ff = ff + _dot((u * u).astype(BF16), wdn_ref[c * ff_chunk:(c + 1) * ff_chunk, :])
    y_ref[...] = x1 + _rms(ff, g3_ref[...])


def _outmlp(x2d, att, o, lw, *, tm, ff_chunk):
    n, d = x2d.shape
    att_w = att.shape[1]
    vw = o.shape[1]
    d_ff = lw["wup"].shape[1]
    assert n % tm == 0 and d_ff % ff_chunk == 0
    row = lambda w: pl.BlockSpec((tm, w), lambda i: (i, 0))
    return pl.pallas_call(
        functools.partial(_outmlp_kernel, ff_chunk=ff_chunk),
        grid=(n // tm,),
        in_specs=[row(d), row(att_w), row(vw), _const_spec((att_w, d)), _const_spec((vw, d)),
                  _const_spec((1, d)), _const_spec((1, d)), _const_spec((1, d)),
                  _const_spec((d, d_ff)), _const_spec((d_ff, d))],
        out_specs=row(d),
        out_shape=jax.ShapeDtypeStruct((n, d), F32),
        compiler_params=pltpu.CompilerParams(dimension_semantics=("arbitrary",),
                                             vmem_limit_bytes=VMEM_LIMIT),
        name="outproj_mlp",
    )(x2d, att, o, lw["woa"], lw["wob"], lw["post_mix_g"], lw["pre_mlp_g"], lw["post_mlp_g"],
      lw["wup"], lw["wdn"])


def _layer_weights(l, att_w, conv_ch, gdn_vw, n_gdn_heads, w_in, conv_w, idx_k_norm_g, idx_k_norm_b,
                   gdn_a_log, gdn_dt_bias, gdn_norm_g, w_out, pre_mix_g, post_mix_g, pre_mlp_g,
                   post_mlp_g, w_mlp_up, w_mlp_down):
    idx_w = N_IDX_HEADS * IDX_DIM
    o_ik = 3 * att_w + idx_w
    o_iw = o_ik + IDX_DIM
    o_qkv = o_iw + N_IDX_HEADS
    o_z = o_qkv + conv_ch
    o_gb = o_z + gdn_vw
    o_ga = o_gb + n_gdn_heads
    wi = w_in[l]
    assert wi.shape[1] == o_ga + n_gdn_heads
    d = wi.shape[0]
    n_small = IDX_DIM + N_IDX_HEADS + 2 * n_gdn_heads
    ws = jnp.concatenate([wi[:, o_ik:o_qkv], wi[:, o_gb:], jnp.zeros((d, LANES - n_small), wi.dtype)], axis=1)
    pad_lanes = lambda v, at: jnp.zeros((1, LANES), F32).at[0, at:at + v.shape[0]].set(v.astype(F32))
    smul = jnp.ones((1, LANES), F32).at[0, SM_IW:SM_IW + N_IDX_HEADS].set(N_IDX_HEADS ** -0.5 * IDX_DIM ** -0.5)
    g_lane0 = SM_GB + n_gdn_heads
    vec = lambda v: v[l].astype(F32)[None, :]
    return {
        "wa": wi[:, :o_ik].astype(BF16),
        "wqkv": wi[:, o_qkv:o_z].astype(BF16),
        "wz": wi[:, o_z:o_gb].astype(BF16),
        "ws": ws.astype(BF16),
        "lng": pad_lanes(idx_k_norm_g[l], 0),
        "lnb": pad_lanes(idx_k_norm_b[l], 0),
        "smul": smul,
        "conv_w": jnp.zeros((SUBLANES, conv_ch), F32).at[:CONV_W].set(conv_w[l].astype(F32)),
        "alog": pad_lanes(gdn_a_log[l], g_lane0),
        "dtb": pad_lanes(gdn_dt_bias[l], g_lane0),
        "gdn_norm_g": vec(gdn_norm_g),
        "woa": w_out[l, :att_w].astype(BF16),
        "wob": w_out[l, att_w:].astype(BF16),
        "pre_mix_g": vec(pre_mix_g), "post_mix_g": vec(post_mix_g),
        "pre_mlp_g": vec(pre_mlp_g), "post_mlp_g": vec(post_mlp_g),
        "wup": w_mlp_up[l].astype(BF16),
        "wdn": w_mlp_down[l].astype(BF16),
    }


def _pick_tile(n, prefs):
    for t in prefs:
        if n % t == 0:
            return t
    return n


def kernel(x_prompt, x_sample, cache_k, cache_v, cache_idx_k, page_table, state_gdn, state_conv, w_in, conv_w, idx_k_norm_g, idx_k_norm_b, gdn_a_log, gdn_dt_bias, gdn_norm_g, w_out, pre_mix_g, post_mix_g, pre_mlp_g, post_mlp_g, w_mlp_up, w_mlp_down):
    b, s, d = x_prompt.shape
    bd, t, _ = x_sample.shape
    depth, n_pool, page, n_att_heads, head_dim = cache_k.shape
    n_pages = page_table.shape[1]
    past = n_pages * page
    n_gdn_heads, dk, dv = state_gdn.shape[2:]
    conv_ch = state_conv.shape[-1]
    att_w = n_att_heads * head_dim
    gdn_vw = n_gdn_heads * dv
    assert head_dim == HEAD_DIM and dk == GDN_DK and dv == GDN_DV and cache_idx_k.shape[-1] == IDX_DIM
    assert conv_ch == 2 * n_gdn_heads * dk + gdn_vw and state_conv.shape[2] == CONV_W - 1
    assert t <= SUBLANES and page == LANES

    tab_p = _rope_tables(jnp.arange(s))
    tab_s = tuple(jnp.tile(tb, (bd, 1)) for tb in _rope_tables(past + jnp.arange(t)))
    topk_p = min(TOPK_MAX, s // 4)
    topk_s = min(TOPK_MAX, (past + t) // 4)
    hist = CONV_W - 1

    tm_p = _pick_tile(b * s, (256, 128, 64, 32, 16, 8))
    tm_s = _pick_tile(bd * t, (256, 128, 64, 32, 16, 8))
    tq = _pick_tile(s, (512, 256, 128))
    tk = _pick_tile(s, (512, 256, 128))
    tt_p = _pick_tile(s, (256, 128, 64))
    chunk_p = min(GDN_CHUNK, tt_p)
    chunk_s = 16
    sb_gdn = _pick_tile(bd, (4, 2, 1))
    sb_sel = _pick_tile(bd, (16, 8, 4, 2))
    pg_sel = _pick_tile(n_pages, (4, 2, 1))
    pg_att = _pick_tile(n_pages, (32, 16, 8, 4, 2, 1))

    yp = x_prompt.reshape(b * s, d)
    ys = x_sample.reshape(bd * t, d)
    outs_p, outs_s = [], []
    for l in range(depth):
        lw = _layer_weights(l, att_w, conv_ch, gdn_vw, n_gdn_heads, w_in, conv_w, idx_k_norm_g, idx_k_norm_b,
                            gdn_a_log, gdn_dt_bias, gdn_norm_g, w_out, pre_mix_g, post_mix_g, pre_mlp_g,
                            post_mlp_g, w_mlp_up, w_mlp_down)
        (q, kf, vf, kb, vb, vt, iq, ikf, ikb, qkv, z, sm) = _inproj(
            yp, tab_p, lw, att_w=att_w, conv_ch=conv_ch, gdn_vw=gdn_vw, tm=tm_p)
        r3 = lambda a: a.reshape(b, s, a.shape[-1])
        att = _prompt_attention(r3(q), r3(iq), r3(sm), r3(kb), vt, r3(ikb), tq=tq, tk=tk, topk=topk_p)
        o, s_fin, conv_out = _gdn(
            r3(qkv), r3(z), r3(sm), jnp.zeros((b, SUBLANES, conv_ch), F32),
            jnp.zeros((b, n_gdn_heads, dk, dv), F32), lw, sb=_pick_tile(b, (2, 1)), tt=tt_p, chunk=chunk_p,
            t_valid=s)
        yp = _outmlp(yp, att.reshape(b * s, att_w), o.reshape(b * s, gdn_vw), lw, tm=tm_p, ff_chunk=1024)
        outs_p.append((kf.reshape(b, s, n_att_heads, head_dim), vf.reshape(b, s, n_att_heads, head_dim),
                       ikf.reshape(b, s, IDX_DIM), s_fin, conv_out[:, SUBLANES - hist:, :]))

        (q, kf, vf, kb, vb, vt, iq, ikf, ikb, qkv, z, sm) = _inproj(
            ys, tab_s, lw, att_w=att_w, conv_ch=conv_ch, gdn_vw=gdn_vw, tm=tm_s)
        iq_s = iq.reshape(bd, t * N_IDX_HEADS, IDX_DIM)
        w_s = sm[:, SM_IW:SM_IW + N_IDX_HEADS].reshape(bd, t * N_IDX_HEADS, 1)
        pad_new = lambda a: jnp.pad(a.reshape(bd, t, a.shape[-1]), ((0, 0), (0, LANES - t), (0, 0)))
        sel8 = _decode_select(page_table, iq_s, w_s, pad_new(ikb), jnp.swapaxes(cache_idx_k[l], 1, 2),
                              sb=sb_sel, pg=pg_sel, n_tok=t, topk=topk_s)
        q8 = jnp.pad(q.reshape(bd, t, n_att_heads, head_dim).transpose(0, 2, 1, 3),
                     ((0, 0), (0, 0), (0, SUBLANES - t), (0, 0))).reshape(bd, n_att_heads * SUBLANES, head_dim)
        att_s = _decode_attention(page_table, q8, sel8, pad_new(kb), pad_new(vb),
                                  cache_k[l].reshape(n_pool, page * n_att_heads, head_dim),
                                  cache_v[l].reshape(n_pool, page * n_att_heads, head_dim),
                                  pg=pg_att, n_tok=t, n_heads=n_att_heads)
        pad_t = lambda a: jnp.pad(a.reshape(bd, t, a.shape[-1]), ((0, 0), (0, chunk_s - t), (0, 0)))
        conv0 = jnp.pad(state_conv[l].astype(F32), ((0, 0), (SUBLANES - hist, 0), (0, 0)))
        o_s, s_fin_s, conv_out_s = _gdn(pad_t(qkv), pad_t(z), pad_t(sm), conv0, state_gdn[l].astype(F32), lw,
                                        sb=sb_gdn, tt=chunk_s, chunk=chunk_s, t_valid=t)
        ys = _outmlp(ys, att_s.reshape(bd * t, att_w), o_s[:, :t].reshape(bd * t, gdn_vw), lw,
                     tm=tm_s, ff_chunk=1024)
        outs_s.append((kf.reshape(bd, t, n_att_heads, head_dim), vf.reshape(bd, t, n_att_heads, head_dim),
                       ikf.reshape(bd, t, IDX_DIM), s_fin_s, conv_out_s[:, SUBLANES - hist:, :]))

    kp, vp, ikp, gp, cp = [jnp.stack([o_[i] for o_ in outs_p]) for i in range(5)]
    ks_, vs_, iks, gs, cs = [jnp.stack([o_[i] for o_ in outs_s]) for i in range(5)]
    return (yp.reshape(b, s, d), ys.reshape(bd, t, d), kp, vp, ikp, gp, cp, ks_, vs_, iks, gs, cs)
```
